```python
import math
import jax
import jax.numpy as jnp
from jax import lax
import numpy as np

D_MODEL = 1024
BATCH = 16
SEQ = 2048
DEPTH = 2

CTX_LEN = 256
GRID_W = 64
N_MIXERS = 2
N_MOD = 6
EPS = 1e-6

SSD_D_INNER = 2 * D_MODEL
SSD_HEADDIM = 64
SSD_HEADS = SSD_D_INNER // SSD_HEADDIM
SSD_GROUPS = 4
SSD_HPG = SSD_HEADS // SSD_GROUPS
SSD_STATE = 128
SSD_CONV = 5
SSD_CHUNK = 128
SSD_BC_DIM = SSD_GROUPS * SSD_STATE
SSD_CONV_DIM = SSD_D_INNER + 2 * SSD_BC_DIM
SSD_IN_DIM = SSD_D_INNER + SSD_CONV_DIM + 2 * SSD_HEADS
DT_MIN = 1e-3
DT_MAX = 1e-1

S5_CH = 16
S5_GROUPS = D_MODEL // S5_CH
S5_STATE = 64

FFN_DIM = 2816
N_EXPERTS = 8
TOP_K = 2
EXPERT_DIM = 3584

kernel_name = 'hybrid_ssd_s5_moe_diffusion_trunk'


def rmsnorm(x, g):
    xf = x.astype(jnp.float32)
    inv = lax.rsqrt(jnp.mean(xf * xf, axis=-1, keepdims=True) + EPS)
    return (xf * inv).astype(x.dtype) * g


def modulate(h, shift, scale):
    return h * (1.0 + scale) + shift


def swiglu(h, w_in, w_out):
    g, u = jnp.split(h @ w_in, 2, axis=-1)
    return (jax.nn.silu(g) * u) @ w_out


def centred_dwconv(u, w, b):
    pad = (w.shape[0] - 1) // 2
    out = lax.conv_general_dilated(u, w[:, None, :], window_strides=(1,), padding=[(pad, pad)],
                                   dimension_numbers=('NWC', 'WIO', 'NWC'),
                                   feature_group_count=u.shape[-1])
    return out + b


def segsum(a):
    t = a.shape[-1]
    cs = jnp.cumsum(a, axis=-1)
    seg = cs[..., :, None] - cs[..., None, :]
    return jnp.where(jnp.tril(jnp.ones((t, t), dtype=bool)), seg, -jnp.inf)


def ssd_scan(xdt, da, bm, cm, h0):
    bsz, seq, g, e, p = xdt.shape
    nc = seq // SSD_CHUNK
    xc = xdt.reshape(bsz, nc, SSD_CHUNK, g, e, p)
    bc = bm.reshape(bsz, nc, SSD_CHUNK, g, -1)
    cc = cm.reshape(bsz, nc, SSD_CHUNK, g, -1)
    a = jnp.moveaxis(da.reshape(bsz, nc, SSD_CHUNK, g, e), 2, -1)
    a_cum = jnp.cumsum(a, axis=-1)
    decay_in = jnp.exp(segsum(a))
    scores = jnp.einsum('bclgn,bcsgn->bcgls', cc, bc)
    y_diag = jnp.einsum('bcgls,bcgels,bcsgep->bclgep', scores, decay_in, xc)
    decay_to_end = jnp.exp(a_cum[..., -1:] - a_cum)
    states = jnp.einsum('bclgn,bcgel,bclgep->bcgepn', bc, decay_to_end, xc)
    states = jnp.concatenate([h0[:, None].astype(states.dtype), states], axis=1)
    chunk_tot = jnp.pad(a_cum[..., -1], ((0, 0), (1, 0), (0, 0), (0, 0)))
    decay_chunk = jnp.exp(segsum(jnp.moveaxis(chunk_tot, 1, -1)))
    new_states = jnp.einsum('bgezc,bcgepn->bzgepn', decay_chunk, states)
    prev_states, final = new_states[:, :-1], new_states[:, -1]
    y_off = jnp.einsum('bclgn,bcgepn,bcgel->bclgep', cc, prev_states, jnp.exp(a_cum))
    return (y_diag + y_off).reshape(bsz, seq, g, e, p), final


def ssd_mixer(h, w_in, conv_w, conv_b, dt_bias, a_log, d_skip, norm_w, w_out, h0_f, h0_b):
    bsz, seq, _ = h.shape
    proj = h @ w_in
    z = proj[..., :SSD_D_INNER]
    xbc = jax.nn.silu(centred_dwconv(proj[..., SSD_D_INNER:SSD_D_INNER + SSD_CONV_DIM], conv_w, conv_b))
    dt_raw = proj[..., SSD_D_INNER + SSD_CONV_DIM:]
    xs = xbc[..., :SSD_D_INNER].reshape(bsz, seq, SSD_GROUPS, SSD_HPG, SSD_HEADDIM)
    bm = xbc[..., SSD_D_INNER:SSD_D_INNER + SSD_BC_DIM].reshape(bsz, seq, SSD_GROUPS, SSD_STATE)
    cm = xbc[..., SSD_D_INNER + SSD_BC_DIM:].reshape(bsz, seq, SSD_GROUPS, SSD_STATE)
    dt = jax.nn.softplus(dt_raw.reshape(bsz, seq, 2, SSD_HEADS).astype(jnp.float32)
                         + dt_bias.astype(jnp.float32))
    a = -jnp.exp(a_log.astype(jnp.float32))
    da = (dt * a).reshape(bsz, seq, 2, SSD_GROUPS, SSD_HPG)
    dt = dt.reshape(bsz, seq, 2, SSD_GROUPS, SSD_HPG)
    flip = lambda t: jnp.flip(t, axis=1)
    y_f, h_f = ssd_scan(xs * dt[:, :, 0, ..., None], da[:, :, 0], bm, cm, h0_f)
    y_b, h_b = ssd_scan(flip(xs * dt[:, :, 1, ..., None]), flip(da[:, :, 1]), flip(bm), flip(cm), h0_b)
    y = y_f + flip(y_b) + d_skip.reshape(SSD_GROUPS, SSD_HPG, 1) * xs
    y = rmsnorm(y.reshape(bsz, seq, SSD_D_INNER) * jax.nn.silu(z), norm_w)
    return y @ w_out, h_f, h_b


def s5_discretize(lam_re, lam_im, log_step, b_re, b_im):
    step = jnp.exp(log_step)[:, None]
    mag = jnp.exp(lam_re * step)
    ar = mag * jnp.cos(lam_im * step)
    ai = mag * jnp.sin(lam_im * step)
    den = lam_re * lam_re + lam_im * lam_im
    cr = ((ar - 1.0) * lam_re + ai * lam_im) / den
    ci = (ai * lam_re - (ar - 1.0) * lam_im) / den
    bbr = cr[..., None] * b_re - ci[..., None] * b_im
    bbi = cr[..., None] * b_im + ci[..., None] * b_re
    return ar, ai, bbr, bbi


def complex_affine_combine(e1, e2):
    ar1, ai1, br1, bi1 = e1
    ar2, ai2, br2, bi2 = e2
    return (ar2 * ar1 - ai2 * ai1, ar2 * ai1 + ai2 * ar1,
            ar2 * br1 - ai2 * bi1 + br2, ar2 * bi1 + ai2 * br1 + bi2)


def s5_scan(u, ar, ai, bbr, bbi, s0r, s0i, reverse):
    seq = u.shape[1]
    bur = jnp.einsum('btgc,gpc->tbgp', u, bbr)
    bui = jnp.einsum('btgc,gpc->tbgp', u, bbi)
    first = -1 if reverse else 0
    bur = bur.at[first].add(ar * s0r - ai * s0i)
    bui = bui.at[first].add(ar * s0i + ai * s0r)
    a_r = jnp.broadcast_to(ar, (seq, 1) + ar.shape)
    a_i = jnp.broadcast_to(ai, (seq, 1) + ai.shape)
    _, _, sr, si = lax.associative_scan(complex_affine_combine, (a_r, a_i, bur, bui),
                                        reverse=reverse, axis=0)
    return sr, si


def s5_readout(sr, si, c_re, c_im):
    t, b, g, _ = sr.shape
    y = jnp.einsum('tbgp,gcp->btgc', sr, c_re) - jnp.einsum('tbgp,gcp->btgc', si, c_im)
    return y.reshape(b, t, g * S5_CH)


def s5_glu(y, w_glu, b_glu):
    a, g = jnp.split(jax.nn.gelu(y) @ w_glu + b_glu, 2, axis=-1)
    return a * jax.nn.sigmoid(g)


def raster_to_colmajor(h, rows):
    b, t, d = h.shape
    return h.reshape(b, rows, GRID_W, d).transpose(0, 2, 1, 3).reshape(b, t, d)


def colmajor_to_raster(h, rows):
    b, t, d = h.shape
    return h.reshape(b, GRID_W, rows, d).transpose(0, 2, 1, 3).reshape(b, t, d)


def s5_mixer(hx, hc, rows, lam_re, lam_im, log_step, b_re, b_im, c_re, c_im, d_skip, w_glu, b_glu, ctx_needed):
    bsz, seq, _ = hx.shape
    hx_col = raster_to_colmajor(hx, rows)
    u_x = hx_col.reshape(bsz, seq, S5_GROUPS, S5_CH)
    u_c = hc.reshape(bsz, hc.shape[1], S5_GROUPS, S5_CH)
    y_x = d_skip * hx_col
    ys_c = []
    for d in range(2):
        rev = d == 1
        ar, ai, bbr, bbi = s5_discretize(lam_re[d], lam_im[d], log_step[d], b_re[d], b_im[d])
        zeros = jnp.zeros((bsz, S5_GROUPS, S5_STATE), hx.dtype)
        cr, ci = s5_scan(u_c, ar, ai, bbr, bbi, zeros, zeros, rev)
        end = 0 if rev else -1
        sr, si = s5_scan(u_x, ar, ai, bbr, bbi, cr[end], ci[end], rev)
        y_x = y_x + s5_readout(sr, si, c_re[d], c_im[d])
        if ctx_needed:
            ys_c.append(s5_readout(cr, ci, c_re[d], c_im[d]))
    out_x = colmajor_to_raster(s5_glu(y_x, w_glu, b_glu), rows)
    out_c = s5_glu(d_skip * hc + ys_c[0] + ys_c[1], w_glu, b_glu) if ctx_needed else None
    return out_x, out_c


def moe_swiglu(h, router_w, router_b, w_in, w_out):
    bsz, seq, d = h.shape
    t = h.reshape(-1, d)
    logits = (t @ router_w + router_b).astype(jnp.float32)
    top_val, top_idx = lax.top_k(logits, TOP_K)
    gates = jax.nn.softmax(top_val, axis=-1)
    combine = jnp.sum(jax.nn.one_hot(top_idx, N_EXPERTS, dtype=jnp.float32) * gates[..., None], axis=1)
    out = jnp.zeros(t.shape, jnp.float32)
    for e in range(N_EXPERTS):
        out = out + combine[:, e:e + 1] * swiglu(t, w_in[e], w_out[e])
    return out.astype(h.dtype).reshape(bsz, seq, d)


def setup_inputs(seed: int = 0) -> dict:
    key = jax.random.key(seed)
    ks = iter(jax.random.split(key, 40))

    def nrm(shape, scale):
        return scale * jax.random.normal(next(ks), shape, jnp.float32)

    def unif(shape, lo, hi):
        return jax.random.uniform(next(ks), shape, jnp.float32, lo, hi)

    n_a = (DEPTH + 1) // 2
    n_b = DEPTH // 2
    d = D_MODEL
    lam_im0 = jnp.pi * jnp.arange(S5_STATE, dtype=jnp.float32)
    inp = {}
    inp['x'] = nrm((BATCH, SEQ, d), 1.0)
    inp['c'] = nrm((BATCH, d), 1.0)
    inp['ctx'] = nrm((BATCH, CTX_LEN, d), 1.0)
    inp['c_ctx'] = nrm((d,), 1.0)
    inp['ada_w'] = nrm((DEPTH, d, N_MOD * d), 0.5 * d ** -0.5)
    inp['ada_b'] = nrm((DEPTH, N_MOD * d), 0.02)
    inp['norm_mix'] = 1.0 + nrm((DEPTH, d), 0.02)
    inp['norm_ffn'] = 1.0 + nrm((DEPTH, d), 0.02)
    inp['ssd_w_in'] = nrm((n_a, d, SSD_IN_DIM), d ** -0.5)
    inp['ssd_conv_w'] = nrm((n_a, SSD_CONV, SSD_CONV_DIM), SSD_CONV ** -0.5)
    inp['ssd_conv_b'] = nrm((n_a, SSD_CONV_DIM), 0.02)
    dt0 = jnp.exp(unif((n_a, 2, SSD_HEADS), math.log(DT_MIN), math.log(DT_MAX)))
    inp['ssd_dt_bias'] = dt0 + jnp.log(-jnp.expm1(-dt0))
    inp['ssd_a_log'] = jnp.log(unif((n_a, 2, SSD_HEADS), 1.0, 16.0))
    inp['ssd_d'] = 1.0 + nrm((n_a, SSD_HEADS), 0.02)
    inp['ssd_norm'] = 1.0 + nrm((n_a, SSD_D_INNER), 0.02)
    inp['ssd_w_out'] = nrm((n_a, SSD_D_INNER, d), SSD_D_INNER ** -0.5)
    inp['s5_lam_re'] = -0.5 + nrm((n_b, 2, S5_GROUPS, S5_STATE), 0.01)
    inp['s5_lam_im'] = lam_im0 + nrm((n_b, 2, S5_GROUPS, S5_STATE), 0.01)
    inp['s5_log_step'] = unif((n_b, 2, S5_GROUPS), math.log(DT_MIN), math.log(DT_MAX))
    inp['s5_b_re'] = nrm((n_b, 2, S5_GROUPS, S5_STATE, S5_CH), (2 * S5_CH) ** -0.5)
    inp['s5_b_im'] = nrm((n_b, 2, S5_GROUPS, S5_STATE, S5_CH), (2 * S5_CH) ** -0.5)
    inp['s5_c_re'] = nrm((n_b, 2, S5_GROUPS, S5_CH, S5_STATE), S5_STATE ** -0.5)
    inp['s5_c_im'] = nrm((n_b, 2, S5_GROUPS, S5_CH, S5_STATE), S5_STATE ** -0.5)
    inp['s5_d'] = nrm((n_b, d), 0.5)
    inp['s5_w_glu'] = nrm((n_b, d, 2 * d), d ** -0.5)
    inp['s5_b_glu'] = nrm((n_b, 2 * d), 0.02)
    inp['ffn_w_in'] = nrm((n_a, d, 2 * FFN_DIM), d ** -0.5)
    inp['ffn_w_out'] = nrm((n_a, FFN_DIM, d), FFN_DIM ** -0.5)
    inp['moe_router_w'] = nrm((n_b, d, N_EXPERTS), d ** -0.5)
    inp['moe_router_b'] = nrm((n_b, N_EXPERTS), 0.01)
    inp['moe_w_in'] = nrm((n_b, N_EXPERTS, d, 2 * EXPERT_DIM), d ** -0.5)
    inp['moe_w_out'] = nrm((n_b, N_EXPERTS, EXPERT_DIM, d), EXPERT_DIM ** -0.5)
    inp['norm_final'] = 1.0 + nrm((d,), 0.02)
    return inp


def reference(x, c, ctx, c_ctx, ada_w, ada_b, norm_mix, norm_ffn,
              ssd_w_in, ssd_conv_w, ssd_conv_b, ssd_dt_bias, ssd_a_log, ssd_d, ssd_norm, ssd_w_out,
              s5_lam_re, s5_lam_im, s5_log_step, s5_b_re, s5_b_im, s5_c_re, s5_c_im, s5_d,
              s5_w_glu, s5_b_glu, ffn_w_in, ffn_w_out, moe_router_w, moe_router_b, moe_w_in, moe_w_out,
              norm_final):
    bsz, seq, _ = x.shape
    rows = seq // GRID_W
    for i in range(DEPTH):
        j = i // N_MIXERS
        ctx_needed = i < DEPTH - 1
        mod_x = jnp.split((jax.nn.silu(c) @ ada_w[i] + ada_b[i])[:, None, :], N_MOD, axis=-1)
        mod_c = jnp.split((jax.nn.silu(c_ctx) @ ada_w[i] + ada_b[i])[None, None, :], N_MOD, axis=-1)
        hx = modulate(rmsnorm(x, norm_mix[i]), mod_x[0], mod_x[1])
        hc = modulate(rmsnorm(ctx, norm_mix[i]), mod_c[0], mod_c[1])
        if i % N_MIXERS == 0:
            prm = (ssd_w_in[j], ssd_conv_w[j], ssd_conv_b[j], ssd_dt_bias[j], ssd_a_log[j],
                   ssd_d[j], ssd_norm[j], ssd_w_out[j])
            zeros = jnp.zeros((bsz, SSD_GROUPS, SSD_HPG, SSD_HEADDIM, SSD_STATE), x.dtype)
            out_c, h_f, h_b = ssd_mixer(hc, *prm, zeros, zeros)
            out_x, _, _ = ssd_mixer(hx, *prm, h_f, h_b)
        else:
            out_x, out_c = s5_mixer(hx, hc, rows, s5_lam_re[j], s5_lam_im[j], s5_log_step[j],
                                    s5_b_re[j], s5_b_im[j], s5_c_re[j], s5_c_im[j], s5_d[j],
                                    s5_w_glu[j], s5_b_glu[j], ctx_needed)
        x = x + mod_x[2] * out_x
        if ctx_needed:
            ctx = ctx + mod_c[2] * out_c

        def channel_mixer(h):
            if i % 2 == 0:
                return swiglu(h, ffn_w_in[j], ffn_w_out[j])
            return moe_swiglu(h, moe_router_w[j], moe_router_b[j], moe_w_in[j], moe_w_out[j])

        x = x + mod_x[5] * channel_mixer(modulate(rmsnorm(x, norm_ffn[i]), mod_x[3], mod_x[4]))
        if ctx_needed:
            ctx = ctx + mod_c[5] * channel_mixer(modulate(rmsnorm(ctx, norm_ffn[i]), mod_c[3], mod_c[4]))
    return rmsnorm(x, norm_final)
```

```python
import functools

import jax
import jax.numpy as jnp
from jax import lax
from jax.experimental import pallas as pl
from jax.experimental.pallas import tpu as pltpu

F32 = jnp.float32
BF16 = jnp.bfloat16
HIGHEST = lax.Precision.HIGHEST

EPS = 1e-6
GRID_W = 64
N_MOD = 6

SSD_HEADDIM = 64
SSD_GROUPS = 4
SSD_STATE = 128
SSD_CONV = 5
SSD_CHUNK = 128

S5_CH = 16
S5_STATE = 64
S5_GB = 8

TOP_K = 2

V7X_VMEM_BYTES = 64 * 1024 * 1024
VMEM_LIMIT = V7X_VMEM_BYTES - 8 * 1024 * 1024
NEG_BIG = -1e30


def _params(*sem):
    return pltpu.CompilerParams(dimension_semantics=sem, vmem_limit_bytes=VMEM_LIMIT)


def _silu(v):
    return v * jax.nn.sigmoid(v)


def _softplus(v):
    return jnp.maximum(v, 0.0) + jnp.log1p(jnp.exp(-jnp.abs(v)))


def _rms(x, g):
    inv = lax.rsqrt(jnp.mean(x * x, axis=-1, keepdims=True) + EPS)
    return (x * inv) * g


def _nt_dot(a, b, precision=None):
    return lax.dot_general(a, b, (((1,), (1,)), ((), ())), precision=precision,
                           preferred_element_type=F32)


def _ada_kernel(c_ref, w_ref, b_ref, o_ref):
    s = _silu(c_ref[...])
    o_ref[0] = jnp.dot(s, w_ref[0], precision=HIGHEST, preferred_element_type=F32) + b_ref[0]


def _ada(cc, ada_w, ada_b):
    depth, d, n = ada_w.shape
    r = cc.shape[0]
    tn = 1536
    return pl.pallas_call(
        _ada_kernel,
        grid=(depth, n // tn),
        in_specs=[pl.BlockSpec((r, d), lambda i, j: (0, 0)),
                  pl.BlockSpec((1, d, tn), lambda i, j: (i, 0, j)),
                  pl.BlockSpec((1, 1, tn), lambda i, j: (i, 0, j))],
        out_specs=pl.BlockSpec((1, r, tn), lambda i, j: (i, 0, j)),
        out_shape=jax.ShapeDtypeStruct((depth, r, n), F32),
        compiler_params=_params("arbitrary", "arbitrary"),
        name="ada_mod",
    )(cc, ada_w, ada_b.reshape(depth, 1, n))


def _ssd_inproj_kernel(x_ref, g_ref, sh_ref, sc_ref, w_ref, cw_ref, cb_ref, wdt_ref, wdtT_ref,
                       bdt_ref, bdtT_ref, alog_ref, alogT_ref,
                       o_ref, dt_ref, dtT_ref, h_scr, pad_scr, *, n_plain, seq, heads, row_chunk):
    j = pl.program_id(1)
    pad = 8

    @pl.when(j == 0)
    def _():
        for r in range(0, seq, row_chunk):
            rs = slice(r, r + row_chunk)
            h = _rms(x_ref[0, rs, :], g_ref[...]) * (1.0 + sc_ref[0]) + sh_ref[0]
            h_scr[rs, :] = h.astype(BF16)
            dt = _softplus(jnp.dot(h, wdt_ref[...], precision=HIGHEST, preferred_element_type=F32)
                           + bdt_ref[...])
            da = dt * (-jnp.exp(alog_ref[...]))
            dtT = _softplus(_nt_dot(wdtT_ref[...], h, precision=HIGHEST) + bdtT_ref[...])
            daT = dtT * (-jnp.exp(alogT_ref[...]))
            for d in range(2):
                sl = slice(d * heads, (d + 1) * heads)
                dt_ref[d, 0, rs, :] = jnp.concatenate([dt[:, sl], da[:, sl]], axis=-1)
                dtT_ref[d, 0, :, rs] = jnp.concatenate([dtT[sl, :], daT[sl, :]], axis=0)
        zeros = jnp.zeros((pad, pad_scr.shape[1]), F32)
        pad_scr[0:pad, :] = zeros
        pad_scr[pad + seq:2 * pad + seq, :] = zeros

    acc = jnp.dot(h_scr[...], w_ref[...], preferred_element_type=F32)

    @pl.when(j < n_plain)
    def _():
        o_ref[0] = acc.astype(o_ref.dtype)

    @pl.when(j >= n_plain)
    def _():
        pad_scr[pad:pad + seq, :] = acc
        half = (SSD_CONV - 1) // 2
        for r in range(0, seq, row_chunk):
            s = cb_ref[...]
            for k in range(SSD_CONV):
                lo = pad + r + k - half
                s = s + cw_ref[k:k + 1, :] * pad_scr[lo:lo + row_chunk, :]
            o_ref[0, r:r + row_chunk, :] = _silu(s).astype(o_ref.dtype)


def _ssd_inproj(x3, g, shift, scale, w_main, conv_w_full, conv_b_full, w_dt, dt_bias, a_log,
                *, d_inner, heads):
    bsz, seq, d = x3.shape
    n = w_main.shape[1]
    tn = 512
    nh2 = 2 * heads
    per_batch = shift.shape[0] > 1
    mod_map = (lambda b, j: (b, 0, 0)) if per_batch else (lambda b, j: (0, 0, 0))
    kern = functools.partial(_ssd_inproj_kernel, n_plain=d_inner // tn, seq=seq, heads=heads,
                             row_chunk=min(256, seq))
    const2 = lambda b, j: (0, 0)
    return pl.pallas_call(
        kern,
        grid=(bsz, n // tn),
        in_specs=[pl.BlockSpec((1, seq, d), lambda b, j: (b, 0, 0), pipeline_mode=pl.Buffered(1)),
                  pl.BlockSpec((1, d), const2),
                  pl.BlockSpec((1, 1, d), mod_map),
                  pl.BlockSpec((1, 1, d), mod_map),
                  pl.BlockSpec((d, tn), lambda b, j: (0, j)),
                  pl.BlockSpec((SSD_CONV, tn), lambda b, j: (0, j)),
                  pl.BlockSpec((1, tn), lambda b, j: (0, j)),
                  pl.BlockSpec((d, nh2), const2),
                  pl.BlockSpec((nh2, d), const2),
                  pl.BlockSpec((1, nh2), const2),
                  pl.BlockSpec((nh2, 1), const2),
                  pl.BlockSpec((1, nh2), const2),
                  pl.BlockSpec((nh2, 1), const2)],
        out_specs=[pl.BlockSpec((1, seq, tn), lambda b, j: (b, 0, j)),
                   pl.BlockSpec((2, 1, seq, nh2), lambda b, j: (0, b, 0, 0)),
                   pl.BlockSpec((2, 1, nh2, seq), lambda b, j: (0, b, 0, 0))],
        out_shape=[jax.ShapeDtypeStruct((bsz, seq, n), BF16),
                   jax.ShapeDtypeStruct((2, bsz, seq, nh2), F32),
                   jax.ShapeDtypeStruct((2, bsz, nh2, seq), F32)],
        scratch_shapes=[pltpu.VMEM((seq, d), BF16), pltpu.VMEM((seq + 16, tn), F32)],
        compiler_params=_params("arbitrary", "arbitrary"),
        name="ssd_inproj",
    )(x3, g.reshape(1, d), shift, scale, w_main, conv_w_full, conv_b_full.reshape(1, n),
      w_dt, w_dt.T, dt_bias.reshape(1, nh2), dt_bias.reshape(nh2, 1),
      a_log.reshape(1, nh2), a_log.reshape(nh2, 1))


def _ssd_scan_kernel(xs_ref, b_ref, c_ref, dtc_ref, dtcT_ref, h0_ref, y_ref, hfin_ref,
                     st_scr, xw_scr, *, heads, n_chunks):
    d = pl.program_id(1)
    k = pl.program_id(2)
    L = xs_ref.shape[1]
    hpg = heads // SSD_GROUPS
    pairs_per_group = hpg // 2
    gw = hpg * SSD_HEADDIM

    @pl.when(k == 0)
    def _():
        st_scr[...] = h0_ref[0, 0]

    row = lax.broadcasted_iota(jnp.int32, (L, L), 0)
    col = lax.broadcasted_iota(jnp.int32, (L, L), 1)
    mask = jnp.where(d == 0, row - col, col - row) >= 0
    mask_f = mask.astype(F32)

    dtc = dtc_ref[0, 0]
    dtcT = dtcT_ref[0, 0]
    dt, da = dtc[:, :heads], dtc[:, heads:]
    dtT, daT = dtcT[:heads, :], dtcT[heads:, :]
    cum = jnp.dot(mask_f, da, precision=HIGHEST, preferred_element_type=F32)
    cumT = _nt_dot(daT, mask_f, precision=HIGHEST)
    tot = jnp.sum(da, axis=0, keepdims=True)
    ecum = jnp.exp(cum)
    etot = jnp.exp(tot)
    wgt = jnp.exp(tot - cum) * dt

    lane = lax.broadcasted_iota(jnp.int32, (L, 2 * SSD_HEADDIM), 1)
    lo = lane < SSD_HEADDIM
    lo_row = lax.broadcasted_iota(jnp.int32, (1, 2 * SSD_HEADDIM), 1) < SSD_HEADDIM

    for g in range(SSD_GROUPS):
        bg = b_ref[0, :, g * SSD_STATE:(g + 1) * SSD_STATE]
        cg = c_ref[0, :, g * SSD_STATE:(g + 1) * SSD_STATE]
        scores = _nt_dot(cg, bg)
        yoff = jnp.dot(cg, st_scr[g].astype(BF16), preferred_element_type=F32)
        for q in range(pairs_per_group):
            p = g * pairs_per_group + q
            h0, h1 = 2 * p, 2 * p + 1
            xs2 = xs_ref[0, :, p * 128:(p + 1) * 128]
            ms = []
            for h in (h0, h1):
                diff = cum[:, h:h + 1] - cumT[h:h + 1, :]
                dec = jnp.exp(jnp.where(mask, diff, NEG_BIG))
                ms.append((scores * dec * dtT[h:h + 1, :]).astype(BF16))
            zero = jnp.zeros_like(xs2)
            rhs = jnp.concatenate([jnp.where(lo, xs2, zero), jnp.where(lo, zero, xs2)], axis=0)
            ydiag = jnp.dot(jnp.concatenate(ms, axis=1), rhs, preferred_element_type=F32)
            e2 = jnp.where(lo, ecum[:, h0:h0 + 1], ecum[:, h1:h1 + 1])
            y2 = ydiag + yoff[:, q * 128:(q + 1) * 128] * e2
            y_ref[0, 0, :, p * 128:(p + 1) * 128] = y2.astype(y_ref.dtype)
            w2 = jnp.where(lo, wgt[:, h0:h0 + 1], wgt[:, h1:h1 + 1])
            xw_scr[:, q * 128:(q + 1) * 128] = (xs2.astype(F32) * w2).astype(BF16)
        bgT = bg.astype(F32).T.astype(BF16)
        new = jnp.dot(bgT, xw_scr[...], preferred_element_type=F32)
        for q in range(pairs_per_group):
            p = g * pairs_per_group + q
            et2 = jnp.where(lo_row, etot[:, 2 * p:2 * p + 1], etot[:, 2 * p + 1:2 * p + 2])
            sl = slice(q * 128, (q + 1) * 128)
            st_scr[g, :, sl] = st_scr[g, :, sl] * et2 + new[:, sl]

    @pl.when(k == n_chunks - 1)
    def _():
        hfin_ref[0, 0] = st_scr[...]


def _ssd_scan(zx, dtc, dtcT, h0, *, d_inner, heads):
    bsz, seq, _ = zx.shape
    L = min(SSD_CHUNK, seq)
    nc = seq // L
    bc_w = SSD_GROUPS * SSD_STATE
    gw = (heads // SSD_GROUPS) * SSD_HEADDIM
    nh2 = 2 * heads
    xs_blk = d_inner // d_inner
    b_blk = (2 * d_inner) // bc_w
    c_blk = b_blk + 1

    def chunk(d, k):
        return k + d * (nc - 1 - 2 * k)

    kern = functools.partial(_ssd_scan_kernel, heads=heads, n_chunks=nc)
    st_shape = (SSD_GROUPS, SSD_STATE, gw)
    return pl.pallas_call(
        kern,
        grid=(bsz, 2, nc),
        in_specs=[pl.BlockSpec((1, L, d_inner), lambda b, d, k: (b, chunk(d, k), xs_blk)),
                  pl.BlockSpec((1, L, bc_w), lambda b, d, k: (b, chunk(d, k), b_blk)),
                  pl.BlockSpec((1, L, bc_w), lambda b, d, k: (b, chunk(d, k), c_blk)),
                  pl.BlockSpec((1, 1, L, nh2), lambda b, d, k: (d, b, chunk(d, k), 0)),
                  pl.BlockSpec((1, 1, nh2, L), lambda b, d, k: (d, b, 0, chunk(d, k))),
                  pl.BlockSpec((1, 1) + st_shape, lambda b, d, k: (b, d, 0, 0, 0))],
        out_specs=[pl.BlockSpec((1, 1, L, d_inner), lambda b, d, k: (d, b, chunk(d, k), 0)),
                   pl.BlockSpec((1, 1) + st_shape, lambda b, d, k: (b, d, 0, 0, 0))],
        out_shape=[jax.ShapeDtypeStruct((2, bsz, seq, d_inner), BF16),
                   jax.ShapeDtypeStruct((bsz, 2) + st_shape, F32)],
        scratch_shapes=[pltpu.VMEM(st_shape, F32), pltpu.VMEM((L, gw), BF16)],
        compiler_params=_params("arbitrary", "arbitrary", "arbitrary"),
        name="ssd_scan",
    )(zx, zx, zx, dtc, dtcT, h0)


def _ssd_out_kernel(y_ref, z_ref, xs_ref, dsk_ref, nw_ref, w_ref, x_ref, gate_ref, o_ref):
    y = y_ref[0].astype(F32) + y_ref[1].astype(F32) + dsk_ref[...] * xs_ref[...].astype(F32)
    yg = y * _silu(z_ref[...].astype(F32))
    yn = _rms(yg, nw_ref[...]).astype(BF16)
    out = jnp.dot(yn, w_ref[...], preferred_element_type=F32)
    o_ref[...] = x_ref[...] + gate_ref[0] * out


def _ssd_out(y, zx, d_exp, norm_w, w_out, x2, gate, *, seq, d_inner):
    m, d = x2.shape
    tm = min(512, seq)
    bpt = seq // tm
    per_batch = gate.shape[0] > 1
    gate_map = (lambda i: (i // bpt, 0, 0)) if per_batch else (lambda i: (0, 0, 0))
    return pl.pallas_call(
        _ssd_out_kernel,
        grid=(m // tm,),
        in_specs=[pl.BlockSpec((2, tm, d_inner), lambda i: (0, i, 0)),
                  pl.BlockSpec((tm, d_inner), lambda i: (i, 0)),
                  pl.BlockSpec((tm, d_inner), lambda i: (i, 1)),
                  pl.BlockSpec((1, d_inner), lambda i: (0, 0)),
                  pl.BlockSpec((1, d_inner), lambda i: (0, 0)),
                  pl.BlockSpec((d_inner, d), lambda i: (0, 0)),
                  pl.BlockSpec((tm, d), lambda i: (i, 0)),
                  pl.BlockSpec((1, 1, d), gate_map)],
        out_specs=pl.BlockSpec((tm, d), lambda i: (i, 0)),
        out_shape=jax.ShapeDtypeStruct((m, d), F32),
        compiler_params=_params("arbitrary"),
        name="ssd_out",
    )(y, zx, zx, d_exp.reshape(1, d_inner), norm_w.reshape(1, d_inner), w_out, x2, gate)


def _ffn_kernel(x_ref, g_ref, sh_ref, sc_ref, gate_ref, wg_ref, wu_ref, wo_ref, o_ref,
                h_scr, acc_scr, *, n_f):
    j = pl.program_id(1)

    @pl.when(j == 0)
    def _():
        h = _rms(x_ref[...], g_ref[...]) * (1.0 + sc_ref[0]) + sh_ref[0]
        h_scr[...] = h.astype(BF16)
        acc_scr[...] = jnp.zeros_like(acc_scr)

    h = h_scr[...]
    gv = jnp.dot(h, wg_ref[...], preferred_element_type=F32)
    uv = jnp.dot(h, wu_ref[...], preferred_element_type=F32)
    a = (_silu(gv) * uv).astype(BF16)
    acc_scr[...] += jnp.dot(a, wo_ref[...], preferred_element_type=F32)

    @pl.when(j == n_f - 1)
    def _():
        o_ref[...] = x_ref[...] + gate_ref[0] * acc_scr[...]


def _ffn(x2, g, shift, scale, gate, w_in, w_out, *, seq):
    m, d = x2.shape
    f = w_out.shape[0]
    tm = min(512, seq)
    bpt = seq // tm
    tf = f // 2 if (f // 2) % 128 == 0 else f
    n_f = f // tf
    per_batch = shift.shape[0] > 1
    mod_map = (lambda i, j: (i // bpt, 0, 0)) if per_batch else (lambda i, j: (0, 0, 0))
    kern = functools.partial(_ffn_kernel, n_f=n_f)
    return pl.pallas_call(
        kern,
        grid=(m // tm, n_f),
        in_specs=[pl.BlockSpec((tm, d), lambda i, j: (i, 0)),
                  pl.BlockSpec((1, d), lambda i, j: (0, 0)),
                  pl.BlockSpec((1, 1, d), mod_map),
                  pl.BlockSpec((1, 1, d), mod_map),
                  pl.BlockSpec((1, 1, d), mod_map),
                  pl.BlockSpec((d, tf), lambda i, j: (0, j)),
                  pl.BlockSpec((d, tf), lambda i, j: (0, n_f + j)),
                  pl.BlockSpec((tf, d), lambda i, j: (j, 0))],
        out_specs=pl.BlockSpec((tm, d), lambda i, j: (i, 0)),
        out_shape=jax.ShapeDtypeStruct((m, d), F32),
        scratch_shapes=[pltpu.VMEM((tm, d), BF16), pltpu.VMEM((tm, d), F32)],
        compiler_params=_params("arbitrary", "arbitrary"),
        name="dense_swiglu",
    )(x2, g.reshape(1, d), shift, scale, gate, w_in, w_in, w_out)


def _s5_disc_kernel(lre_ref, lim_ref, step_ref, bre_ref, bim_ref, ar_ref, ai_ref, bbr_ref, bbi_ref):
    step = jnp.exp(step_ref[...])
    lre, lim = lre_ref[...], lim_ref[...]
    mag = jnp.exp(lre * step)
    ar = mag * jnp.cos(lim * step)
    ai = mag * jnp.sin(lim * step)
    den = lre * lre + lim * lim
    cr = ((ar - 1.0) * lre + ai * lim) / den
    ci = (ai * lre - (ar - 1.0) * lim) / den
    ar_ref[...] = ar
    ai_ref[...] = ai
    bbr_ref[...] = cr * bre_ref[...] - ci * bim_ref[...]
    bbi_ref[...] = cr * bim_ref[...] + ci * bre_ref[...]


def _s5_discretize(lam_re, lam_im, log_step, b_re, b_im):
    two, g, p = lam_re.shape
    c = b_re.shape[-1]
    lre = jnp.repeat(lam_re, c, axis=-1)
    lim = jnp.repeat(lam_im, c, axis=-1)
    shp = jax.ShapeDtypeStruct((two, g, p * c), F32)
    ar, ai, bbr, bbi = pl.pallas_call(
        _s5_disc_kernel, out_shape=[shp, shp, shp, shp], name="s5_discretize",
    )(lre, lim, log_step.reshape(two, g, 1), b_re.reshape(two, g, p * c), b_im.reshape(two, g, p * c))
    ar = ar.reshape(two, g, p, c)[..., 0]
    ai = ai.reshape(two, g, p, c)[..., 0]
    return ar, ai, bbr.reshape(two, g, p, c), bbi.reshape(two, g, p, c)


def _s5_kernel(x_ref, g_ref, sh_ref, sc_ref, wb_ref, wc_ref, a_ref, s0_ref, y_ref, sfin_ref,
               st_scr, bu_scr, *, n_chunks, steps, bsz):
    d = pl.program_id(0)
    k = pl.program_id(1)
    rows, dm = x_ref.shape
    gl = S5_GB * S5_STATE
    cb = S5_GB * S5_CH

    @pl.when(k == 0)
    def _():
        st_scr[...] = s0_ref[0]

    hn = _rms(x_ref[...], g_ref[...]).reshape(steps, bsz, dm)
    h = hn * (1.0 + sc_ref[...])[None] + sh_ref[...][None]
    u = h.reshape(rows, dm).astype(BF16)

    for gb in range(dm // cb):
        bu_scr[...] = jnp.dot(u[:, gb * cb:(gb + 1) * cb], wb_ref[0, gb], preferred_element_type=F32)
        sl = slice(gb * gl, (gb + 1) * gl)
        ar = jnp.broadcast_to(a_ref[0, 0:1, sl], (bsz, gl))
        ai = jnp.broadcast_to(a_ref[0, 1:2, sl], (bsz, gl))

        def step(i, carry):
            sr, si = carry
            t = jnp.where(d == 0, i, steps - 1 - i)
            r0 = pl.multiple_of(t * bsz, bsz)
            bur = bu_scr[pl.ds(r0, bsz), 0:gl]
            bui = bu_scr[pl.ds(r0, bsz), gl:2 * gl]
            nsr = ar * sr - ai * si + bur
            nsi = ar * si + ai * sr + bui
            bu_scr[pl.ds(r0, bsz), 0:gl] = nsr
            bu_scr[pl.ds(r0, bsz), gl:2 * gl] = nsi
            return nsr, nsi

        sr, si = lax.fori_loop(0, steps, step, (st_scr[0, :, sl], st_scr[1, :, sl]), unroll=2)
        st_scr[0, :, sl] = sr
        st_scr[1, :, sl] = si
        yv = jnp.dot(bu_scr[...].astype(BF16), wc_ref[0, gb], preferred_element_type=F32)
        y_ref[0, :, gb * cb:(gb + 1) * cb] = yv.astype(y_ref.dtype)

    @pl.when(k == n_chunks - 1)
    def _():
        sfin_ref[0] = st_scr[...]


def _s5_scan(xt, g, shift, scale, wb, wc, a, s0, *, bsz):
    m, d = xt.shape
    seq = m // bsz
    steps = min(64, seq)
    rows = steps * bsz
    nc = seq // steps
    nstate = s0.shape[-1]
    n_gb = d // (S5_GB * S5_CH)
    gl = S5_GB * S5_STATE

    def chunk(dd, k):
        return k + dd * (nc - 1 - 2 * k)

    kern = functools.partial(_s5_kernel, n_chunks=nc, steps=steps, bsz=bsz)
    return pl.pallas_call(
        kern,
        grid=(2, nc),
        in_specs=[pl.BlockSpec((rows, d), lambda dd, k: (chunk(dd, k), 0)),
                  pl.BlockSpec((1, d), lambda dd, k: (0, 0)),
                  pl.BlockSpec((bsz, d), lambda dd, k: (0, 0)),
                  pl.BlockSpec((bsz, d), lambda dd, k: (0, 0)),
                  pl.BlockSpec((1, n_gb, S5_GB * S5_CH, 2 * gl), lambda dd, k: (dd, 0, 0, 0)),
                  pl.BlockSpec((1, n_gb, 2 * gl, S5_GB * S5_CH), lambda dd, k: (dd, 0, 0, 0)),
                  pl.BlockSpec((1, 2, nstate), lambda dd, k: (dd, 0, 0)),
                  pl.BlockSpec((1, 2, bsz, nstate), lambda dd, k: (dd, 0, 0, 0))],
        out_specs=[pl.BlockSpec((1, rows, d), lambda dd, k: (dd, chunk(dd, k), 0)),
                   pl.BlockSpec((1, 2, bsz, nstate), lambda dd, k: (dd, 0, 0, 0))],
        out_shape=[jax.ShapeDtypeStruct((2, m, d), BF16),
                   jax.ShapeDtypeStruct((2, 2, bsz, nstate), F32)],
        scratch_shapes=[pltpu.VMEM((2, bsz, nstate), F32), pltpu.VMEM((rows, 2 * gl), F32)],
        compiler_params=_params("arbitrary", "arbitrary"),
        name="s5_scan",
    )(xt, g.reshape(1, d), shift, scale, wb, wc, a, s0)


def _gelu_tanh(v):
    return 0.5 * v * (1.0 + jnp.tanh(0.7978845608028654 * (v + 0.044715 * (v * v * v))))


def _s5_glu_kernel(x_ref, y_ref, g_ref, sh_ref, sc_ref, gate_ref, dsk_ref, w_ref, b_ref, o_ref, *, bsz):
    rows, dm = x_ref.shape
    x = x_ref[...]
    hn = _rms(x, g_ref[...]).reshape(rows // bsz, bsz, dm)
    h = (hn * (1.0 + sc_ref[...])[None] + sh_ref[...][None]).reshape(rows, dm)
    yv = dsk_ref[...] * h + y_ref[0].astype(F32) + y_ref[1].astype(F32)
    ge = _gelu_tanh(yv).astype(BF16)
    o = jnp.dot(ge, w_ref[...], preferred_element_type=F32) + b_ref[...]
    out = o[:, :dm] * jax.nn.sigmoid(o[:, dm:])
    res = x.reshape(rows // bsz, bsz, dm) + gate_ref[...][None] * out.reshape(rows // bsz, bsz, dm)
    o_ref[...] = res.reshape(rows, dm)


def _s5_glu(xt, y, g, shift, scale, gate, d_skip, w_glu, b_glu, *, bsz):
    m, d = xt.shape
    tm = min(512, m)
    kern = functools.partial(_s5_glu_kernel, bsz=bsz)
    full = lambda i: (0, 0)
    return pl.pallas_call(
        kern,
        grid=(m // tm,),
        in_specs=[pl.BlockSpec((tm, d), lambda i: (i, 0)),
                  pl.BlockSpec((2, tm, d), lambda i: (0, i, 0)),
                  pl.BlockSpec((1, d), full),
                  pl.BlockSpec((bsz, d), full),
                  pl.BlockSpec((bsz, d), full),
                  pl.BlockSpec((bsz, d), full),
                  pl.BlockSpec((1, d), full),
                  pl.BlockSpec((d, 2 * d), full),
                  pl.BlockSpec((1, 2 * d), full)],
        out_specs=pl.BlockSpec((tm, d), lambda i: (i, 0)),
        out_shape=jax.ShapeDtypeStruct((m, d), F32),
        compiler_params=_params("arbitrary"),
        name="s5_glu",
    )(xt, y, g.reshape(1, d), shift, scale, gate, d_skip.reshape(1, d), w_glu, b_glu.reshape(1, 2 * d))


def _moe_kernel(x_ref, g_ref, sh_ref, sc_ref, gate_ref, rw_ref, rb_ref, wg_ref, wu_ref, wo_ref,
                gf_ref, o_ref, h_scr, comb_scr, acc_scr, *, bsz, n_exp, n_f):
    e = pl.program_id(1)
    f = pl.program_id(2)
    rows, dm = x_ref.shape

    @pl.when((e == 0) & (f == 0))
    def _():
        hn = _rms(x_ref[...], g_ref[...]).reshape(rows // bsz, bsz, dm)
        h = (hn * (1.0 + sc_ref[...])[None] + sh_ref[...][None]).reshape(rows, dm)
        h_scr[...] = h.astype(BF16)
        logits = jnp.dot(h, rw_ref[...], precision=HIGHEST, preferred_element_type=F32) + rb_ref[...]
        idx = lax.broadcasted_iota(jnp.int32, logits.shape, 1)
        m1 = jnp.max(logits, axis=-1, keepdims=True)
        i1 = jnp.min(jnp.where(logits == m1, idx, n_exp), axis=-1, keepdims=True)
        rest = jnp.where(idx == i1, -jnp.inf, logits)
        m2 = jnp.max(rest, axis=-1, keepdims=True)
        i2 = jnp.min(jnp.where(rest == m2, idx, n_exp), axis=-1, keepdims=True)
        e2 = jnp.exp(m2 - m1)
        g1 = 1.0 / (1.0 + e2)
        comb = jnp.where(idx == i1, g1, 0.0) + jnp.where(idx == i2, e2 * g1, 0.0)
        for ee in range(n_exp):
            col = jnp.sum(jnp.where(idx == ee, comb, 0.0), axis=-1, keepdims=True)
            comb_scr[ee] = jnp.broadcast_to(col, (rows, 128))
        acc_scr[...] = jnp.zeros_like(acc_scr)

    h = h_scr[...]
    gv = jnp.dot(h, wg_ref[0], preferred_element_type=F32)
    uv = jnp.dot(h, wu_ref[0], preferred_element_type=F32)
    tf = gv.shape[1]
    cw = pltpu.repeat(comb_scr[e], tf // 128, axis=1)
    a = (_silu(gv) * uv * cw).astype(BF16)
    acc_scr[...] += jnp.dot(a, wo_ref[0], preferred_element_type=F32)

    @pl.when((e == n_exp - 1) & (f == n_f - 1))
    def _():
        upd = gate_ref[...][None] * acc_scr[...].reshape(rows // bsz, bsz, dm)
        xo = x_ref[...] + upd.reshape(rows, dm)
        o_ref[...] = _rms(xo, gf_ref[...])


def _moe(xt, g, shift, scale, gate, router_w, router_b, w_in, w_out, g_final, *, bsz):
    m, d = xt.shape
    n_exp, f, _ = w_out.shape
    tm = min(1024, m)
    tf = 512
    n_f = f // tf
    kern = functools.partial(_moe_kernel, bsz=bsz, n_exp=n_exp, n_f=n_f)
    full = lambda i, e, j: (0, 0)
    return pl.pallas_call(
        kern,
        grid=(m // tm, n_exp, n_f),
        in_specs=[pl.BlockSpec((tm, d), lambda i, e, j: (i, 0)),
                  pl.BlockSpec((1, d), full),
                  pl.BlockSpec((bsz, d), full),
                  pl.BlockSpec((bsz, d), full),
                  pl.BlockSpec((bsz, d), full),
                  pl.BlockSpec((d, n_exp), full),
                  pl.BlockSpec((1, n_exp), full),
                  pl.BlockSpec((1, d, tf), lambda i, e, j: (e, 0, j)),
                  pl.BlockSpec((1, d, tf), lambda i, e, j: (e, 0, n_f + j)),
                  pl.BlockSpec((1, tf, d), lambda i, e, j: (e, j, 0)),
                  pl.BlockSpec((1, d), full)],
        out_specs=pl.BlockSpec((tm, d), lambda i, e, j: (i, 0)),
        out_shape=jax.ShapeDtypeStruct((m, d), F32),
        scratch_shapes=[pltpu.VMEM((tm, d), BF16), pltpu.VMEM((n_exp, tm, 128), F32),
                        pltpu.VMEM((tm, d), F32)],
        compiler_params=_params("arbitrary", "arbitrary", "arbitrary"),
        name="moe_swiglu",
    )(xt, g.reshape(1, d), shift, scale, gate, router_w, router_b.reshape(1, n_exp),
      w_in, w_in, w_out, g_final.reshape(1, d))


def _blockdiag(w, n_gb):
    two, g, r, c = w.shape
    eye = jnp.eye(S5_GB, dtype=w.dtype)
    out = jnp.einsum("dbgrc,gh->dbgrhc", w.reshape(two, n_gb, S5_GB, r, c), eye)
    return out.reshape(two, n_gb, S5_GB * r, S5_GB * c)


def kernel(x, c, ctx, c_ctx, ada_w, ada_b, norm_mix, norm_ffn, ssd_w_in, ssd_conv_w, ssd_conv_b, ssd_dt_bias, ssd_a_log, ssd_d, ssd_norm, ssd_w_out, s5_lam_re, s5_lam_im, s5_log_step, s5_b_re, s5_b_im, s5_c_re, s5_c_im, s5_d, s5_w_glu, s5_b_glu, ffn_w_in, ffn_w_out, moe_router_w, moe_router_b, moe_w_in, moe_w_out, norm_final):
    bsz, seq, d = x.shape
    ctx_len = ctx.shape[1]
    depth = ada_w.shape[0]
    assert depth == 2, "one SSD layer followed by one S5 layer"
    rows = seq // GRID_W

    pad_rows = (-(bsz + 1)) % 8
    cc = jnp.concatenate([c, c_ctx[None], jnp.zeros((pad_rows, d), F32)], axis=0)
    mods = _ada(cc, ada_w, ada_b)

    def mod(i, n):
        sl = slice(n * d, (n + 1) * d)
        return mods[i, :bsz, sl], mods[i, bsz:bsz + 1, sl]

    heads = ssd_d.shape[1]
    d_inner = heads * SSD_HEADDIM
    conv_dim = ssd_conv_w.shape[2]
    n_main = d_inner + conv_dim
    w_in0 = ssd_w_in[0]
    w_main = w_in0[:, :n_main].astype(BF16)
    w_dt = w_in0[:, n_main:]
    conv_w_full = jnp.concatenate([jnp.zeros((SSD_CONV, d_inner), F32), ssd_conv_w[0]], axis=1)
    conv_b_full = jnp.concatenate([jnp.zeros((d_inner,), F32), ssd_conv_b[0]])
    d_exp = jnp.repeat(ssd_d[0], SSD_HEADDIM)
    w_out0 = ssd_w_out[0].astype(BF16)
    gw = (heads // SSD_GROUPS) * SSD_HEADDIM

    (sh_x, sh_c), (sc_x, sc_c), (gt_x, gt_c) = mod(0, 0), mod(0, 1), mod(0, 2)
    r3 = lambda a: a[:, None, :]
    dt_bias = ssd_dt_bias[0].reshape(-1)
    a_log = ssd_a_log[0].reshape(-1)

    def ssd_mixer(tok3, shift, scale, gate, h0):
        b_, t_, _ = tok3.shape
        zx, dtc, dtcT = _ssd_inproj(tok3, norm_mix[0], r3(shift), r3(scale), w_main, conv_w_full,
                                    conv_b_full, w_dt, dt_bias, a_log, d_inner=d_inner, heads=heads)
        y, hfin = _ssd_scan(zx, dtc, dtcT, h0, d_inner=d_inner, heads=heads)
        out = _ssd_out(y.reshape(2, b_ * t_, d_inner), zx.reshape(b_ * t_, n_main), d_exp, ssd_norm[0],
                       w_out0, tok3.reshape(b_ * t_, d), r3(gate), seq=t_, d_inner=d_inner)
        return out, hfin

    h_zero = jnp.zeros((bsz, 2, SSD_GROUPS, SSD_STATE, gw), F32)
    ctx2, h_ctx = ssd_mixer(ctx, sh_c, sc_c, gt_c, h_zero)
    x2, _ = ssd_mixer(x, sh_x, sc_x, gt_x, h_ctx)

    ffn_in = ffn_w_in[0].astype(BF16)
    ffn_out = ffn_w_out[0].astype(BF16)
    (sh_x, sh_c), (sc_x, sc_c), (gt_x, gt_c) = mod(0, 3), mod(0, 4), mod(0, 5)
    x2 = _ffn(x2, norm_ffn[0], r3(sh_x), r3(sc_x), r3(gt_x), ffn_in, ffn_out, seq=seq)
    ctx2 = _ffn(ctx2, norm_ffn[0], r3(sh_c), r3(sc_c), r3(gt_c), ffn_in, ffn_out, seq=ctx_len)

    xt = x2.reshape(bsz, rows, GRID_W, d).transpose(2, 1, 0, 3).reshape(seq * bsz, d)
    ct = ctx2.reshape(bsz, ctx_len, d).transpose(1, 0, 2).reshape(ctx_len * bsz, d)

    ar, ai, bbr, bbi = _s5_discretize(s5_lam_re[0], s5_lam_im[0], s5_log_step[0], s5_b_re[0], s5_b_im[0])
    n_groups = ar.shape[1]
    n_gb = n_groups // S5_GB
    wb = jnp.concatenate([_blockdiag(bbr.transpose(0, 1, 3, 2), n_gb),
                          _blockdiag(bbi.transpose(0, 1, 3, 2), n_gb)], axis=-1).astype(BF16)
    wc = jnp.concatenate([_blockdiag(s5_c_re[0].transpose(0, 1, 3, 2), n_gb),
                          _blockdiag(-s5_c_im[0].transpose(0, 1, 3, 2), n_gb)], axis=2).astype(BF16)
    a_coef = jnp.stack([ar.reshape(2, -1), ai.reshape(2, -1)], axis=1)
    nstate = a_coef.shape[-1]

    (sh_x, sh_c), (sc_x, sc_c), (gt_x, _) = mod(1, 0), mod(1, 1), mod(1, 2)
    bc = lambda a: jnp.broadcast_to(a, (bsz, d))
    s_zero = jnp.zeros((2, 2, bsz, nstate), F32)
    _, s_ctx = _s5_scan(ct, norm_mix[1], bc(sh_c), bc(sc_c), wb, wc, a_coef, s_zero, bsz=bsz)
    y, _ = _s5_scan(xt, norm_mix[1], sh_x, sc_x, wb, wc, a_coef, s_ctx, bsz=bsz)
    xt = _s5_glu(xt, y, norm_mix[1], sh_x, sc_x, gt_x, s5_d[0], s5_w_glu[0].astype(BF16), s5_b_glu[0], bsz=bsz)

    (sh_x, _), (sc_x, _), (gt_x, _) = mod(1, 3), mod(1, 4), mod(1, 5)
    out_t = _moe(xt, norm_ffn[1], sh_x, sc_x, gt_x, moe_router_w[0], moe_router_b[0],
                 moe_w_in[0].astype(BF16), moe_w_out[0].astype(BF16), norm_final, bsz=bsz)
    return out_t.reshape(GRID_W, rows, bsz, d).transpose(2, 1, 0, 3).reshape(bsz, seq, d)
```

```python
import functools

import jax
import jax.numpy as jnp
from jax import lax
from jax.experimental import pallas as pl
from jax.experimental.pallas import tpu as pltpu

F32 = jnp.float32
BF16 = jnp.bfloat16
HIGHEST = lax.Precision.HIGHEST

EPS = 1e-6
GRID_W = 64
N_MOD = 6

SSD_HEADDIM = 64
SSD_GROUPS = 4
SSD_STATE = 128
SSD_CONV = 5
SSD_CHUNK = 128

S5_CH = 16
S5_STATE = 64
S5_GB = 8

TOP_K = 2

V7X_VMEM_BYTES = 64 * 1024 * 1024
VMEM_LIMIT = V7X_VMEM_BYTES - 8 * 1024 * 1024
NEG_BIG = -1e30


def _params(*sem):
    return pltpu.CompilerParams(dimension_semantics=sem, vmem_limit_bytes=VMEM_LIMIT)


def _silu(v):
    return v * jax.nn.sigmoid(v)


def _softplus(v):
    return jnp.maximum(v, 0.0) + jnp.log1p(jnp.exp(-jnp.abs(v)))


def _rms(x, g):
    inv = lax.rsqrt(jnp.mean(x * x, axis=-1, keepdims=True) + EPS)
    return (x * inv) * g


def _nt_dot(a, b, precision=None):
    return lax.dot_general(a, b, (((1,), (1,)), ((), ())), precision=precision,
                           preferred_element_type=F32)


def _ada_kernel(c_ref, w_ref, b_ref, o_ref):
    s = _silu(c_ref[...])
    o_ref[0] = jnp.dot(s, w_ref[0], precision=HIGHEST, preferred_element_type=F32) + b_ref[0]


def _ada(cc, ada_w, ada_b):
    depth, d, n = ada_w.shape
    r = cc.shape[0]
    tn = 1536
    return pl.pallas_call(
        _ada_kernel,
        grid=(depth, n // tn),
        in_specs=[pl.BlockSpec((r, d), lambda i, j: (0, 0)),
                  pl.BlockSpec((1, d, tn), lambda i, j: (i, 0, j)),
                  pl.BlockSpec((1, 1, tn), lambda i, j: (i, 0, j))],
        out_specs=pl.BlockSpec((1, r, tn), lambda i, j: (i, 0, j)),
        out_shape=jax.ShapeDtypeStruct((depth, r, n), F32),
        compiler_params=_params("arbitrary", "arbitrary"),
        name="ada_mod",
    )(cc, ada_w, ada_b.reshape(depth, 1, n))


def _ssd_inproj_kernel(x_ref, g_ref, sh_ref, sc_ref, w_ref, cw_ref, cb_ref, wdt_ref, wdtT_ref,
                       bdt_ref, bdtT_ref, alog_ref, alogT_ref,
                       o_ref, dt_ref, dtT_ref, h_scr, pad_scr, *, n_plain, seq, heads, row_chunk):
    j = pl.program_id(1)
    pad = 8

    @pl.when(j == 0)
    def _():
        for r in range(0, seq, row_chunk):
            rs = slice(r, r + row_chunk)
            h = _rms(x_ref[0, rs, :], g_ref[...]) * (1.0 + sc_ref[0]) + sh_ref[0]
            h_scr[rs, :] = h.astype(BF16)
            dt = _softplus(jnp.dot(h, wdt_ref[...], precision=HIGHEST, preferred_element_type=F32)
                           + bdt_ref[...])
            da = dt * (-jnp.exp(alog_ref[...]))
            dtT = _softplus(_nt_dot(wdtT_ref[...], h, precision=HIGHEST) + bdtT_ref[...])
            daT = dtT * (-jnp.exp(alogT_ref[...]))
            for d in range(2):
                sl = slice(d * heads, (d + 1) * heads)
                dt_ref[d, 0, rs, :] = jnp.concatenate([dt[:, sl], da[:, sl]], axis=-1)
                dtT_ref[d, 0, :, rs] = jnp.concatenate([dtT[sl, :], daT[sl, :]], axis=0)
        zeros = jnp.zeros((pad, pad_scr.shape[1]), F32)
        pad_scr[0:pad, :] = zeros
        pad_scr[pad + seq:2 * pad + seq, :] = zeros

    acc = jnp.dot(h_scr[...], w_ref[...], preferred_element_type=F32)

    @pl.when(j < n_plain)
    def _():
        o_ref[0] = acc.astype(o_ref.dtype)

    @pl.when(j >= n_plain)
    def _():
        pad_scr[pad:pad + seq, :] = acc
        half = (SSD_CONV - 1) // 2
        for r in range(0, seq, row_chunk):
            s = cb_ref[...]
            for k in range(SSD_CONV):
                lo = pad + r + k - half
                s = s + cw_ref[k:k + 1, :] * pad_scr[lo:lo + row_chunk, :]
            o_ref[0, r:r + row_chunk, :] = _silu(s).astype(o_ref.dtype)


def _ssd_inproj(x3, g, shift, scale, w_main, conv_w_full, conv_b_full, w_dt, dt_bias, a_log,
                *, d_inner, heads):
    bsz, seq, d = x3.shape
    n = w_main.shape[1]
    tn = 512
    nh2 = 2 * heads
    per_batch = shift.shape[0] > 1
    mod_map = (lambda b, j: (b, 0, 0)) if per_batch else (lambda b, j: (0, 0, 0))
    kern = functools.partial(_ssd_inproj_kernel, n_plain=d_inner // tn, seq=seq, heads=heads,
                             row_chunk=min(256, seq))
    const2 = lambda b, j: (0, 0)
    return pl.pallas_call(
        kern,
        grid=(bsz, n // tn),
        in_specs=[pl.BlockSpec((1, seq, d), lambda b, j: (b, 0, 0), pipeline_mode=pl.Buffered(1)),
                  pl.BlockSpec((1, d), const2),
                  pl.BlockSpec((1, 1, d), mod_map),
                  pl.BlockSpec((1, 1, d), mod_map),
                  pl.BlockSpec((d, tn), lambda b, j: (0, j)),
                  pl.BlockSpec((SSD_CONV, tn), lambda b, j: (0, j)),
                  pl.BlockSpec((1, tn), lambda b, j: (0, j)),
                  pl.BlockSpec((d, nh2), const2),
                  pl.BlockSpec((nh2, d), const2),
                  pl.BlockSpec((1, nh2), const2),
                  pl.BlockSpec((nh2, 1), const2),
                  pl.BlockSpec((1, nh2), const2),
                  pl.BlockSpec((nh2, 1), const2)],
        out_specs=[pl.BlockSpec((1, seq, tn), lambda b, j: (b, 0, j)),
                   pl.BlockSpec((2, 1, seq, nh2), lambda b, j: (0, b, 0, 0)),
                   pl.BlockSpec((2, 1, nh2, seq), lambda b, j: (0, b, 0, 0))],
        out_shape=[jax.ShapeDtypeStruct((bsz, seq, n), BF16),
                   jax.ShapeDtypeStruct((2, bsz, seq, nh2), F32),
                   jax.ShapeDtypeStruct((2, bsz, nh2, seq), F32)],
        scratch_shapes=[pltpu.VMEM((seq, d), BF16), pltpu.VMEM((seq + 16, tn), F32)],
        compiler_params=_params("arbitrary", "arbitrary"),
        name="ssd_inproj",
    )(x3, g.reshape(1, d), shift, scale, w_main, conv_w_full, conv_b_full.reshape(1, n),
      w_dt, w_dt.T, dt_bias.reshape(1, nh2), dt_bias.reshape(nh2, 1),
      a_log.reshape(1, nh2), a_log.reshape(nh2, 1))


def _ssd_scan_kernel(xs_ref, b_ref, c_ref, dtc_ref, dtcT_ref, h0_ref, y_ref, hfin_ref,
                     st_scr, xw_scr, *, heads, n_chunks):
    d = pl.program_id(1)
    k = pl.program_id(2)
    L = xs_ref.shape[1]
    hpg = heads // SSD_GROUPS
    pairs_per_group = hpg // 2
    gw = hpg * SSD_HEADDIM

    @pl.when(k == 0)
    def _():
        st_scr[...] = h0_ref[0, 0]

    row = lax.broadcasted_iota(jnp.int32, (L, L), 0)
    col = lax.broadcasted_iota(jnp.int32, (L, L), 1)
    mask = jnp.where(d == 0, row - col, col - row) >= 0
    mask_f = mask.astype(F32)

    dtc = dtc_ref[0, 0]
    dtcT = dtcT_ref[0, 0]
    dt, da = dtc[:, :heads], dtc[:, heads:]
    dtT, daT = dtcT[:heads, :], dtcT[heads:, :]
    cum = jnp.dot(mask_f, da, precision=HIGHEST, preferred_element_type=F32)
    cumT = _nt_dot(daT, mask_f, precision=HIGHEST)
    tot = jnp.sum(da, axis=0, keepdims=True)
    ecum = jnp.exp(cum)
    etot = jnp.exp(tot)
    wgt = jnp.exp(tot - cum) * dt

    lane = lax.broadcasted_iota(jnp.int32, (L, 2 * SSD_HEADDIM), 1)
    lo = lane < SSD_HEADDIM
    lo_row = lax.broadcasted_iota(jnp.int32, (1, 2 * SSD_HEADDIM), 1) < SSD_HEADDIM

    for g in range(SSD_GROUPS):
        bg = b_ref[0, :, g * SSD_STATE:(g + 1) * SSD_STATE]
        cg = c_ref[0, :, g * SSD_STATE:(g + 1) * SSD_STATE]
        scores = _nt_dot(cg, bg)
        yoff = jnp.dot(cg, st_scr[g].astype(BF16), preferred_element_type=F32)
        for q in range(pairs_per_group):
            p = g * pairs_per_group + q
            h0, h1 = 2 * p, 2 * p + 1
            xs2 = xs_ref[0, :, p * 128:(p + 1) * 128]
            ms = []
            for h in (h0, h1):
                diff = cum[:, h:h + 1] - cumT[h:h + 1, :]
                dec = jnp.exp(jnp.where(mask, diff, NEG_BIG))
                ms.append((scores * dec * dtT[h:h + 1, :]).astype(BF16))
            zero = jnp.zeros_like(xs2)
            rhs = jnp.concatenate([jnp.where(lo, xs2, zero), jnp.where(lo, zero, xs2)], axis=0)
            ydiag = jnp.dot(jnp.concatenate(ms, axis=1), rhs, preferred_element_type=F32)
            e2 = jnp.where(lo, ecum[:, h0:h0 + 1], ecum[:, h1:h1 + 1])
            y2 = ydiag + yoff[:, q * 128:(q + 1) * 128] * e2
            y_ref[0, 0, :, p * 128:(p + 1) * 128] = y2.astype(y_ref.dtype)
            w2 = jnp.where(lo, wgt[:, h0:h0 + 1], wgt[:, h1:h1 + 1])
            xw_scr[:, q * 128:(q + 1) * 128] = (xs2.astype(F32) * w2).astype(BF16)
        bgT = bg.astype(F32).T.astype(BF16)
        new = jnp.dot(bgT, xw_scr[...], preferred_element_type=F32)
        for q in range(pairs_per_group):
            p = g * pairs_per_group + q
            et2 = jnp.where(lo_row, etot[:, 2 * p:2 * p + 1], etot[:, 2 * p + 1:2 * p + 2])
            sl = slice(q * 128, (q + 1) * 128)
            st_scr[g, :, sl] = st_scr[g, :, sl] * et2 + new[:, sl]

    @pl.when(k == n_chunks - 1)
    def _():
        hfin_ref[0, 0] = st_scr[...]


def _ssd_scan(zx, dtc, dtcT, h0, *, d_inner, heads):
    bsz, seq, _ = zx.shape
    L = min(SSD_CHUNK, seq)
    nc = seq // L
    bc_w = SSD_GROUPS * SSD_STATE
    gw = (heads // SSD_GROUPS) * SSD_HEADDIM
    nh2 = 2 * heads
    xs_blk = d_inner // d_inner
    b_blk = (2 * d_inner) // bc_w
    c_blk = b_blk + 1

    def chunk(d, k):
        return k + d * (nc - 1 - 2 * k)

    kern = functools.partial(_ssd_scan_kernel, heads=heads, n_chunks=nc)
    st_shape = (SSD_GROUPS, SSD_STATE, gw)
    return pl.pallas_call(
        kern,
        grid=(bsz, 2, nc),
        in_specs=[pl.BlockSpec((1, L, d_inner), lambda b, d, k: (b, chunk(d, k), xs_blk)),
                  pl.BlockSpec((1, L, bc_w), lambda b, d, k: (b, chunk(d, k), b_blk)),
                  pl.BlockSpec((1, L, bc_w), lambda b, d, k: (b, chunk(d, k), c_blk)),
                  pl.BlockSpec((1, 1, L, nh2), lambda b, d, k: (d, b, chunk(d, k), 0)),
                  pl.BlockSpec((1, 1, nh2, L), lambda b, d, k: (d, b, 0, chunk(d, k))),
                  pl.BlockSpec((1, 1) + st_shape, lambda b, d, k: (b, d, 0, 0, 0))],
        out_specs=[pl.BlockSpec((1, 1, L, d_inner), lambda b, d, k: (d, b, chunk(d, k), 0)),
                   pl.BlockSpec((1, 1) + st_shape, lambda b, d, k: (b, d, 0, 0, 0))],
        out_shape=[jax.ShapeDtypeStruct((2, bsz, seq, d_inner), BF16),
                   jax.ShapeDtypeStruct((bsz, 2) + st_shape, F32)],
        scratch_shapes=[pltpu.VMEM(st_shape, F32), pltpu.VMEM((L, gw), BF16)],
        compiler_params=_params("arbitrary", "arbitrary", "arbitrary"),
        name="ssd_scan",
    )(zx, zx, zx, dtc, dtcT, h0)


def _ssd_out_kernel(y_ref, z_ref, xs_ref, dsk_ref, nw_ref, w_ref, x_ref, gate_ref, o_ref):
    y = y_ref[0].astype(F32) + y_ref[1].astype(F32) + dsk_ref[...] * xs_ref[...].astype(F32)
    yg = y * _silu(z_ref[...].astype(F32))
    yn = _rms(yg, nw_ref[...]).astype(BF16)
    out = jnp.dot(yn, w_ref[...], preferred_element_type=F32)
    o_ref[...] = x_ref[...] + gate_ref[0] * out


def _ssd_out(y, zx, d_exp, norm_w, w_out, x2, gate, *, seq, d_inner):
    m, d = x2.shape
    tm = min(512, seq)
    bpt = seq // tm
    per_batch = gate.shape[0] > 1
    gate_map = (lambda i: (i // bpt, 0, 0)) if per_batch else (lambda i: (0, 0, 0))
    return pl.pallas_call(
        _ssd_out_kernel,
        grid=(m // tm,),
        in_specs=[pl.BlockSpec((2, tm, d_inner), lambda i: (0, i, 0)),
                  pl.BlockSpec((tm, d_inner), lambda i: (i, 0)),
                  pl.BlockSpec((tm, d_inner), lambda i: (i, 1)),
                  pl.BlockSpec((1, d_inner), lambda i: (0, 0)),
                  pl.BlockSpec((1, d_inner), lambda i: (0, 0)),
                  pl.BlockSpec((d_inner, d), lambda i: (0, 0)),
                  pl.BlockSpec((tm, d), lambda i: (i, 0)),
                  pl.BlockSpec((1, 1, d), gate_map)],
        out_specs=pl.BlockSpec((tm, d), lambda i: (i, 0)),
        out_shape=jax.ShapeDtypeStruct((m, d), F32),
        compiler_params=_params("arbitrary"),
        name="ssd_out",
    )(y, zx, zx, d_exp.reshape(1, d_inner), norm_w.reshape(1, d_inner), w_out, x2, gate)


def _ffn_kernel(x_ref, g_ref, sh_ref, sc_ref, gate_ref, wg_ref, wu_ref, wo_ref, o_ref,
                h_scr, acc_scr, *, n_f):
    j = pl.program_id(1)

    @pl.when(j == 0)
    def _():
        h = _rms(x_ref[...], g_ref[...]) * (1.0 + sc_ref[0]) + sh_ref[0]
        h_scr[...] = h.astype(BF16)
        acc_scr[...] = jnp.zeros_like(acc_scr)

    h = h_scr[...]
    gv = jnp.dot(h, wg_ref[...], preferred_element_type=F32)
    uv = jnp.dot(h, wu_ref[...], preferred_element_type=F32)
    a = (_silu(gv) * uv).astype(BF16)
    acc_scr[...] += jnp.dot(a, wo_ref[...], preferred_element_type=F32)

    @pl.when(j == n_f - 1)
    def _():
        o_ref[...] = x_ref[...] + gate_ref[0] * acc_scr[...]


def _ffn(x2, g, shift, scale, gate, w_in, w_out, *, seq):
    m, d = x2.shape
    f = w_out.shape[0]
    tm = min(512, seq)
    bpt = seq // tm
    tf = f // 2 if (f // 2) % 128 == 0 else f
    n_f = f // tf
    per_batch = shift.shape[0] > 1
    mod_map = (lambda i, j: (i // bpt, 0, 0)) if per_batch else (lambda i, j: (0, 0, 0))
    kern = functools.partial(_ffn_kernel, n_f=n_f)
    return pl.pallas_call(
        kern,
        grid=(m // tm, n_f),
        in_specs=[pl.BlockSpec((tm, d), lambda i, j: (i, 0)),
                  pl.BlockSpec((1, d), lambda i, j: (0, 0)),
                  pl.BlockSpec((1, 1, d), mod_map),
                  pl.BlockSpec((1, 1, d), mod_map),
                  pl.BlockSpec((1, 1, d), mod_map),
                  pl.BlockSpec((d, tf), lambda i, j: (0, j)),
                  pl.BlockSpec((d, tf), lambda i, j: (0, n_f + j)),
                  pl.BlockSpec((tf, d), lambda i, j: (j, 0))],
        out_specs=pl.BlockSpec((tm, d), lambda i, j: (i, 0)),
        out_shape=jax.ShapeDtypeStruct((m, d), F32),
        scratch_shapes=[pltpu.VMEM((tm, d), BF16), pltpu.VMEM((tm, d), F32)],
        compiler_params=_params("arbitrary", "arbitrary"),
        name="dense_swiglu",
    )(x2, g.reshape(1, d), shift, scale, gate, w_in, w_in, w_out)


def _s5_disc_kernel(lre_ref, lim_ref, step_ref, bre_ref, bim_ref, ar_ref, ai_ref, bbr_ref, bbi_ref):
    step = jnp.exp(step_ref[...])
    lre, lim = lre_ref[...], lim_ref[...]
    mag = jnp.exp(lre * step)
    ar = mag * jnp.cos(lim * step)
    ai = mag * jnp.sin(lim * step)
    den = lre * lre + lim * lim
    cr = ((ar - 1.0) * lre + ai * lim) / den
    ci = (ai * lre - (ar - 1.0) * lim) / den
    ar_ref[...] = ar
    ai_ref[...] = ai
    bbr_ref[...] = cr * bre_ref[...] - ci * bim_ref[...]
    bbi_ref[...] = cr * bim_ref[...] + ci * bre_ref[...]


def _s5_discretize(lam_re, lam_im, log_step, b_re, b_im):
    two, g, p = lam_re.shape
    c = b_re.shape[-1]
    lre = jnp.repeat(lam_re, c, axis=-1)
    lim = jnp.repeat(lam_im, c, axis=-1)
    shp = jax.ShapeDtypeStruct((two, g, p * c), F32)
    ar, ai, bbr, bbi = pl.pallas_call(
        _s5_disc_kernel, out_shape=[shp, shp, shp, shp], name="s5_discretize",
    )(lre, lim, log_step.reshape(two, g, 1), b_re.reshape(two, g, p * c), b_im.reshape(two, g, p * c))
    ar = ar.reshape(two, g, p, c)[..., 0]
    ai = ai.reshape(two, g, p, c)[..., 0]
    return ar, ai, bbr.reshape(two, g, p, c), bbi.reshape(two, g, p, c)


def _s5_kernel(x_ref, g_ref, sh_ref, sc_ref, wb_ref, wc_ref, a_ref, s0_ref, y_ref, sfin_ref,
               st_scr, bu_scr, *, n_chunks, steps, bsz):
    d = pl.program_id(0)
    k = pl.program_id(1)
    rows, dm = x_ref.shape
    gl = S5_GB * S5_STATE
    cb = S5_GB * S5_CH

    @pl.when(k == 0)
    def _():
        st_scr[...] = s0_ref[0]

    hn = _rms(x_ref[...], g_ref[...]).reshape(steps, bsz, dm)
    h = hn * (1.0 + sc_ref[...])[None] + sh_ref[...][None]
    u = h.reshape(rows, dm).astype(BF16)

    for gb in range(dm // cb):
        bu_scr[...] = jnp.dot(u[:, gb * cb:(gb + 1) * cb], wb_ref[0, gb], preferred_element_type=F32)
        sl = slice(gb * gl, (gb + 1) * gl)
        ar = jnp.broadcast_to(a_ref[0, 0:1, sl], (bsz, gl))
        ai = jnp.broadcast_to(a_ref[0, 1:2, sl], (bsz, gl))

        def step(i, carry):
            sr, si = carry
            t = jnp.where(d == 0, i, steps - 1 - i)
            r0 = pl.multiple_of(t * bsz, bsz)
            bur = bu_scr[pl.ds(r0, bsz), 0:gl]
            bui = bu_scr[pl.ds(r0, bsz), gl:2 * gl]
            nsr = ar * sr - ai * si + bur
            nsi = ar * si + ai * sr + bui
            bu_scr[pl.ds(r0, bsz), 0:gl] = nsr
            bu_scr[pl.ds(r0, bsz), gl:2 * gl] = nsi
            return nsr, nsi

        sr, si = lax.fori_loop(0, steps, step, (st_scr[0, :, sl], st_scr[1, :, sl]), unroll=2)
        st_scr[0, :, sl] = sr
        st_scr[1, :, sl] = si
        yv = jnp.dot(bu_scr[...].astype(BF16), wc_ref[0, gb], preferred_element_type=F32)
        y_ref[0, :, gb * cb:(gb + 1) * cb] = yv.astype(y_ref.dtype)

    @pl.when(k == n_chunks - 1)
    def _():
        sfin_ref[0] = st_scr[...]


def _s5_scan(xt, g, shift, scale, wb, wc, a, s0, *, bsz):
    m, d = xt.shape
    seq = m // bsz
    steps = min(64, seq)
    rows = steps * bsz
    nc = seq // steps
    nstate = s0.shape[-1]
    n_gb = d // (S5_GB * S5_CH)
    gl = S5_GB * S5_STATE

    def chunk(dd, k):
        return k + dd * (nc - 1 - 2 * k)

    kern = functools.partial(_s5_kernel, n_chunks=nc, steps=steps, bsz=bsz)
    return pl.pallas_call(
        kern,
        grid=(2, nc),
        in_specs=[pl.BlockSpec((rows, d), lambda dd, k: (chunk(dd, k), 0)),
                  pl.BlockSpec((1, d), lambda dd, k: (0, 0)),
                  pl.BlockSpec((bsz, d), lambda dd, k: (0, 0)),
                  pl.BlockSpec((bsz, d), lambda dd, k: (0, 0)),
                  pl.BlockSpec((1, n_gb, S5_GB * S5_CH, 2 * gl), lambda dd, k: (dd, 0, 0, 0)),
                  pl.BlockSpec((1, n_gb, 2 * gl, S5_GB * S5_CH), lambda dd, k: (dd, 0, 0, 0)),
                  pl.BlockSpec((1, 2, nstate), lambda dd, k: (dd, 0, 0)),
                  pl.BlockSpec((1, 2, bsz, nstate), lambda dd, k: (dd, 0, 0, 0))],
        out_specs=[pl.BlockSpec((1, rows, d), lambda dd, k: (dd, chunk(dd, k), 0)),
                   pl.BlockSpec((1, 2, bsz, nstate), lambda dd, k: (dd, 0, 0, 0))],
        out_shape=[jax.ShapeDtypeStruct((2, m, d), BF16),
                   jax.ShapeDtypeStruct((2, 2, bsz, nstate), F32)],
        scratch_shapes=[pltpu.VMEM((2, bsz, nstate), F32), pltpu.VMEM((rows, 2 * gl), F32)],
        compiler_params=_params("arbitrary", "arbitrary"),
        name="s5_scan",
    )(xt, g.reshape(1, d), shift, scale, wb, wc, a, s0)


def _gelu_tanh(v):
    return 0.5 * v * (1.0 + jnp.tanh(0.7978845608028654 * (v + 0.044715 * (v * v * v))))


def _s5_glu_kernel(x_ref, y_ref, g_ref, sh_ref, sc_ref, gate_ref, dsk_ref, w_ref, b_ref, o_ref, *, bsz):
    rows, dm = x_ref.shape
    x = x_ref[...]
    hn = _rms(x, g_ref[...]).reshape(rows // bsz, bsz, dm)
    h = (hn * (1.0 + sc_ref[...])[None] + sh_ref[...][None]).reshape(rows, dm)
    yv = dsk_ref[...] * h + y_ref[0].astype(F32) + y_ref[1].astype(F32)
    ge = _gelu_tanh(yv).astype(BF16)
    o = jnp.dot(ge, w_ref[...], preferred_element_type=F32) + b_ref[...]
    out = o[:, :dm] * jax.nn.sigmoid(o[:, dm:])
    res = x.reshape(rows // bsz, bsz, dm) + gate_ref[...][None] * out.reshape(rows // bsz, bsz, dm)
    o_ref[...] = res.reshape(rows, dm)


def _s5_glu(xt, y, g, shift, scale, gate, d_skip, w_glu, b_glu, *, bsz):
    m, d = xt.shape
    tm = min(512, m)
    kern = functools.partial(_s5_glu_kernel, bsz=bsz)
    full = lambda i: (0, 0)
    return pl.pallas_call(
        kern,
        grid=(m // tm,),
        in_specs=[pl.BlockSpec((tm, d), lambda i: (i, 0)),
                  pl.BlockSpec((2, tm, d), lambda i: (0, i, 0)),
                  pl.BlockSpec((1, d), full),
                  pl.BlockSpec((bsz, d), full),
                  pl.BlockSpec((bsz, d), full),
                  pl.BlockSpec((bsz, d), full),
                  pl.BlockSpec((1, d), full),
                  pl.BlockSpec((d, 2 * d), full),
                  pl.BlockSpec((1, 2 * d), full)],
        out_specs=pl.BlockSpec((tm, d), lambda i: (i, 0)),
        out_shape=jax.ShapeDtypeStruct((m, d), F32),
        compiler_params=_params("arbitrary"),
        name="s5_glu",
    )(xt, y, g.reshape(1, d), shift, scale, gate, d_skip.reshape(1, d), w_glu, b_glu.reshape(1, 2 * d))


MOE_TB = 512
MOE_TILE = 512
MOE_WIN = 256
MOE_PAD = 16
MOE_LANES = 128


def _moe_cap(n_tok, n_blocks):
    rows = n_tok + n_blocks * MOE_PAD + MOE_TILE
    return -(-rows // MOE_TILE) * MOE_TILE


def _moe_route_kernel(x_ref, g_ref, sh_ref, sc_ref, rwT_ref, rbT_ref, ut_ref,
                      sorted_ref, meta_ref, tab_ref,
                      buf_scr, zero_scr, run_smem, sems, *, bsz, n_exp, n_blocks, local_rows):
    b = pl.program_id(0)
    slot = lax.rem(b, 2)
    rows, dm = x_ref.shape

    @pl.when(b == 0)
    def _():
        for e in range(n_exp):
            run_smem[e] = 0
        zero_scr[...] = jnp.zeros_like(zero_scr)

    hn = _rms(x_ref[...], g_ref[...]).reshape(rows // bsz, bsz, dm)
    h = (hn * (1.0 + sc_ref[...])[None] + sh_ref[...][None]).reshape(rows, dm)
    hb = h.astype(BF16)

    logits = _nt_dot(rwT_ref[...], h, precision=HIGHEST) + rbT_ref[...]
    sub = lax.broadcasted_iota(jnp.int32, logits.shape, 0)
    m1 = jnp.max(logits, axis=0, keepdims=True)
    i1 = jnp.min(jnp.where(logits == m1, sub, n_exp), axis=0, keepdims=True)
    rest = jnp.where(sub == i1, -jnp.inf, logits)
    m2 = jnp.max(rest, axis=0, keepdims=True)
    i2 = jnp.min(jnp.where(rest == m2, sub, n_exp), axis=0, keepdims=True)
    e2 = jnp.exp(m2 - m1)
    g1 = 1.0 / (1.0 + e2)
    g2 = e2 * g1
    sel1 = sub == i1
    sel2 = sub == i2
    oh = jnp.where(sel1, 1.0, 0.0) + jnp.where(sel2, 1.0, 0.0)
    cnt = jnp.sum(oh, axis=1, keepdims=True).astype(jnp.int32)
    cpad_v = ((cnt + (MOE_PAD - 1)) // MOE_PAD) * MOE_PAD
    rank = jnp.dot(oh.astype(BF16), ut_ref[...], preferred_element_type=F32)

    cpad, loc, run = [], [], []
    off = 0
    for e in range(n_exp):
        cpad.append(cpad_v[e, 0])
        loc.append(off)
        off = off + cpad[e]
        run.append(run_smem[e])
    sub1 = lax.broadcasted_iota(jnp.int32, (n_exp, 1), 0)
    loc_v = jnp.zeros((n_exp, 1), jnp.int32)
    run_v = jnp.zeros((n_exp, 1), jnp.int32)
    for e in range(n_exp):
        loc_v = jnp.where(sub1 == e, loc[e], loc_v)
        run_v = jnp.where(sub1 == e, run[e], run_v)
    loc_v = loc_v.astype(F32)
    run_v = run_v.astype(F32)

    def pick(sel, v):
        return jnp.sum(jnp.where(sel, v, 0.0), axis=0, keepdims=True)

    lp1 = pick(sel1, loc_v + rank).astype(jnp.int32)
    lp2 = pick(sel2, loc_v + rank).astype(jnp.int32)
    pos1 = pick(sel1, run_v + rank)
    pos2 = pick(sel2, run_v + rank)

    r = lax.broadcasted_iota(jnp.int32, (local_rows, rows), 0)
    onehot = jnp.where(r == lp1, 1.0, jnp.where(r == lp2, 1.0, 0.0)).astype(BF16)
    buf_scr[slot, 0:local_rows, :] = jnp.dot(onehot, hb, preferred_element_type=F32).astype(BF16)
    buf_scr[slot, local_rows:local_rows + MOE_TILE, :] = jnp.zeros((MOE_TILE, dm), BF16)

    rec = jnp.concatenate([i1.astype(F32), i2.astype(F32), pos1, pos2, g1, g2,
                           jnp.zeros((MOE_LANES - 6, rows), F32)], axis=0)
    meta_ref[...] = rec.T

    base = b * (2 * n_exp)
    for e in range(n_exp):
        tab_ref[base + e] = run[e]
        tab_ref[base + n_exp + e] = cpad[e]
        run_smem[e] = run[e] + cpad[e]

    def group_copy(e, slot_, loc_e, run_e):
        src = buf_scr.at[slot_, pl.ds(pl.multiple_of(loc_e, MOE_PAD), MOE_TILE)]
        dst = sorted_ref.at[e, pl.ds(pl.multiple_of(run_e, MOE_PAD), MOE_TILE)]
        return pltpu.make_async_copy(src, dst, sems.at[slot_, e])

    @pl.when(b > 0)
    def _():
        for e in range(n_exp):
            group_copy(e, 1 - slot, 0, 0).wait()

    for e in range(n_exp):
        group_copy(e, slot, loc[e], run[e]).start()

    @pl.when(b == n_blocks - 1)
    def _():
        fin = (n_blocks) * (2 * n_exp)
        for e in range(n_exp):
            group_copy(e, slot, 0, 0).wait()
        for e in range(n_exp):
            tot = run[e] + cpad[e]
            tab_ref[fin + e] = tot
            tab_ref[fin + n_exp + e] = 0
            dst = sorted_ref.at[e, pl.ds(pl.multiple_of(tot, MOE_PAD), MOE_TILE)]
            pltpu.make_async_copy(zero_scr, dst, sems.at[slot, e]).start()
        for e in range(n_exp):
            dst = sorted_ref.at[e, pl.ds(0, MOE_TILE)]
            pltpu.make_async_copy(zero_scr, dst, sems.at[slot, e]).wait()


def _moe_expert_kernel(te_ref, tr_ref, na_ref, xs_ref, wg_ref, wu_ref, wo_ref, y_ref, acc_scr, *, n_f):
    i = pl.program_id(0)
    f = pl.program_id(1)

    @pl.when(i < na_ref[0])
    def _():
        @pl.when(f == 0)
        def _():
            acc_scr[...] = jnp.zeros_like(acc_scr)

        xs = xs_ref[0]
        gv = jnp.dot(xs, wg_ref[0], preferred_element_type=F32)
        uv = jnp.dot(xs, wu_ref[0], preferred_element_type=F32)
        a = (_silu(gv) * uv).astype(BF16)
        acc_scr[...] += jnp.dot(a, wo_ref[0], preferred_element_type=F32)

        @pl.when(f == n_f - 1)
        def _():
            y_ref[0] = acc_scr[...].astype(y_ref.dtype)


def _moe_combine_kernel(tab_ref, end_ref, x_ref, meta_ref, gate_ref, gf_ref, ys_ref, o_ref,
                        win_scr, acc_scr, sems, *, bsz, n_exp):
    b = pl.program_id(0)
    rows, dm = x_ref.shape
    n_win = MOE_TILE // MOE_WIN
    base = b * (2 * n_exp)

    def window(e, c):
        run_e = tab_ref[base + e]
        live = tab_ref[base + n_exp + e] > c * MOE_WIN
        ws = jnp.minimum(run_e + c * MOE_WIN, end_ref[e] - MOE_WIN)
        k = e * n_win + c
        cp = pltpu.make_async_copy(ys_ref.at[e, pl.ds(pl.multiple_of(ws, MOE_PAD), MOE_WIN)],
                                   win_scr.at[k], sems.at[k])
        return live, ws, k, cp

    for e in range(n_exp):
        for c in range(n_win):
            live, _, _, cp = window(e, c)

            @pl.when(live)
            def _():
                cp.start()

    meta = meta_ref[...]
    e1, e2 = meta[:, 0:1], meta[:, 1:2]
    pos1, pos2 = meta[:, 2:3].astype(jnp.int32), meta[:, 3:4].astype(jnp.int32)
    g1, g2 = meta[:, 4:5], meta[:, 5:6]
    acc_scr[...] = jnp.zeros_like(acc_scr)
    lane = lax.broadcasted_iota(jnp.int32, (rows, MOE_WIN), 1)

    for e in range(n_exp):
        for c in range(n_win):
            live, ws, k, cp = window(e, c)

            @pl.when(live)
            def _():
                cp.wait()
                p1 = jnp.where(e1 == e, jnp.where(pos1 - ws == lane, 1.0, 0.0), 0.0)
                p2 = jnp.where(e2 == e, jnp.where(pos2 - ws == lane, 1.0, 0.0), 0.0)
                pp = jnp.concatenate([p1, p2], axis=0).astype(BF16)
                acc_scr[...] += jnp.dot(pp, win_scr[k], preferred_element_type=F32)

    mix = g1 * acc_scr[0:rows, :] + g2 * acc_scr[rows:2 * rows, :]
    upd = gate_ref[...][None] * mix.reshape(rows // bsz, bsz, dm)
    xo = x_ref[...] + upd.reshape(rows, dm)
    o_ref[...] = _rms(xo, gf_ref[...])


def _moe(xt, g, shift, scale, gate, router_w, router_b, w_in, w_out, g_final, *, bsz):
    m, d = xt.shape
    n_exp, f, _ = w_out.shape
    assert m % MOE_TB == 0 and MOE_TB % bsz == 0
    nb = m // MOE_TB
    cap = _moe_cap(m, nb)
    local_rows = -(-(TOP_K * MOE_TB + n_exp * (MOE_PAD - 1)) // 128) * 128
    full1 = lambda i: (0, 0)

    ut = (jnp.arange(MOE_TB)[:, None] < jnp.arange(MOE_TB)[None, :]).astype(BF16)
    route = functools.partial(_moe_route_kernel, bsz=bsz, n_exp=n_exp, n_blocks=nb, local_rows=local_rows)
    sorted_h, meta, tab = pl.pallas_call(
        route,
        grid=(nb,),
        in_specs=[pl.BlockSpec((MOE_TB, d), lambda i: (i, 0)),
                  pl.BlockSpec((1, d), full1),
                  pl.BlockSpec((bsz, d), full1),
                  pl.BlockSpec((bsz, d), full1),
                  pl.BlockSpec((n_exp, d), full1),
                  pl.BlockSpec((n_exp, 1), full1),
                  pl.BlockSpec((MOE_TB, MOE_TB), full1)],
        out_specs=[pl.BlockSpec(memory_space=pl.ANY),
                   pl.BlockSpec((MOE_TB, MOE_LANES), lambda i: (i, 0)),
                   pl.BlockSpec(memory_space=pltpu.SMEM)],
        out_shape=[jax.ShapeDtypeStruct((n_exp, cap, d), BF16),
                   jax.ShapeDtypeStruct((m, MOE_LANES), F32),
                   jax.ShapeDtypeStruct(((nb + 1) * 2 * n_exp,), jnp.int32)],
        scratch_shapes=[pltpu.VMEM((2, local_rows + MOE_TILE, d), BF16),
                        pltpu.VMEM((MOE_TILE, d), BF16),
                        pltpu.SMEM((n_exp,), jnp.int32),
                        pltpu.SemaphoreType.DMA((2, n_exp))],
        compiler_params=_params("arbitrary"),
        name="moe_route",
    )(xt, g.reshape(1, d), shift, scale, router_w.T, router_b.reshape(n_exp, 1), ut)

    tot = tab[nb * 2 * n_exp: nb * 2 * n_exp + n_exp]
    ntile = jnp.maximum((tot + MOE_TILE - 1) // MOE_TILE, 1)
    cum = jnp.cumsum(ntile)
    n_active = cum[-1:]
    max_tiles = (TOP_K * m + nb * n_exp * (MOE_PAD - 1)) // MOE_TILE + n_exp + 1
    idc = jnp.minimum(jnp.arange(max_tiles, dtype=jnp.int32), n_active - 1)
    tile_e = jnp.sum(idc[:, None] >= cum[None, :], axis=1).astype(jnp.int32)
    tile_r = (idc - (cum - ntile)[tile_e]).astype(jnp.int32)
    end = (ntile * MOE_TILE).astype(jnp.int32)

    tf = f // 2 if (f // 2) % 128 == 0 else f
    n_f = f // tf

    def f_idx(i, j, na):
        return jnp.where(i < na[0], j, n_f - 1)

    ys = pl.pallas_call(
        functools.partial(_moe_expert_kernel, n_f=n_f),
        grid_spec=pltpu.PrefetchScalarGridSpec(
            num_scalar_prefetch=3,
            grid=(max_tiles, n_f),
            in_specs=[pl.BlockSpec((1, MOE_TILE, d), lambda i, j, te, tr, na: (te[i], tr[i], 0)),
                      pl.BlockSpec((1, d, tf), lambda i, j, te, tr, na: (te[i], 0, f_idx(i, j, na))),
                      pl.BlockSpec((1, d, tf), lambda i, j, te, tr, na: (te[i], 0, n_f + f_idx(i, j, na))),
                      pl.BlockSpec((1, tf, d), lambda i, j, te, tr, na: (te[i], f_idx(i, j, na), 0))],
            out_specs=pl.BlockSpec((1, MOE_TILE, d), lambda i, j, te, tr, na: (te[i], tr[i], 0)),
            scratch_shapes=[pltpu.VMEM((MOE_TILE, d), F32)]),
        out_shape=jax.ShapeDtypeStruct((n_exp, cap, d), BF16),
        compiler_params=_params("arbitrary", "arbitrary"),
        name="moe_experts",
    )(tile_e, tile_r, n_active.astype(jnp.int32), sorted_h, w_in, w_in, w_out)

    n_win = MOE_TILE // MOE_WIN
    return pl.pallas_call(
        functools.partial(_moe_combine_kernel, bsz=bsz, n_exp=n_exp),
        grid_spec=pltpu.PrefetchScalarGridSpec(
            num_scalar_prefetch=2,
            grid=(nb,),
            in_specs=[pl.BlockSpec((MOE_TB, d), lambda i, tb, en: (i, 0)),
                      pl.BlockSpec((MOE_TB, MOE_LANES), lambda i, tb, en: (i, 0)),
                      pl.BlockSpec((bsz, d), lambda i, tb, en: (0, 0)),
                      pl.BlockSpec((1, d), lambda i, tb, en: (0, 0)),
                      pl.BlockSpec(memory_space=pl.ANY)],
            out_specs=pl.BlockSpec((MOE_TB, d), lambda i, tb, en: (i, 0)),
            scratch_shapes=[pltpu.VMEM((n_exp * n_win, MOE_WIN, d), BF16),
                            pltpu.VMEM((TOP_K * MOE_TB, d), F32),
                            pltpu.SemaphoreType.DMA((n_exp * n_win,))]),
        out_shape=jax.ShapeDtypeStruct((m, d), F32),
        compiler_params=_params("arbitrary"),
        name="moe_combine",
    )(tab, end, xt, meta, gate, g_final.reshape(1, d), ys)


def _blockdiag(w, n_gb):
    two, g, r, c = w.shape
    eye = jnp.eye(S5_GB, dtype=w.dtype)
    out = jnp.einsum("dbgrc,gh->dbgrhc", w.reshape(two, n_gb, S5_GB, r, c), eye)
    return out.reshape(two, n_gb, S5_GB * r, S5_GB * c)


def kernel(x, c, ctx, c_ctx, ada_w, ada_b, norm_mix, norm_ffn, ssd_w_in, ssd_conv_w, ssd_conv_b, ssd_dt_bias, ssd_a_log, ssd_d, ssd_norm, ssd_w_out, s5_lam_re, s5_lam_im, s5_log_step, s5_b_re, s5_b_im, s5_c_re, s5_c_im, s5_d, s5_w_glu, s5_b_glu, ffn_w_in, ffn_w_out, moe_router_w, moe_router_b, moe_w_in, moe_w_out, norm_final):
    bsz, seq, d = x.shape
    ctx_len = ctx.shape[1]
    depth = ada_w.shape[0]
    assert depth == 2, "one SSD layer followed by one S5 layer"
    rows = seq // GRID_W

    pad_rows = (-(bsz + 1)) % 8
    cc = jnp.concatenate([c, c_ctx[None], jnp.zeros((pad_rows, d), F32)], axis=0)
    mods = _ada(cc, ada_w, ada_b)

    def mod(i, n):
        sl = slice(n * d, (n + 1) * d)
        return mods[i, :bsz, sl], mods[i, bsz:bsz + 1, sl]

    heads = ssd_d.shape[1]
    d_inner = heads * SSD_HEADDIM
    conv_dim = ssd_conv_w.shape[2]
    n_main = d_inner + conv_dim
    w_in0 = ssd_w_in[0]
    w_main = w_in0[:, :n_main].astype(BF16)
    w_dt = w_in0[:, n_main:]
    conv_w_full = jnp.concatenate([jnp.zeros((SSD_CONV, d_inner), F32), ssd_conv_w[0]], axis=1)
    conv_b_full = jnp.concatenate([jnp.zeros((d_inner,), F32), ssd_conv_b[0]])
    d_exp = jnp.repeat(ssd_d[0], SSD_HEADDIM)
    w_out0 = ssd_w_out[0].astype(BF16)
    gw = (heads // SSD_GROUPS) * SSD_HEADDIM

    (sh_x, sh_c), (sc_x, sc_c), (gt_x, gt_c) = mod(0, 0), mod(0, 1), mod(0, 2)
    r3 = lambda a: a[:, None, :]
    dt_bias = ssd_dt_bias[0].reshape(-1)
    a_log = ssd_a_log[0].reshape(-1)

    def ssd_mixer(tok3, shift, scale, gate, h0):
        b_, t_, _ = tok3.shape
        zx, dtc, dtcT = _ssd_inproj(tok3, norm_mix[0], r3(shift), r3(scale), w_main, conv_w_full,
                                    conv_b_full, w_dt, dt_bias, a_log, d_inner=d_inner, heads=heads)
        y, hfin = _ssd_scan(zx, dtc, dtcT, h0, d_inner=d_inner, heads=heads)
        out = _ssd_out(y.reshape(2, b_ * t_, d_inner), zx.reshape(b_ * t_, n_main), d_exp, ssd_norm[0],
                       w_out0, tok3.reshape(b_ * t_, d), r3(gate), seq=t_, d_inner=d_inner)
        return out, hfin

    h_zero = jnp.zeros((bsz, 2, SSD_GROUPS, SSD_STATE, gw), F32)
    ctx2, h_ctx = ssd_mixer(ctx, sh_c, sc_c, gt_c, h_zero)
    x2, _ = ssd_mixer(x, sh_x, sc_x, gt_x, h_ctx)

    ffn_in = ffn_w_in[0].astype(BF16)
    ffn_out = ffn_w_out[0].astype(BF16)
    (sh_x, sh_c), (sc_x, sc_c), (gt_x, gt_c) = mod(0, 3), mod(0, 4), mod(0, 5)
    x2 = _ffn(x2, norm_ffn[0], r3(sh_x), r3(sc_x), r3(gt_x), ffn_in, ffn_out, seq=seq)
    ctx2 = _ffn(ctx2, norm_ffn[0], r3(sh_c), r3(sc_c), r3(gt_c), ffn_in, ffn_out, seq=ctx_len)

    xt = x2.reshape(bsz, rows, GRID_W, d).transpose(2, 1, 0, 3).reshape(seq * bsz, d)
    ct = ctx2.reshape(bsz, ctx_len, d).transpose(1, 0, 2).reshape(ctx_len * bsz, d)

    ar, ai, bbr, bbi = _s5_discretize(s5_lam_re[0], s5_lam_im[0], s5_log_step[0], s5_b_re[0], s5_b_im[0])
    n_groups = ar.shape[1]
    n_gb = n_groups // S5_GB
    wb = jnp.concatenate([_blockdiag(bbr.transpose(0, 1, 3, 2), n_gb),
                          _blockdiag(bbi.transpose(0, 1, 3, 2), n_gb)], axis=-1).astype(BF16)
    wc = jnp.concatenate([_blockdiag(s5_c_re[0].transpose(0, 1, 3, 2), n_gb),
                          _blockdiag(-s5_c_im[0].transpose(0, 1, 3, 2), n_gb)], axis=2).astype(BF16)
    a_coef = jnp.stack([ar.reshape(2, -1), ai.reshape(2, -1)], axis=1)
    nstate = a_coef.shape[-1]

    (sh_x, sh_c), (sc_x, sc_c), (gt_x, _) = mod(1, 0), mod(1, 1), mod(1, 2)
    bc = lambda a: jnp.broadcast_to(a, (bsz, d))
    s_zero = jnp.zeros((2, 2, bsz, nstate), F32)
    _, s_ctx = _s5_scan(ct, norm_mix[1], bc(sh_c), bc(sc_c), wb, wc, a_coef, s_zero, bsz=bsz)
    y, _ = _s5_scan(xt, norm_mix[1], sh_x, sc_x, wb, wc, a_coef, s_ctx, bsz=bsz)
    xt = _s5_glu(xt, y, norm_mix[1], sh_x, sc_x, gt_x, s5_d[0], s5_w_glu[0].astype(BF16), s5_b_glu[0], bsz=bsz)

    (sh_x, _), (sc_x, _), (gt_x, _) = mod(1, 3), mod(1, 4), mod(1, 5)
    out_t = _moe(xt, norm_ffn[1], sh_x, sc_x, gt_x, moe_router_w[0], moe_router_b[0],
                 moe_w_in[0].astype(BF16), moe_w_out[0].astype(BF16), norm_final, bsz=bsz)
    return out_t.reshape(GRID_W, rows, bsz, d).transpose(2, 1, 0, 3).reshape(bsz, seq, d)
```

```python
import functools

import jax
import jax.numpy as jnp
from jax import lax
from jax.experimental import pallas as pl
from jax.experimental.pallas import tpu as pltpu

F32 = jnp.float32
BF16 = jnp.bfloat16
HIGHEST = lax.Precision.HIGHEST

EPS = 1e-6
GRID_W = 64
N_MOD = 6

SSD_HEADDIM = 64
SSD_GROUPS = 4
SSD_STATE = 128
SSD_CONV = 5
SSD_CHUNK = 128

S5_CH = 16
S5_STATE = 64
S5_GB = 8

TOP_K = 2

V7X_VMEM_BYTES = 64 * 1024 * 1024
VMEM_LIMIT = V7X_VMEM_BYTES - 8 * 1024 * 1024
NEG_BIG = -1e30
LANES = 128


def _fill_lane_tiles(res, tr_scr):
    for lt in range(res.shape[1] // LANES):
        tr_scr[lt] = res[:, lt * LANES:(lt + 1) * LANES]


def _strided_rows_out(tr_scr, dst, *, n_groups, group_rows, base=0):
    for g in range(n_groups):
        for lt in range(tr_scr.shape[0]):
            dst(g, lt)[...] = tr_scr[lt, pl.ds(base + g, group_rows, stride=n_groups), :]


def _params(*sem):
    return pltpu.CompilerParams(dimension_semantics=sem, vmem_limit_bytes=VMEM_LIMIT)


def _silu(v):
    return v * jax.nn.sigmoid(v)


def _softplus(v):
    return jnp.maximum(v, 0.0) + jnp.log1p(jnp.exp(-jnp.abs(v)))


def _rms(x, g):
    inv = lax.rsqrt(jnp.mean(x * x, axis=-1, keepdims=True) + EPS)
    return (x * inv) * g


def _nt_dot(a, b, precision=None):
    return lax.dot_general(a, b, (((1,), (1,)), ((), ())), precision=precision,
                           preferred_element_type=F32)


def _ada_kernel(c_ref, w_ref, b_ref, o_ref):
    s = _silu(c_ref[...])
    o_ref[0] = jnp.dot(s, w_ref[0], precision=HIGHEST, preferred_element_type=F32) + b_ref[0]


def _ada(cc, ada_w, ada_b):
    depth, d, n = ada_w.shape
    r = cc.shape[0]
    tn = 1536
    return pl.pallas_call(
        _ada_kernel,
        grid=(depth, n // tn),
        in_specs=[pl.BlockSpec((r, d), lambda i, j: (0, 0)),
                  pl.BlockSpec((1, d, tn), lambda i, j: (i, 0, j)),
                  pl.BlockSpec((1, 1, tn), lambda i, j: (i, 0, j))],
        out_specs=pl.BlockSpec((1, r, tn), lambda i, j: (i, 0, j)),
        out_shape=jax.ShapeDtypeStruct((depth, r, n), F32),
        compiler_params=_params("arbitrary", "arbitrary"),
        name="ada_mod",
    )(cc, ada_w, ada_b.reshape(depth, 1, n))


def _ssd_inproj_kernel(x_ref, g_ref, sh_ref, sc_ref, w_ref, cw_ref, cb_ref, wdt_ref, wdtT_ref,
                       bdt_ref, bdtT_ref, alog_ref, alogT_ref,
                       o_ref, dt_ref, dtT_ref, h_scr, pad_scr, *, n_plain, seq, heads, row_chunk):
    j = pl.program_id(1)
    pad = 8

    @pl.when(j == 0)
    def _():
        for r in range(0, seq, row_chunk):
            rs = slice(r, r + row_chunk)
            h = _rms(x_ref[0, rs, :], g_ref[...]) * (1.0 + sc_ref[0]) + sh_ref[0]
            hb = h.astype(BF16)
            h_scr[rs, :] = hb
            hl = (h - hb.astype(F32)).astype(BF16)
            nh2 = 2 * heads
            p = jnp.dot(hb, wdt_ref[...], preferred_element_type=F32)
            q = jnp.dot(hl, wdt_ref[...], preferred_element_type=F32)
            dt = _softplus(p[:, :nh2] + (p[:, nh2:] + q[:, :nh2]) + bdt_ref[...])
            da = dt * (-jnp.exp(alog_ref[...]))
            pT = _nt_dot(wdtT_ref[...], hb)
            qT = _nt_dot(wdtT_ref[...], hl)
            dtT = _softplus(pT[:nh2, :] + (pT[nh2:, :] + qT[:nh2, :]) + bdtT_ref[...])
            daT = dtT * (-jnp.exp(alogT_ref[...]))
            for d in range(2):
                sl = slice(d * heads, (d + 1) * heads)
                dt_ref[d, 0, rs, :] = jnp.concatenate([dt[:, sl], da[:, sl]], axis=-1)
                dtT_ref[d, 0, :, rs] = jnp.concatenate([dtT[sl, :], daT[sl, :]], axis=0)
        zeros = jnp.zeros((pad, pad_scr.shape[1]), F32)
        pad_scr[0:pad, :] = zeros
        pad_scr[pad + seq:2 * pad + seq, :] = zeros

    acc = jnp.dot(h_scr[...], w_ref[...], preferred_element_type=F32)

    @pl.when(j < n_plain)
    def _():
        o_ref[0] = acc.astype(o_ref.dtype)

    @pl.when(j >= n_plain)
    def _():
        pad_scr[pad:pad + seq, :] = acc
        half = (SSD_CONV - 1) // 2
        for r in range(0, seq, row_chunk):
            s = cb_ref[...]
            for k in range(SSD_CONV):
                lo = pad + r + k - half
                s = s + cw_ref[k:k + 1, :] * pad_scr[lo:lo + row_chunk, :]
            o_ref[0, r:r + row_chunk, :] = _silu(s).astype(o_ref.dtype)


def _ssd_inproj(x3, g, shift, scale, w_main, conv_w_full, conv_b_full, w_dt, dt_bias, a_log,
                *, d_inner, heads):
    bsz, seq, d = x3.shape
    n = w_main.shape[1]
    tn = 512
    nh2 = 2 * heads
    per_batch = shift.shape[0] > 1
    mod_map = (lambda b, j: (b, 0, 0)) if per_batch else (lambda b, j: (0, 0, 0))
    kern = functools.partial(_ssd_inproj_kernel, n_plain=d_inner // tn, seq=seq, heads=heads,
                             row_chunk=min(256, seq))
    w_hi = w_dt.astype(BF16)
    w_dt2 = jnp.concatenate([w_hi, (w_dt - w_hi.astype(F32)).astype(BF16)], axis=1)
    const2 = lambda b, j: (0, 0)
    return pl.pallas_call(
        kern,
        grid=(bsz, n // tn),
        in_specs=[pl.BlockSpec((1, seq, d), lambda b, j: (b, 0, 0), pipeline_mode=pl.Buffered(1)),
                  pl.BlockSpec((1, d), const2),
                  pl.BlockSpec((1, 1, d), mod_map),
                  pl.BlockSpec((1, 1, d), mod_map),
                  pl.BlockSpec((d, tn), lambda b, j: (0, j)),
                  pl.BlockSpec((SSD_CONV, tn), lambda b, j: (0, j)),
                  pl.BlockSpec((1, tn), lambda b, j: (0, j)),
                  pl.BlockSpec((d, 2 * nh2), const2),
                  pl.BlockSpec((2 * nh2, d), const2),
                  pl.BlockSpec((1, nh2), const2),
                  pl.BlockSpec((nh2, 1), const2),
                  pl.BlockSpec((1, nh2), const2),
                  pl.BlockSpec((nh2, 1), const2)],
        out_specs=[pl.BlockSpec((1, seq, tn), lambda b, j: (b, 0, j)),
                   pl.BlockSpec((2, 1, seq, nh2), lambda b, j: (0, b, 0, 0)),
                   pl.BlockSpec((2, 1, nh2, seq), lambda b, j: (0, b, 0, 0))],
        out_shape=[jax.ShapeDtypeStruct((bsz, seq, n), BF16),
                   jax.ShapeDtypeStruct((2, bsz, seq, nh2), F32),
                   jax.ShapeDtypeStruct((2, bsz, nh2, seq), F32)],
        scratch_shapes=[pltpu.VMEM((seq, d), BF16), pltpu.VMEM((seq + 16, tn), F32)],
        compiler_params=_params("arbitrary", "arbitrary"),
        name="ssd_inproj",
    )(x3, g.reshape(1, d), shift, scale, w_main, conv_w_full, conv_b_full.reshape(1, n),
      w_dt2, w_dt2.T, dt_bias.reshape(1, nh2), dt_bias.reshape(nh2, 1),
      a_log.reshape(1, nh2), a_log.reshape(nh2, 1))


def _ssd_scan_kernel(xs_ref, b_ref, c_ref, dtc_ref, dtcT_ref, h0_ref, y_ref, hfin_ref,
                     st_scr, xw_scr, *, heads, n_chunks):
    d = pl.program_id(1)
    k = pl.program_id(2)
    L = xs_ref.shape[1]
    hpg = heads // SSD_GROUPS
    pairs_per_group = hpg // 2
    gw = hpg * SSD_HEADDIM

    @pl.when(k == 0)
    def _():
        st_scr[...] = h0_ref[0, 0]

    row = lax.broadcasted_iota(jnp.int32, (L, L), 0)
    col = lax.broadcasted_iota(jnp.int32, (L, L), 1)
    mask = jnp.where(d == 0, row - col, col - row) >= 0
    mask_f = mask.astype(F32)

    dtc = dtc_ref[0, 0]
    dtcT = dtcT_ref[0, 0]
    dt, da = dtc[:, :heads], dtc[:, heads:]
    dtT, daT = dtcT[:heads, :], dtcT[heads:, :]
    cum = jnp.dot(mask_f, da, precision=HIGHEST, preferred_element_type=F32)
    cumT = _nt_dot(daT, mask_f, precision=HIGHEST)
    tot = jnp.sum(da, axis=0, keepdims=True)
    ecum = jnp.exp(cum)
    etot = jnp.exp(tot)
    wgt = jnp.exp(tot - cum) * dt

    lane = lax.broadcasted_iota(jnp.int32, (L, 2 * SSD_HEADDIM), 1)
    lo = lane < SSD_HEADDIM
    lo_row = lax.broadcasted_iota(jnp.int32, (1, 2 * SSD_HEADDIM), 1) < SSD_HEADDIM

    for g in range(SSD_GROUPS):
        bg = b_ref[0, :, g * SSD_STATE:(g + 1) * SSD_STATE]
        cg = c_ref[0, :, g * SSD_STATE:(g + 1) * SSD_STATE]
        scores = _nt_dot(cg, bg)
        yoff = jnp.dot(cg, st_scr[g].astype(BF16), preferred_element_type=F32)
        for q in range(pairs_per_group):
            p = g * pairs_per_group + q
            h0, h1 = 2 * p, 2 * p + 1
            xs2 = xs_ref[0, :, p * 128:(p + 1) * 128]
            ms = []
            for h in (h0, h1):
                diff = cum[:, h:h + 1] - cumT[h:h + 1, :]
                dec = jnp.exp(jnp.where(mask, diff, NEG_BIG))
                ms.append((scores * dec * dtT[h:h + 1, :]).astype(BF16))
            zero = jnp.zeros_like(xs2)
            rhs = jnp.concatenate([jnp.where(lo, xs2, zero), jnp.where(lo, zero, xs2)], axis=0)
            ydiag = jnp.dot(jnp.concatenate(ms, axis=1), rhs, preferred_element_type=F32)
            e2 = jnp.where(lo, ecum[:, h0:h0 + 1], ecum[:, h1:h1 + 1])
            y2 = ydiag + yoff[:, q * 128:(q + 1) * 128] * e2
            y_ref[0, 0, :, p * 128:(p + 1) * 128] = y2.astype(y_ref.dtype)
            w2 = jnp.where(lo, wgt[:, h0:h0 + 1], wgt[:, h1:h1 + 1])
            xw_scr[:, q * 128:(q + 1) * 128] = (xs2.astype(F32) * w2).astype(BF16)
        bgT = bg.astype(F32).T.astype(BF16)
        new = jnp.dot(bgT, xw_scr[...], preferred_element_type=F32)
        for q in range(pairs_per_group):
            p = g * pairs_per_group + q
            et2 = jnp.where(lo_row, etot[:, 2 * p:2 * p + 1], etot[:, 2 * p + 1:2 * p + 2])
            sl = slice(q * 128, (q + 1) * 128)
            st_scr[g, :, sl] = st_scr[g, :, sl] * et2 + new[:, sl]

    @pl.when(k == n_chunks - 1)
    def _():
        hfin_ref[0, 0] = st_scr[...]


def _ssd_scan(zx, dtc, dtcT, h0, *, d_inner, heads):
    bsz, seq, _ = zx.shape
    L = min(SSD_CHUNK, seq)
    nc = seq // L
    bc_w = SSD_GROUPS * SSD_STATE
    gw = (heads // SSD_GROUPS) * SSD_HEADDIM
    nh2 = 2 * heads
    xs_blk = d_inner // d_inner
    b_blk = (2 * d_inner) // bc_w
    c_blk = b_blk + 1

    def chunk(d, k):
        return k + d * (nc - 1 - 2 * k)

    kern = functools.partial(_ssd_scan_kernel, heads=heads, n_chunks=nc)
    st_shape = (SSD_GROUPS, SSD_STATE, gw)
    return pl.pallas_call(
        kern,
        grid=(bsz, 2, nc),
        in_specs=[pl.BlockSpec((1, L, d_inner), lambda b, d, k: (b, chunk(d, k), xs_blk)),
                  pl.BlockSpec((1, L, bc_w), lambda b, d, k: (b, chunk(d, k), b_blk)),
                  pl.BlockSpec((1, L, bc_w), lambda b, d, k: (b, chunk(d, k), c_blk)),
                  pl.BlockSpec((1, 1, L, nh2), lambda b, d, k: (d, b, chunk(d, k), 0)),
                  pl.BlockSpec((1, 1, nh2, L), lambda b, d, k: (d, b, 0, chunk(d, k))),
                  pl.BlockSpec((1, 1) + st_shape, lambda b, d, k: (b, d, 0, 0, 0))],
        out_specs=[pl.BlockSpec((1, 1, L, d_inner), lambda b, d, k: (d, b, chunk(d, k), 0)),
                   pl.BlockSpec((1, 1) + st_shape, lambda b, d, k: (b, d, 0, 0, 0))],
        out_shape=[jax.ShapeDtypeStruct((2, bsz, seq, d_inner), BF16),
                   jax.ShapeDtypeStruct((bsz, 2) + st_shape, F32)],
        scratch_shapes=[pltpu.VMEM(st_shape, F32), pltpu.VMEM((L, gw), BF16)],
        compiler_params=_params("arbitrary", "arbitrary", "arbitrary"),
        name="ssd_scan",
    )(zx, zx, zx, dtc, dtcT, h0)


def _ssd_out_kernel(y_ref, z_ref, xs_ref, dsk_ref, nw_ref, w_ref, x_ref, gate_ref, o_ref):
    y = y_ref[0].astype(F32) + y_ref[1].astype(F32) + dsk_ref[...] * xs_ref[...].astype(F32)
    yg = y * _silu(z_ref[...].astype(F32))
    yn = _rms(yg, nw_ref[...]).astype(BF16)
    out = jnp.dot(yn, w_ref[...], preferred_element_type=F32)
    o_ref[...] = x_ref[...] + gate_ref[0] * out


def _ssd_out(y, zx, d_exp, norm_w, w_out, x2, gate, *, seq, d_inner):
    m, d = x2.shape
    tm = min(512, seq)
    bpt = seq // tm
    per_batch = gate.shape[0] > 1
    gate_map = (lambda i: (i // bpt, 0, 0)) if per_batch else (lambda i: (0, 0, 0))
    return pl.pallas_call(
        _ssd_out_kernel,
        grid=(m // tm,),
        in_specs=[pl.BlockSpec((2, tm, d_inner), lambda i: (0, i, 0)),
                  pl.BlockSpec((tm, d_inner), lambda i: (i, 0)),
                  pl.BlockSpec((tm, d_inner), lambda i: (i, 1)),
                  pl.BlockSpec((1, d_inner), lambda i: (0, 0)),
                  pl.BlockSpec((1, d_inner), lambda i: (0, 0)),
                  pl.BlockSpec((d_inner, d), lambda i: (0, 0)),
                  pl.BlockSpec((tm, d), lambda i: (i, 0)),
                  pl.BlockSpec((1, 1, d), gate_map)],
        out_specs=pl.BlockSpec((tm, d), lambda i: (i, 0)),
        out_shape=jax.ShapeDtypeStruct((m, d), F32),
        compiler_params=_params("arbitrary"),
        name="ssd_out",
    )(y, zx, zx, d_exp.reshape(1, d_inner), norm_w.reshape(1, d_inner), w_out, x2, gate)


def _ffn_kernel(x_ref, g_ref, sh_ref, sc_ref, gate_ref, wg_ref, wu_ref, wo_ref, o_ref,
                h_scr, acc_scr, tr_scr, *, n_f, grid_w):
    j = pl.program_id(1)

    @pl.when(j == 0)
    def _():
        h = _rms(x_ref[...], g_ref[...]) * (1.0 + sc_ref[0]) + sh_ref[0]
        h_scr[...] = h.astype(BF16)
        acc_scr[...] = jnp.zeros_like(acc_scr)

    h = h_scr[...]
    gv = jnp.dot(h, wg_ref[...], preferred_element_type=F32)
    uv = jnp.dot(h, wu_ref[...], preferred_element_type=F32)
    a = (_silu(gv) * uv).astype(BF16)
    acc_scr[...] += jnp.dot(a, wo_ref[...], preferred_element_type=F32)

    @pl.when(j == n_f - 1)
    def _():
        res = x_ref[...] + gate_ref[0] * acc_scr[...]
        if grid_w:
            _fill_lane_tiles(res, tr_scr)
            _strided_rows_out(tr_scr, lambda c, lt: o_ref.at[c, :, lt * LANES:(lt + 1) * LANES],
                              n_groups=grid_w, group_rows=o_ref.shape[1])
        else:
            o_ref[...] = res


def _ffn(x2, g, shift, scale, gate, w_in, w_out, *, seq, time_major_batch=0):
    m, d = x2.shape
    f = w_out.shape[0]
    tm = min(512, seq)
    bpt = seq // tm
    tf = f // 2 if (f // 2) % 128 == 0 else f
    n_f = f // tf
    per_batch = shift.shape[0] > 1
    mod_map = (lambda i, j: (i // bpt, 0, 0)) if per_batch else (lambda i, j: (0, 0, 0))
    if time_major_batch:
        assert tm % GRID_W == 0
        grows = tm // GRID_W
        out_spec = pl.BlockSpec((GRID_W, grows, d), lambda i, j: (0, i % bpt, i // bpt))
        out_shape = jax.ShapeDtypeStruct((GRID_W, seq // GRID_W, time_major_batch * d), F32)
    else:
        out_spec = pl.BlockSpec((tm, d), lambda i, j: (i, 0))
        out_shape = jax.ShapeDtypeStruct((m, d), F32)
    kern = functools.partial(_ffn_kernel, n_f=n_f, grid_w=GRID_W if time_major_batch else 0)
    return pl.pallas_call(
        kern,
        grid=(m // tm, n_f),
        in_specs=[pl.BlockSpec((tm, d), lambda i, j: (i, 0)),
                  pl.BlockSpec((1, d), lambda i, j: (0, 0)),
                  pl.BlockSpec((1, 1, d), mod_map),
                  pl.BlockSpec((1, 1, d), mod_map),
                  pl.BlockSpec((1, 1, d), mod_map),
                  pl.BlockSpec((d, tf), lambda i, j: (0, j)),
                  pl.BlockSpec((d, tf), lambda i, j: (0, n_f + j)),
                  pl.BlockSpec((tf, d), lambda i, j: (j, 0))],
        out_specs=out_spec,
        out_shape=out_shape,
        scratch_shapes=[pltpu.VMEM((tm, d), BF16), pltpu.VMEM((tm, d), F32),
                        pltpu.VMEM((d // LANES, tm if time_major_batch else 8, LANES), F32)],
        compiler_params=_params("arbitrary", "arbitrary"),
        name="dense_swiglu",
    )(x2, g.reshape(1, d), shift, scale, gate, w_in, w_in, w_out)


def _s5_disc_kernel(lre_ref, lim_ref, step_ref, bre_ref, bim_ref, ar_ref, ai_ref, bbr_ref, bbi_ref):
    step = jnp.exp(step_ref[...])
    lre, lim = lre_ref[...], lim_ref[...]
    mag = jnp.exp(lre * step)
    ar = mag * jnp.cos(lim * step)
    ai = mag * jnp.sin(lim * step)
    den = lre * lre + lim * lim
    cr = ((ar - 1.0) * lre + ai * lim) / den
    ci = (ai * lre - (ar - 1.0) * lim) / den
    ar_ref[...] = ar
    ai_ref[...] = ai
    bbr_ref[...] = cr * bre_ref[...] - ci * bim_ref[...]
    bbi_ref[...] = cr * bim_ref[...] + ci * bre_ref[...]


def _s5_discretize(lam_re, lam_im, log_step, b_re, b_im):
    two, g, p = lam_re.shape
    c = b_re.shape[-1]
    lre = jnp.repeat(lam_re, c, axis=-1)
    lim = jnp.repeat(lam_im, c, axis=-1)
    shp = jax.ShapeDtypeStruct((two, g, p * c), F32)
    ar, ai, bbr, bbi = pl.pallas_call(
        _s5_disc_kernel, out_shape=[shp, shp, shp, shp], name="s5_discretize",
    )(lre, lim, log_step.reshape(two, g, 1), b_re.reshape(two, g, p * c), b_im.reshape(two, g, p * c))
    ar = ar.reshape(two, g, p, c)[..., 0]
    ai = ai.reshape(two, g, p, c)[..., 0]
    return ar, ai, bbr.reshape(two, g, p, c), bbi.reshape(two, g, p, c)


def _s5_kernel(x_ref, g_ref, sh_ref, sc_ref, wb_ref, wc_ref, a_ref, s0_ref, y_ref, sfin_ref,
               st_scr, bu0_scr, bu1_scr, sb0_scr, sb1_scr, *, n_chunks, steps, bsz):
    bu_bufs = (bu0_scr, bu1_scr)
    sb_bufs = (sb0_scr, sb1_scr)
    d = pl.program_id(0)
    k = pl.program_id(1)
    rows, dm = x_ref.shape
    gl = S5_GB * S5_STATE
    cb = S5_GB * S5_CH

    @pl.when(k == 0)
    def _():
        st_scr[...] = s0_ref[0]

    hn = _rms(x_ref[...], g_ref[...]).reshape(steps, bsz, dm)
    h = hn * (1.0 + sc_ref[...])[None] + sh_ref[...][None]
    u = h.reshape(rows, dm).astype(BF16)

    n_gb = dm // cb

    def bu_dot(gb):
        return jnp.dot(u[:, gb * cb:(gb + 1) * cb], wb_ref[0, gb], preferred_element_type=F32)

    bu_bufs[0][...] = bu_dot(0)
    for gb in range(n_gb):
        bu_cur, bu_nxt = bu_bufs[gb % 2], bu_bufs[1 - gb % 2]
        sb_cur = sb_bufs[gb % 2]
        if gb + 1 < n_gb:
            bu_nxt[...] = bu_dot(gb + 1)
        sl = slice(gb * gl, (gb + 1) * gl)
        ar = jnp.broadcast_to(a_ref[0, 0:1, sl], (bsz, gl))
        ai = jnp.broadcast_to(a_ref[0, 1:2, sl], (bsz, gl))
        sr, si = st_scr[0, :, sl], st_scr[1, :, sl]
        for i in range(steps):
            t = jnp.where(d == 0, i, steps - 1 - i)
            r0 = pl.multiple_of(t * bsz, bsz)
            bur = bu_cur[pl.ds(r0, bsz), 0:gl]
            bui = bu_cur[pl.ds(r0, bsz), gl:2 * gl]
            sr, si = ar * sr - ai * si + bur, ar * si + ai * sr + bui
            sb_cur[pl.ds(r0, bsz), 0:gl] = sr.astype(BF16)
            sb_cur[pl.ds(r0, bsz), gl:2 * gl] = si.astype(BF16)
        st_scr[0, :, sl] = sr
        st_scr[1, :, sl] = si
        yv = jnp.dot(sb_cur[...], wc_ref[0, gb], preferred_element_type=F32)
        y_ref[0, :, gb * cb:(gb + 1) * cb] = yv.astype(y_ref.dtype)

    @pl.when(k == n_chunks - 1)
    def _():
        sfin_ref[0] = st_scr[...]


def _s5_scan(xt, g, shift, scale, wb, wc, a, s0, *, bsz):
    m, d = xt.shape
    seq = m // bsz
    steps = min(64, seq)
    rows = steps * bsz
    nc = seq // steps
    nstate = s0.shape[-1]
    n_gb = d // (S5_GB * S5_CH)
    gl = S5_GB * S5_STATE

    def chunk(dd, k):
        return k + dd * (nc - 1 - 2 * k)

    kern = functools.partial(_s5_kernel, n_chunks=nc, steps=steps, bsz=bsz)
    return pl.pallas_call(
        kern,
        grid=(2, nc),
        in_specs=[pl.BlockSpec((rows, d), lambda dd, k: (chunk(dd, k), 0)),
                  pl.BlockSpec((1, d), lambda dd, k: (0, 0)),
                  pl.BlockSpec((bsz, d), lambda dd, k: (0, 0)),
                  pl.BlockSpec((bsz, d), lambda dd, k: (0, 0)),
                  pl.BlockSpec((1, n_gb, S5_GB * S5_CH, 2 * gl), lambda dd, k: (dd, 0, 0, 0)),
                  pl.BlockSpec((1, n_gb, 2 * gl, S5_GB * S5_CH), lambda dd, k: (dd, 0, 0, 0)),
                  pl.BlockSpec((1, 2, nstate), lambda dd, k: (dd, 0, 0)),
                  pl.BlockSpec((1, 2, bsz, nstate), lambda dd, k: (dd, 0, 0, 0))],
        out_specs=[pl.BlockSpec((1, rows, d), lambda dd, k: (dd, chunk(dd, k), 0)),
                   pl.BlockSpec((1, 2, bsz, nstate), lambda dd, k: (dd, 0, 0, 0))],
        out_shape=[jax.ShapeDtypeStruct((2, m, d), BF16),
                   jax.ShapeDtypeStruct((2, 2, bsz, nstate), F32)],
        scratch_shapes=[pltpu.VMEM((2, bsz, nstate), F32),
                        pltpu.VMEM((rows, 2 * gl), F32), pltpu.VMEM((rows, 2 * gl), F32),
                        pltpu.VMEM((rows, 2 * gl), BF16), pltpu.VMEM((rows, 2 * gl), BF16)],
        compiler_params=_params("arbitrary", "arbitrary"),
        name="s5_scan",
    )(xt, g.reshape(1, d), shift, scale, wb, wc, a, s0)


def _gelu_tanh(v):
    return 0.5 * v * (1.0 + jnp.tanh(0.7978845608028654 * (v + 0.044715 * (v * v * v))))


def _s5_glu_kernel(x_ref, y_ref, g_ref, sh_ref, sc_ref, gate_ref, dsk_ref, w_ref, b_ref, o_ref, *, bsz):
    rows, dm = x_ref.shape
    x = x_ref[...]
    hn = _rms(x, g_ref[...]).reshape(rows // bsz, bsz, dm)
    h = (hn * (1.0 + sc_ref[...])[None] + sh_ref[...][None]).reshape(rows, dm)
    yv = dsk_ref[...] * h + y_ref[0].astype(F32) + y_ref[1].astype(F32)
    ge = _gelu_tanh(yv).astype(BF16)
    o = jnp.dot(ge, w_ref[...], preferred_element_type=F32) + b_ref[...]
    out = o[:, :dm] * jax.nn.sigmoid(o[:, dm:])
    res = x.reshape(rows // bsz, bsz, dm) + gate_ref[...][None] * out.reshape(rows // bsz, bsz, dm)
    o_ref[...] = res.reshape(rows, dm)


def _s5_glu(xt, y, g, shift, scale, gate, d_skip, w_glu, b_glu, *, bsz):
    m, d = xt.shape
    tm = min(512, m)
    kern = functools.partial(_s5_glu_kernel, bsz=bsz)
    full = lambda i: (0, 0)
    return pl.pallas_call(
        kern,
        grid=(m // tm,),
        in_specs=[pl.BlockSpec((tm, d), lambda i: (i, 0)),
                  pl.BlockSpec((2, tm, d), lambda i: (0, i, 0)),
                  pl.BlockSpec((1, d), full),
                  pl.BlockSpec((bsz, d), full),
                  pl.BlockSpec((bsz, d), full),
                  pl.BlockSpec((bsz, d), full),
                  pl.BlockSpec((1, d), full),
                  pl.BlockSpec((d, 2 * d), full),
                  pl.BlockSpec((1, 2 * d), full)],
        out_specs=pl.BlockSpec((tm, d), lambda i: (i, 0)),
        out_shape=jax.ShapeDtypeStruct((m, d), F32),
        compiler_params=_params("arbitrary"),
        name="s5_glu",
    )(xt, y, g.reshape(1, d), shift, scale, gate, d_skip.reshape(1, d), w_glu, b_glu.reshape(1, 2 * d))


MOE_TB = 512
MOE_TILE = 512
MOE_WIN = 256
MOE_PAD = 16
MOE_LANES = 128


def _moe_cap(n_tok, n_blocks):
    rows = n_tok + n_blocks * MOE_PAD + MOE_TILE
    return -(-rows // MOE_TILE) * MOE_TILE


def _moe_route_kernel(x_ref, g_ref, sh_ref, sc_ref, rwT_ref, rbT_ref, ut_ref,
                      sorted_ref, meta_ref, tab_ref,
                      buf_scr, zero_scr, run_smem, sems, *, bsz, n_exp, n_blocks, local_rows):
    b = pl.program_id(0)
    slot = lax.rem(b, 2)
    rows, dm = x_ref.shape

    @pl.when(b == 0)
    def _():
        for e in range(n_exp):
            run_smem[e] = 0
        zero_scr[...] = jnp.zeros_like(zero_scr)

    hn = _rms(x_ref[...], g_ref[...]).reshape(rows // bsz, bsz, dm)
    h = (hn * (1.0 + sc_ref[...])[None] + sh_ref[...][None]).reshape(rows, dm)
    hb = h.astype(BF16)

    logits = _nt_dot(rwT_ref[...], h, precision=HIGHEST) + rbT_ref[...]
    sub = lax.broadcasted_iota(jnp.int32, logits.shape, 0)
    m1 = jnp.max(logits, axis=0, keepdims=True)
    i1 = jnp.min(jnp.where(logits == m1, sub, n_exp), axis=0, keepdims=True)
    rest = jnp.where(sub == i1, -jnp.inf, logits)
    m2 = jnp.max(rest, axis=0, keepdims=True)
    i2 = jnp.min(jnp.where(rest == m2, sub, n_exp), axis=0, keepdims=True)
    e2 = jnp.exp(m2 - m1)
    g1 = 1.0 / (1.0 + e2)
    g2 = e2 * g1
    sel1 = sub == i1
    sel2 = sub == i2
    oh = jnp.where(sel1, 1.0, 0.0) + jnp.where(sel2, 1.0, 0.0)
    cnt = jnp.sum(oh, axis=1, keepdims=True).astype(jnp.int32)
    cpad_v = ((cnt + (MOE_PAD - 1)) // MOE_PAD) * MOE_PAD
    rank = jnp.dot(oh.astype(BF16), ut_ref[...], preferred_element_type=F32)

    cpad, loc, run = [], [], []
    off = 0
    for e in range(n_exp):
        cpad.append(cpad_v[e, 0])
        loc.append(off)
        off = off + cpad[e]
        run.append(run_smem[e])
    sub1 = lax.broadcasted_iota(jnp.int32, (n_exp, 1), 0)
    loc_v = jnp.zeros((n_exp, 1), jnp.int32)
    run_v = jnp.zeros((n_exp, 1), jnp.int32)
    for e in range(n_exp):
        loc_v = jnp.where(sub1 == e, loc[e], loc_v)
        run_v = jnp.where(sub1 == e, run[e], run_v)
    loc_v = loc_v.astype(F32)
    run_v = run_v.astype(F32)

    def pick(sel, v):
        return jnp.sum(jnp.where(sel, v, 0.0), axis=0, keepdims=True)

    lp1 = pick(sel1, loc_v + rank).astype(jnp.int32)
    lp2 = pick(sel2, loc_v + rank).astype(jnp.int32)
    pos1 = pick(sel1, run_v + rank)
    pos2 = pick(sel2, run_v + rank)

    r = lax.broadcasted_iota(jnp.int32, (local_rows, rows), 0)
    onehot = jnp.where(r == lp1, 1.0, jnp.where(r == lp2, 1.0, 0.0)).astype(BF16)
    buf_scr[slot, 0:local_rows, :] = jnp.dot(onehot, hb, preferred_element_type=F32).astype(BF16)
    buf_scr[slot, local_rows:local_rows + MOE_TILE, :] = jnp.zeros((MOE_TILE, dm), BF16)

    rec = jnp.concatenate([i1.astype(F32), i2.astype(F32), pos1, pos2, g1, g2,
                           jnp.zeros((MOE_LANES - 6, rows), F32)], axis=0)
    meta_ref[...] = rec.T

    base = b * (2 * n_exp)
    for e in range(n_exp):
        tab_ref[base + e] = run[e]
        tab_ref[base + n_exp + e] = cpad[e]
        run_smem[e] = run[e] + cpad[e]

    def group_copy(e, slot_, loc_e, run_e):
        src = buf_scr.at[slot_, pl.ds(pl.multiple_of(loc_e, MOE_PAD), MOE_TILE)]
        dst = sorted_ref.at[e, pl.ds(pl.multiple_of(run_e, MOE_PAD), MOE_TILE)]
        return pltpu.make_async_copy(src, dst, sems.at[slot_, e])

    @pl.when(b > 0)
    def _():
        for e in range(n_exp):
            group_copy(e, 1 - slot, 0, 0).wait()

    for e in range(n_exp):
        group_copy(e, slot, loc[e], run[e]).start()

    @pl.when(b == n_blocks - 1)
    def _():
        fin = (n_blocks) * (2 * n_exp)
        for e in range(n_exp):
            group_copy(e, slot, 0, 0).wait()
        for e in range(n_exp):
            tot = run[e] + cpad[e]
            tab_ref[fin + e] = tot
            tab_ref[fin + n_exp + e] = 0
            dst = sorted_ref.at[e, pl.ds(pl.multiple_of(tot, MOE_PAD), MOE_TILE)]
            pltpu.make_async_copy(zero_scr, dst, sems.at[slot, e]).start()
        for e in range(n_exp):
            dst = sorted_ref.at[e, pl.ds(0, MOE_TILE)]
            pltpu.make_async_copy(zero_scr, dst, sems.at[slot, e]).wait()


def _moe_expert_kernel(te_ref, tr_ref, na_ref, xs_ref, wg_ref, wu_ref, wo_ref, y_ref, acc_scr, *, n_f):
    i = pl.program_id(0)
    f = pl.program_id(1)

    @pl.when(i < na_ref[0])
    def _():
        @pl.when(f == 0)
        def _():
            acc_scr[...] = jnp.zeros_like(acc_scr)

        xs = xs_ref[0]
        gv = jnp.dot(xs, wg_ref[0], preferred_element_type=F32)
        uv = jnp.dot(xs, wu_ref[0], preferred_element_type=F32)
        a = (_silu(gv) * uv).astype(BF16)
        acc_scr[...] += jnp.dot(a, wo_ref[0], preferred_element_type=F32)

        @pl.when(f == n_f - 1)
        def _():
            y_ref[0] = acc_scr[...].astype(y_ref.dtype)


def _moe_combine_kernel(tab_ref, end_ref, x_ref, meta_ref, gate_ref, gf_ref, ys_ref, o_ref,
                        win_scr, acc_scr, tr_scr, sems, *, bsz, n_exp, grid_rows):
    b = pl.program_id(0)
    slot = lax.rem(b, 2)
    rows, dm = x_ref.shape
    n_win = MOE_TILE // MOE_WIN

    def window(blk, slot_, e, c):
        base = blk * (2 * n_exp)
        run_e = tab_ref[base + e]
        live = tab_ref[base + n_exp + e] > c * MOE_WIN
        ws = jnp.minimum(run_e + c * MOE_WIN, end_ref[e] - MOE_WIN)
        k = e * n_win + c
        cp = pltpu.make_async_copy(ys_ref.at[e, pl.ds(pl.multiple_of(ws, MOE_PAD), MOE_WIN)],
                                   win_scr.at[slot_, k], sems.at[slot_, k])
        return live, ws, k, cp

    def fetch(blk, slot_):
        for e in range(n_exp):
            for c in range(n_win):
                live, _, _, cp = window(blk, slot_, e, c)

                @pl.when(live)
                def _():
                    cp.start()

    @pl.when(b == 0)
    def _():
        fetch(0, 0)

    fetch(b + 1, 1 - slot)

    meta = meta_ref[...]
    e1, e2 = meta[:, 0:1], meta[:, 1:2]
    pos1, pos2 = meta[:, 2:3].astype(jnp.int32), meta[:, 3:4].astype(jnp.int32)
    g1, g2 = meta[:, 4:5], meta[:, 5:6]
    acc_scr[...] = jnp.zeros_like(acc_scr)
    lane = lax.broadcasted_iota(jnp.int32, (rows, MOE_WIN), 1)

    for e in range(n_exp):
        for c in range(n_win):
            live, ws, k, cp = window(b, slot, e, c)

            @pl.when(live)
            def _():
                cp.wait()
                p1 = jnp.where(e1 == e, jnp.where(pos1 - ws == lane, 1.0, 0.0), 0.0)
                p2 = jnp.where(e2 == e, jnp.where(pos2 - ws == lane, 1.0, 0.0), 0.0)
                pp = jnp.concatenate([p1, p2], axis=0).astype(BF16)
                acc_scr[...] += jnp.dot(pp, win_scr[slot, k], preferred_element_type=F32)

    mix = g1 * acc_scr[0:rows, :] + g2 * acc_scr[rows:2 * rows, :]
    upd = gate_ref[...][None] * mix.reshape(rows // bsz, bsz, dm)
    xo = x_ref[...] + upd.reshape(rows, dm)
    _fill_lane_tiles(_rms(xo, gf_ref[...]), tr_scr)
    for cl in range(rows // bsz // grid_rows):
        _strided_rows_out(
            tr_scr, lambda bb, lt: o_ref.at[bb, :, cl * dm + lt * LANES:cl * dm + (lt + 1) * LANES],
            n_groups=bsz, group_rows=grid_rows, base=cl * grid_rows * bsz)


def _moe(xt, g, shift, scale, gate, router_w, router_b, w_in, w_out, g_final, *, bsz, grid_rows):
    m, d = xt.shape
    n_exp, f, _ = w_out.shape
    assert m % MOE_TB == 0 and MOE_TB % bsz == 0
    nb = m // MOE_TB
    cap = _moe_cap(m, nb)
    local_rows = -(-(TOP_K * MOE_TB + n_exp * (MOE_PAD - 1)) // 128) * 128
    full1 = lambda i: (0, 0)

    ut = (jnp.arange(MOE_TB)[:, None] < jnp.arange(MOE_TB)[None, :]).astype(BF16)
    route = functools.partial(_moe_route_kernel, bsz=bsz, n_exp=n_exp, n_blocks=nb, local_rows=local_rows)
    sorted_h, meta, tab = pl.pallas_call(
        route,
        grid=(nb,),
        in_specs=[pl.BlockSpec((MOE_TB, d), lambda i: (i, 0)),
                  pl.BlockSpec((1, d), full1),
                  pl.BlockSpec((bsz, d), full1),
                  pl.BlockSpec((bsz, d), full1),
                  pl.BlockSpec((n_exp, d), full1),
                  pl.BlockSpec((n_exp, 1), full1),
                  pl.BlockSpec((MOE_TB, MOE_TB), full1)],
        out_specs=[pl.BlockSpec(memory_space=pl.ANY),
                   pl.BlockSpec((MOE_TB, MOE_LANES), lambda i: (i, 0)),
                   pl.BlockSpec(memory_space=pltpu.SMEM)],
        out_shape=[jax.ShapeDtypeStruct((n_exp, cap, d), BF16),
                   jax.ShapeDtypeStruct((m, MOE_LANES), F32),
                   jax.ShapeDtypeStruct(((nb + 1) * 2 * n_exp,), jnp.int32)],
        scratch_shapes=[pltpu.VMEM((2, local_rows + MOE_TILE, d), BF16),
                        pltpu.VMEM((MOE_TILE, d), BF16),
                        pltpu.SMEM((n_exp,), jnp.int32),
                        pltpu.SemaphoreType.DMA((2, n_exp))],
        compiler_params=_params("arbitrary"),
        name="moe_route",
    )(xt, g.reshape(1, d), shift, scale, router_w.T, router_b.reshape(n_exp, 1), ut)

    tot = tab[nb * 2 * n_exp: nb * 2 * n_exp + n_exp]
    ntile = jnp.maximum((tot + MOE_TILE - 1) // MOE_TILE, 1)
    cum = jnp.cumsum(ntile)
    n_active = cum[-1:]
    max_tiles = (TOP_K * m + nb * n_exp * (MOE_PAD - 1)) // MOE_TILE + n_exp + 1
    idc = jnp.minimum(jnp.arange(max_tiles, dtype=jnp.int32), n_active - 1)
    tile_e = jnp.sum(idc[:, None] >= cum[None, :], axis=1).astype(jnp.int32)
    tile_r = (idc - (cum - ntile)[tile_e]).astype(jnp.int32)
    end = (ntile * MOE_TILE).astype(jnp.int32)

    tf = f // 2 if (f // 2) % 128 == 0 else f
    n_f = f // tf

    def f_idx(i, j, na):
        return jnp.where(i < na[0], j, n_f - 1)

    ys = pl.pallas_call(
        functools.partial(_moe_expert_kernel, n_f=n_f),
        grid_spec=pltpu.PrefetchScalarGridSpec(
            num_scalar_prefetch=3,
            grid=(max_tiles, n_f),
            in_specs=[pl.BlockSpec((1, MOE_TILE, d), lambda i, j, te, tr, na: (te[i], tr[i], 0)),
                      pl.BlockSpec((1, d, tf), lambda i, j, te, tr, na: (te[i], 0, f_idx(i, j, na))),
                      pl.BlockSpec((1, d, tf), lambda i, j, te, tr, na: (te[i], 0, n_f + f_idx(i, j, na))),
                      pl.BlockSpec((1, tf, d), lambda i, j, te, tr, na: (te[i], f_idx(i, j, na), 0))],
            out_specs=pl.BlockSpec((1, MOE_TILE, d), lambda i, j, te, tr, na: (te[i], tr[i], 0)),
            scratch_shapes=[pltpu.VMEM((MOE_TILE, d), F32)]),
        out_shape=jax.ShapeDtypeStruct((n_exp, cap, d), BF16),
        compiler_params=_params("arbitrary", "arbitrary"),
        name="moe_experts",
    )(tile_e, tile_r, n_active.astype(jnp.int32), sorted_h, w_in, w_in, w_out)

    n_win = MOE_TILE // MOE_WIN
    tsteps = MOE_TB // bsz
    assert tsteps % grid_rows == 0
    ncols = tsteps // grid_rows
    out = pl.pallas_call(
        functools.partial(_moe_combine_kernel, bsz=bsz, n_exp=n_exp, grid_rows=grid_rows),
        grid_spec=pltpu.PrefetchScalarGridSpec(
            num_scalar_prefetch=2,
            grid=(nb,),
            in_specs=[pl.BlockSpec((MOE_TB, d), lambda i, tb, en: (i, 0)),
                      pl.BlockSpec((MOE_TB, MOE_LANES), lambda i, tb, en: (i, 0)),
                      pl.BlockSpec((bsz, d), lambda i, tb, en: (0, 0)),
                      pl.BlockSpec((1, d), lambda i, tb, en: (0, 0)),
                      pl.BlockSpec(memory_space=pl.ANY)],
            out_specs=pl.BlockSpec((bsz, grid_rows, ncols * d), lambda i, tb, en: (0, 0, i)),
            scratch_shapes=[pltpu.VMEM((2, n_exp * n_win, MOE_WIN, d), BF16),
                            pltpu.VMEM((TOP_K * MOE_TB, d), F32),
                            pltpu.VMEM((d // LANES, MOE_TB, LANES), F32),
                            pltpu.SemaphoreType.DMA((2, n_exp * n_win))]),
        out_shape=jax.ShapeDtypeStruct((bsz, grid_rows, (m // bsz // grid_rows) * d), F32),
        compiler_params=_params("arbitrary"),
        name="moe_combine",
    )(tab, end, xt, meta, gate, g_final.reshape(1, d), ys)
    return out.reshape(bsz, m // bsz, d)


def _blockdiag(w, n_gb):
    two, g, r, c = w.shape
    eye = jnp.eye(S5_GB, dtype=w.dtype)
    out = jnp.einsum("dbgrc,gh->dbgrhc", w.reshape(two, n_gb, S5_GB, r, c), eye)
    return out.reshape(two, n_gb, S5_GB * r, S5_GB * c)


def kernel(x, c, ctx, c_ctx, ada_w, ada_b, norm_mix, norm_ffn, ssd_w_in, ssd_conv_w, ssd_conv_b, ssd_dt_bias, ssd_a_log, ssd_d, ssd_norm, ssd_w_out, s5_lam_re, s5_lam_im, s5_log_step, s5_b_re, s5_b_im, s5_c_re, s5_c_im, s5_d, s5_w_glu, s5_b_glu, ffn_w_in, ffn_w_out, moe_router_w, moe_router_b, moe_w_in, moe_w_out, norm_final):
    bsz, seq, d = x.shape
    ctx_len = ctx.shape[1]
    depth = ada_w.shape[0]
    assert depth == 2, "one SSD layer followed by one S5 layer"
    rows = seq // GRID_W

    pad_rows = (-(bsz + 1)) % 8
    cc = jnp.concatenate([c, c_ctx[None], jnp.zeros((pad_rows, d), F32)], axis=0)
    mods = _ada(cc, ada_w, ada_b)

    def mod(i, n):
        sl = slice(n * d, (n + 1) * d)
        return mods[i, :bsz, sl], mods[i, bsz:bsz + 1, sl]

    heads = ssd_d.shape[1]
    d_inner = heads * SSD_HEADDIM
    conv_dim = ssd_conv_w.shape[2]
    n_main = d_inner + conv_dim
    w_in0 = ssd_w_in[0]
    w_main = w_in0[:, :n_main].astype(BF16)
    w_dt = w_in0[:, n_main:]
    conv_w_full = jnp.concatenate([jnp.zeros((SSD_CONV, d_inner), F32), ssd_conv_w[0]], axis=1)
    conv_b_full = jnp.concatenate([jnp.zeros((d_inner,), F32), ssd_conv_b[0]])
    d_exp = jnp.repeat(ssd_d[0], SSD_HEADDIM)
    w_out0 = ssd_w_out[0].astype(BF16)
    gw = (heads // SSD_GROUPS) * SSD_HEADDIM

    (sh_x, sh_c), (sc_x, sc_c), (gt_x, gt_c) = mod(0, 0), mod(0, 1), mod(0, 2)
    r3 = lambda a: a[:, None, :]
    dt_bias = ssd_dt_bias[0].reshape(-1)
    a_log = ssd_a_log[0].reshape(-1)

    def ssd_mixer(tok3, shift, scale, gate, h0):
        b_, t_, _ = tok3.shape
        zx, dtc, dtcT = _ssd_inproj(tok3, norm_mix[0], r3(shift), r3(scale), w_main, conv_w_full,
                                    conv_b_full, w_dt, dt_bias, a_log, d_inner=d_inner, heads=heads)
        y, hfin = _ssd_scan(zx, dtc, dtcT, h0, d_inner=d_inner, heads=heads)
        out = _ssd_out(y.reshape(2, b_ * t_, d_inner), zx.reshape(b_ * t_, n_main), d_exp, ssd_norm[0],
                       w_out0, tok3.reshape(b_ * t_, d), r3(gate), seq=t_, d_inner=d_inner)
        return out, hfin

    h_zero = jnp.zeros((bsz, 2, SSD_GROUPS, SSD_STATE, gw), F32)
    ctx2, h_ctx = ssd_mixer(ctx, sh_c, sc_c, gt_c, h_zero)
    x2, _ = ssd_mixer(x, sh_x, sc_x, gt_x, h_ctx)

    ffn_in = ffn_w_in[0].astype(BF16)
    ffn_out = ffn_w_out[0].astype(BF16)
    (sh_x, sh_c), (sc_x, sc_c), (gt_x, gt_c) = mod(0, 3), mod(0, 4), mod(0, 5)
    xt = _ffn(x2, norm_ffn[0], r3(sh_x), r3(sc_x), r3(gt_x), ffn_in, ffn_out, seq=seq,
              time_major_batch=bsz).reshape(seq * bsz, d)
    ctx2 = _ffn(ctx2, norm_ffn[0], r3(sh_c), r3(sc_c), r3(gt_c), ffn_in, ffn_out, seq=ctx_len)

    ct = ctx2.reshape(bsz, ctx_len, d).transpose(1, 0, 2).reshape(ctx_len * bsz, d)

    ar, ai, bbr, bbi = _s5_discretize(s5_lam_re[0], s5_lam_im[0], s5_log_step[0], s5_b_re[0], s5_b_im[0])
    n_groups = ar.shape[1]
    n_gb = n_groups // S5_GB
    wb = jnp.concatenate([_blockdiag(bbr.transpose(0, 1, 3, 2), n_gb),
                          _blockdiag(bbi.transpose(0, 1, 3, 2), n_gb)], axis=-1).astype(BF16)
    wc = jnp.concatenate([_blockdiag(s5_c_re[0].transpose(0, 1, 3, 2), n_gb),
                          _blockdiag(-s5_c_im[0].transpose(0, 1, 3, 2), n_gb)], axis=2).astype(BF16)
    a_coef = jnp.stack([ar.reshape(2, -1), ai.reshape(2, -1)], axis=1)
    nstate = a_coef.shape[-1]

    (sh_x, sh_c), (sc_x, sc_c), (gt_x, _) = mod(1, 0), mod(1, 1), mod(1, 2)
    bc = lambda a: jnp.broadcast_to(a, (bsz, d))
    s_zero = jnp.zeros((2, 2, bsz, nstate), F32)
    _, s_ctx = _s5_scan(ct, norm_mix[1], bc(sh_c), bc(sc_c), wb, wc, a_coef, s_zero, bsz=bsz)
    y, _ = _s5_scan(xt, norm_mix[1], sh_x, sc_x, wb, wc, a_coef, s_ctx, bsz=bsz)
    xt = _s5_glu(xt, y, norm_mix[1], sh_x, sc_x, gt_x, s5_d[0], s5_w_glu[0].astype(BF16), s5_b_glu[0], bsz=bsz)

    (sh_x, _), (sc_x, _), (gt_x, _) = mod(1, 3), mod(1, 4), mod(1, 5)
    return _moe(xt, norm_ffn[1], sh_x, sc_x, gt_x, moe_router_w[0], moe_router_b[0],
                moe_w_in[0].astype(BF16), moe_w_out[0].astype(BF16), norm_final, bsz=bsz, grid_rows=rows)
```

```python
import functools

import jax
import jax.numpy as jnp
from jax import lax
from jax.experimental import pallas as pl
from jax.experimental.pallas import tpu as pltpu

F32 = jnp.float32
BF16 = jnp.bfloat16
HIGHEST = lax.Precision.HIGHEST

EPS = 1e-6
GRID_W = 64
N_MOD = 6

SSD_HEADDIM = 64
SSD_GROUPS = 4
SSD_STATE = 128
SSD_CONV = 5
SSD_CHUNK = 128

S5_CH = 16
S5_STATE = 64
S5_GB = 8

TOP_K = 2

V7X_VMEM_BYTES = 64 * 1024 * 1024
VMEM_LIMIT = V7X_VMEM_BYTES - 8 * 1024 * 1024
NEG_BIG = -1e30
LANES = 128
LOG2E = 1.4426950408889634


def _fill_lane_tiles(res, tr_scr):
    for lt in range(res.shape[1] // LANES):
        tr_scr[lt] = res[:, lt * LANES:(lt + 1) * LANES]


def _strided_rows_out(tr_scr, dst, *, n_groups, group_rows, stride, base=0):
    for g in range(n_groups):
        for lt in range(tr_scr.shape[0]):
            dst(g, lt)[...] = tr_scr[lt, pl.ds(base + g, group_rows, stride=stride), :]


def _params(*sem):
    return pltpu.CompilerParams(dimension_semantics=sem, vmem_limit_bytes=VMEM_LIMIT)


def _silu(v):
    return v * jax.nn.sigmoid(v)


def _softplus(v):
    return jnp.maximum(v, 0.0) + jnp.log1p(jnp.exp(-jnp.abs(v)))


def _rms(x, g):
    inv = lax.rsqrt(jnp.mean(x * x, axis=-1, keepdims=True) + EPS)
    return (x * inv) * g


def _nt_dot(a, b, precision=None):
    return lax.dot_general(a, b, (((1,), (1,)), ((), ())), precision=precision,
                           preferred_element_type=F32)


def _ada_kernel(c_ref, w_ref, b_ref, o_ref):
    s = _silu(c_ref[...])
    o_ref[0] = jnp.dot(s, w_ref[0], precision=HIGHEST, preferred_element_type=F32) + b_ref[0]


def _ada(cc, ada_w, ada_b):
    depth, d, n = ada_w.shape
    r = cc.shape[0]
    tn = 1536
    return pl.pallas_call(
        _ada_kernel,
        grid=(depth, n // tn),
        in_specs=[pl.BlockSpec((r, d), lambda i, j: (0, 0)),
                  pl.BlockSpec((1, d, tn), lambda i, j: (i, 0, j)),
                  pl.BlockSpec((1, 1, tn), lambda i, j: (i, 0, j))],
        out_specs=pl.BlockSpec((1, r, tn), lambda i, j: (i, 0, j)),
        out_shape=jax.ShapeDtypeStruct((depth, r, n), F32),
        compiler_params=_params("arbitrary", "arbitrary"),
        name="ada_mod",
    )(cc, ada_w, ada_b.reshape(depth, 1, n))


def _ssd_inproj_kernel(x_ref, g_ref, sh_ref, sc_ref, w_ref, cw_ref, cb_ref, wdt_ref, wdtT_ref,
                       bdt_ref, bdtT_ref, alog_ref, alogT_ref,
                       o_ref, dt_ref, dtT_ref, h_scr, pad_scr, *, n_plain, seq, heads, row_chunk):
    j = pl.program_id(1)
    pad = 8

    @pl.when(j == 0)
    def _():
        for r in range(0, seq, row_chunk):
            rs = slice(r, r + row_chunk)
            h = _rms(x_ref[0, rs, :], g_ref[...]) * (1.0 + sc_ref[0]) + sh_ref[0]
            hb = h.astype(BF16)
            h_scr[rs, :] = hb
            hl = (h - hb.astype(F32)).astype(BF16)
            nh2 = 2 * heads
            p = jnp.dot(hb, wdt_ref[...], preferred_element_type=F32)
            q = jnp.dot(hl, wdt_ref[...], preferred_element_type=F32)
            dt = _softplus(p[:, :nh2] + (p[:, nh2:] + q[:, :nh2]) + bdt_ref[...])
            da = dt * (-jnp.exp(alog_ref[...]))
            pT = _nt_dot(wdtT_ref[...], hb)
            qT = _nt_dot(wdtT_ref[...], hl)
            dtT = _softplus(pT[:nh2, :] + (pT[nh2:, :] + qT[:nh2, :]) + bdtT_ref[...])
            daT = dtT * (-jnp.exp(alogT_ref[...]))
            for d in range(2):
                sl = slice(d * heads, (d + 1) * heads)
                dt_ref[d, 0, rs, :] = jnp.concatenate([dt[:, sl], da[:, sl]], axis=-1)
                dtT_ref[d, 0, :, rs] = jnp.concatenate([dtT[sl, :], daT[sl, :]], axis=0)
        zeros = jnp.zeros((pad, pad_scr.shape[1]), F32)
        pad_scr[0:pad, :] = zeros
        pad_scr[pad + seq:2 * pad + seq, :] = zeros

    acc = jnp.dot(h_scr[...], w_ref[...], preferred_element_type=F32)

    @pl.when(j < n_plain)
    def _():
        o_ref[0] = acc.astype(o_ref.dtype)

    @pl.when(j >= n_plain)
    def _():
        pad_scr[pad:pad + seq, :] = acc
        half = (SSD_CONV - 1) // 2
        for r in range(0, seq, row_chunk):
            s = cb_ref[...]
            for k in range(SSD_CONV):
                lo = pad + r + k - half
                s = s + cw_ref[k:k + 1, :] * pad_scr[lo:lo + row_chunk, :]
            o_ref[0, r:r + row_chunk, :] = _silu(s).astype(o_ref.dtype)


def _ssd_inproj(x3, g, shift, scale, w_main, conv_w_full, conv_b_full, w_dt, dt_bias, a_log,
                *, d_inner, heads):
    bsz, seq, d = x3.shape
    n = w_main.shape[1]
    tn = 512
    nh2 = 2 * heads
    per_batch = shift.shape[0] > 1
    mod_map = (lambda b, j: (b, 0, 0)) if per_batch else (lambda b, j: (0, 0, 0))
    kern = functools.partial(_ssd_inproj_kernel, n_plain=d_inner // tn, seq=seq, heads=heads,
                             row_chunk=min(256, seq))
    w_hi = w_dt.astype(BF16)
    w_dt2 = jnp.concatenate([w_hi, (w_dt - w_hi.astype(F32)).astype(BF16)], axis=1)
    const2 = lambda b, j: (0, 0)
    return pl.pallas_call(
        kern,
        grid=(bsz, n // tn),
        in_specs=[pl.BlockSpec((1, seq, d), lambda b, j: (b, 0, 0), pipeline_mode=pl.Buffered(1)),
                  pl.BlockSpec((1, d), const2),
                  pl.BlockSpec((1, 1, d), mod_map),
                  pl.BlockSpec((1, 1, d), mod_map),
                  pl.BlockSpec((d, tn), lambda b, j: (0, j)),
                  pl.BlockSpec((SSD_CONV, tn), lambda b, j: (0, j)),
                  pl.BlockSpec((1, tn), lambda b, j: (0, j)),
                  pl.BlockSpec((d, 2 * nh2), const2),
                  pl.BlockSpec((2 * nh2, d), const2),
                  pl.BlockSpec((1, nh2), const2),
                  pl.BlockSpec((nh2, 1), const2),
                  pl.BlockSpec((1, nh2), const2),
                  pl.BlockSpec((nh2, 1), const2)],
        out_specs=[pl.BlockSpec((1, seq, tn), lambda b, j: (b, 0, j)),
                   pl.BlockSpec((2, 1, seq, nh2), lambda b, j: (0, b, 0, 0)),
                   pl.BlockSpec((2, 1, nh2, seq), lambda b, j: (0, b, 0, 0))],
        out_shape=[jax.ShapeDtypeStruct((bsz, seq, n), BF16),
                   jax.ShapeDtypeStruct((2, bsz, seq, nh2), F32),
                   jax.ShapeDtypeStruct((2, bsz, nh2, seq), F32)],
        scratch_shapes=[pltpu.VMEM((seq, d), BF16), pltpu.VMEM((seq + 16, tn), F32)],
        compiler_params=_params("arbitrary", "arbitrary"),
        name="ssd_inproj",
    )(x3, g.reshape(1, d), shift, scale, w_main, conv_w_full, conv_b_full.reshape(1, n),
      w_dt2, w_dt2.T, dt_bias.reshape(1, nh2), dt_bias.reshape(nh2, 1),
      a_log.reshape(1, nh2), a_log.reshape(nh2, 1))


def _ssd_scan_kernel(xs_ref, b_ref, c_ref, dtc_ref, dtcT_ref, h0_ref, y_ref, hfin_ref,
                     st_scr, *, heads, n_chunks):
    d = pl.program_id(1)
    k = pl.program_id(2)
    L = xs_ref.shape[1]
    hpg = heads // SSD_GROUPS
    pairs_per_group = hpg // 2
    gw = hpg * SSD_HEADDIM

    @pl.when(k == 0)
    def _():
        st_scr[...] = h0_ref[0, 0]

    row = lax.broadcasted_iota(jnp.int32, (L, L), 0)
    col = lax.broadcasted_iota(jnp.int32, (L, L), 1)
    mask = jnp.where(d == 0, row - col, col - row) >= 0
    mask_f = mask.astype(F32)

    dtc = dtc_ref[0, 0]
    dtcT = dtcT_ref[0, 0]
    assert L == 2 * SSD_HEADDIM, "the head-pair tiles assume chunk length == 2 * head dim"
    da = dtc[:, heads:]
    dtT, daT = dtcT[:heads, :], dtcT[heads:, :]
    cum = jnp.dot(mask_f, da, precision=HIGHEST, preferred_element_type=F32)
    cumT = _nt_dot(daT, mask_f, precision=HIGHEST)
    tot = jnp.sum(da, axis=0, keepdims=True)
    totT = jnp.sum(daT, axis=1, keepdims=True)
    etot = jnp.exp(tot)
    c2 = cum * LOG2E
    c2T = cumT * LOG2E
    wrow = jnp.exp(totT - cumT) * dtT

    lane = lax.broadcasted_iota(jnp.int32, (L, 2 * SSD_HEADDIM), 1)
    lo = lane < SSD_HEADDIM
    lo_row = lax.broadcasted_iota(jnp.int32, (1, 2 * SSD_HEADDIM), 1) < SSD_HEADDIM

    for g in range(SSD_GROUPS):
        bg = b_ref[0, :, g * SSD_STATE:(g + 1) * SSD_STATE]
        cg = c_ref[0, :, g * SSD_STATE:(g + 1) * SSD_STATE]
        scores = _nt_dot(cg, bg) * mask_f
        bgT = bg.astype(F32).T
        yoff = jnp.dot(cg, st_scr[g].astype(BF16), preferred_element_type=F32)
        for q in range(pairs_per_group):
            p = g * pairs_per_group + q
            h0, h1 = 2 * p, 2 * p + 1
            xs2 = xs_ref[0, :, p * 128:(p + 1) * 128]
            zero = jnp.zeros_like(xs2)
            rhs = jnp.concatenate([jnp.where(lo, xs2, zero), jnp.where(lo, zero, xs2)], axis=0)
            ms, bs, es = [], [], []
            for h in (h0, h1):
                col = jnp.broadcast_to(c2[:, h:h + 1], (L, L))
                dec = jnp.exp2(jnp.minimum(col - c2T[h:h + 1, :], 0.0))
                ms.append((scores * dec * dtT[h:h + 1, :]).astype(BF16))
                bs.append((bgT * wrow[h:h + 1, :]).astype(BF16))
                es.append(jnp.exp2(col))
            ydiag = jnp.dot(jnp.concatenate(ms, axis=1), rhs, preferred_element_type=F32)
            new = jnp.dot(jnp.concatenate(bs, axis=1), rhs, preferred_element_type=F32)
            sl = slice(q * 128, (q + 1) * 128)
            y2 = ydiag + yoff[:, sl] * jnp.where(lo, es[0], es[1])
            y_ref[0, 0, :, p * 128:(p + 1) * 128] = y2.astype(y_ref.dtype)
            et2 = jnp.where(lo_row, etot[:, h0:h0 + 1], etot[:, h1:h1 + 1])
            st_scr[g, :, sl] = st_scr[g, :, sl] * et2 + new

    @pl.when(k == n_chunks - 1)
    def _():
        hfin_ref[0, 0] = st_scr[...]


def _ssd_scan(zx, dtc, dtcT, h0, *, d_inner, heads):
    bsz, seq, _ = zx.shape
    L = min(SSD_CHUNK, seq)
    nc = seq // L
    bc_w = SSD_GROUPS * SSD_STATE
    gw = (heads // SSD_GROUPS) * SSD_HEADDIM
    nh2 = 2 * heads
    xs_blk = d_inner // d_inner
    b_blk = (2 * d_inner) // bc_w
    c_blk = b_blk + 1

    def chunk(d, k):
        return k + d * (nc - 1 - 2 * k)

    kern = functools.partial(_ssd_scan_kernel, heads=heads, n_chunks=nc)
    st_shape = (SSD_GROUPS, SSD_STATE, gw)
    return pl.pallas_call(
        kern,
        grid=(bsz, 2, nc),
        in_specs=[pl.BlockSpec((1, L, d_inner), lambda b, d, k: (b, chunk(d, k), xs_blk)),
                  pl.BlockSpec((1, L, bc_w), lambda b, d, k: (b, chunk(d, k), b_blk)),
                  pl.BlockSpec((1, L, bc_w), lambda b, d, k: (b, chunk(d, k), c_blk)),
                  pl.BlockSpec((1, 1, L, nh2), lambda b, d, k: (d, b, chunk(d, k), 0)),
                  pl.BlockSpec((1, 1, nh2, L), lambda b, d, k: (d, b, 0, chunk(d, k))),
                  pl.BlockSpec((1, 1) + st_shape, lambda b, d, k: (b, d, 0, 0, 0))],
        out_specs=[pl.BlockSpec((1, 1, L, d_inner), lambda b, d, k: (d, b, chunk(d, k), 0)),
                   pl.BlockSpec((1, 1) + st_shape, lambda b, d, k: (b, d, 0, 0, 0))],
        out_shape=[jax.ShapeDtypeStruct((2, bsz, seq, d_inner), BF16),
                   jax.ShapeDtypeStruct((bsz, 2) + st_shape, F32)],
        scratch_shapes=[pltpu.VMEM(st_shape, F32)],
        compiler_params=_params("arbitrary", "arbitrary", "arbitrary"),
        name="ssd_scan",
    )(zx, zx, zx, dtc, dtcT, h0)


def _ssd_out_kernel(y_ref, z_ref, xs_ref, dsk_ref, nw_ref, w_ref, x_ref, gate_ref, o_ref):
    y = y_ref[0].astype(F32) + y_ref[1].astype(F32) + dsk_ref[...] * xs_ref[...].astype(F32)
    yg = y * _silu(z_ref[...].astype(F32))
    yn = _rms(yg, nw_ref[...]).astype(BF16)
    out = jnp.dot(yn, w_ref[...], preferred_element_type=F32)
    o_ref[...] = x_ref[...] + gate_ref[0] * out


def _ssd_out(y, zx, d_exp, norm_w, w_out, x2, gate, *, seq, d_inner):
    m, d = x2.shape
    tm = min(512, seq)
    bpt = seq // tm
    per_batch = gate.shape[0] > 1
    gate_map = (lambda i: (i // bpt, 0, 0)) if per_batch else (lambda i: (0, 0, 0))
    return pl.pallas_call(
        _ssd_out_kernel,
        grid=(m // tm,),
        in_specs=[pl.BlockSpec((2, tm, d_inner), lambda i: (0, i, 0)),
                  pl.BlockSpec((tm, d_inner), lambda i: (i, 0)),
                  pl.BlockSpec((tm, d_inner), lambda i: (i, 1)),
                  pl.BlockSpec((1, d_inner), lambda i: (0, 0)),
                  pl.BlockSpec((1, d_inner), lambda i: (0, 0)),
                  pl.BlockSpec((d_inner, d), lambda i: (0, 0)),
                  pl.BlockSpec((tm, d), lambda i: (i, 0)),
                  pl.BlockSpec((1, 1, d), gate_map)],
        out_specs=pl.BlockSpec((tm, d), lambda i: (i, 0)),
        out_shape=jax.ShapeDtypeStruct((m, d), F32),
        compiler_params=_params("arbitrary"),
        name="ssd_out",
    )(y, zx, zx, d_exp.reshape(1, d_inner), norm_w.reshape(1, d_inner), w_out, x2, gate)


def _ffn_kernel(x_ref, g_ref, sh_ref, sc_ref, gate_ref, wg_ref, wu_ref, wo_ref, o_ref,
                h_scr, acc_scr, tr_scr, *, n_f, time_major):
    j = pl.program_id(1)
    bb, tt, dm = x_ref.shape

    @pl.when(j == 0)
    def _():
        h = _rms(x_ref[...], g_ref[...]) * (1.0 + sc_ref[...]) + sh_ref[...]
        h_scr[...] = h.reshape(bb * tt, dm).astype(BF16)
        acc_scr[...] = jnp.zeros_like(acc_scr)

    h = h_scr[...]
    gv = jnp.dot(h, wg_ref[...], preferred_element_type=F32)
    uv = jnp.dot(h, wu_ref[...], preferred_element_type=F32)
    a = (_silu(gv) * uv).astype(BF16)
    acc_scr[...] += jnp.dot(a, wo_ref[...], preferred_element_type=F32)

    @pl.when(j == n_f - 1)
    def _():
        res = x_ref[...] + gate_ref[...] * acc_scr[...].reshape(bb, tt, dm)
        if time_major:
            _fill_lane_tiles(res.reshape(bb * tt, dm), tr_scr)
            _strided_rows_out(tr_scr, lambda c, lt: o_ref.at[c, 0, :, lt * LANES:(lt + 1) * LANES],
                              n_groups=tt, group_rows=bb, stride=tt)
        else:
            o_ref[...] = res


def _ffn(x3, g, shift, scale, gate, w_in, w_out, *, time_major=False):
    bsz, seq, d = x3.shape
    f = w_out.shape[0]
    tf = f // 2 if (f // 2) % 128 == 0 else f
    n_f = f // tf
    per_batch = shift.shape[0] > 1
    if time_major:
        bb, tt = 8, GRID_W
        assert bsz % bb == 0 and seq % tt == 0 and per_batch
        rows = seq // tt
        x_map = lambda i, j: (i // rows, i % rows, 0)
        mod_map = lambda i, j: (i // rows, 0, 0)
        out_spec = pl.BlockSpec((tt, 1, bb, d), lambda i, j: (0, i % rows, i // rows, 0))
        out_shape = jax.ShapeDtypeStruct((tt, rows, bsz, d), F32)
        n_blocks = (bsz // bb) * rows
    else:
        bb, tt = 1, min(512, seq)
        bpt = seq // tt
        x_map = lambda i, j: (i // bpt, i % bpt, 0)
        mod_map = (lambda i, j: (i // bpt, 0, 0)) if per_batch else (lambda i, j: (0, 0, 0))
        out_spec = pl.BlockSpec((bb, tt, d), x_map)
        out_shape = jax.ShapeDtypeStruct((bsz, seq, d), F32)
        n_blocks = bsz * bpt
    mb = bb if per_batch else 1
    tm = bb * tt
    kern = functools.partial(_ffn_kernel, n_f=n_f, time_major=time_major)
    return pl.pallas_call(
        kern,
        grid=(n_blocks, n_f),
        in_specs=[pl.BlockSpec((bb, tt, d), x_map),
                  pl.BlockSpec((1, d), lambda i, j: (0, 0)),
                  pl.BlockSpec((mb, 1, d), mod_map),
                  pl.BlockSpec((mb, 1, d), mod_map),
                  pl.BlockSpec((mb, 1, d), mod_map),
                  pl.BlockSpec((d, tf), lambda i, j: (0, j)),
                  pl.BlockSpec((d, tf), lambda i, j: (0, n_f + j)),
                  pl.BlockSpec((tf, d), lambda i, j: (j, 0))],
        out_specs=out_spec,
        out_shape=out_shape,
        scratch_shapes=[pltpu.VMEM((tm, d), BF16), pltpu.VMEM((tm, d), F32),
                        pltpu.VMEM((d // LANES, tm if time_major else 8, LANES), F32)],
        compiler_params=_params("arbitrary", "arbitrary"),
        name="dense_swiglu",
    )(x3, g.reshape(1, d), shift, scale, gate, w_in, w_in, w_out)


def _s5_disc_kernel(lre_ref, lim_ref, step_ref, bre_ref, bim_ref, ar_ref, ai_ref, bbr_ref, bbi_ref):
    step = jnp.exp(step_ref[...])
    lre, lim = lre_ref[...], lim_ref[...]
    mag = jnp.exp(lre * step)
    ar = mag * jnp.cos(lim * step)
    ai = mag * jnp.sin(lim * step)
    den = lre * lre + lim * lim
    cr = ((ar - 1.0) * lre + ai * lim) / den
    ci = (ai * lre - (ar - 1.0) * lim) / den
    ar_ref[...] = ar
    ai_ref[...] = ai
    bbr_ref[...] = cr * bre_ref[...] - ci * bim_ref[...]
    bbi_ref[...] = cr * bim_ref[...] + ci * bre_ref[...]


def _s5_discretize(lam_re, lam_im, log_step, b_re, b_im):
    two, g, p = lam_re.shape
    c = b_re.shape[-1]
    lre = jnp.repeat(lam_re, c, axis=-1)
    lim = jnp.repeat(lam_im, c, axis=-1)
    shp = jax.ShapeDtypeStruct((two, g, p * c), F32)
    ar, ai, bbr, bbi = pl.pallas_call(
        _s5_disc_kernel, out_shape=[shp, shp, shp, shp], name="s5_discretize",
    )(lre, lim, log_step.reshape(two, g, 1), b_re.reshape(two, g, p * c), b_im.reshape(two, g, p * c))
    ar = ar.reshape(two, g, p, c)[..., 0]
    ai = ai.reshape(two, g, p, c)[..., 0]
    return ar, ai, bbr.reshape(two, g, p, c), bbi.reshape(two, g, p, c)


def _s5_kernel(x_ref, g_ref, sh_ref, sc_ref, wb_ref, wc_ref, a_ref, s0_ref, y_ref, sfin_ref,
               st_scr, bu0_scr, bu1_scr, sb0_scr, sb1_scr, *, n_chunks, steps, bsz):
    bu_bufs = (bu0_scr, bu1_scr)
    sb_bufs = (sb0_scr, sb1_scr)
    d = pl.program_id(0)
    k = pl.program_id(1)
    rows, dm = x_ref.shape
    gl = S5_GB * S5_STATE
    cb = S5_GB * S5_CH

    @pl.when(k == 0)
    def _():
        st_scr[...] = s0_ref[0]

    hn = _rms(x_ref[...], g_ref[...]).reshape(steps, bsz, dm)
    h = hn * (1.0 + sc_ref[...])[None] + sh_ref[...][None]
    u = h.reshape(rows, dm).astype(BF16)

    n_gb = dm // cb

    def bu_dot(gb):
        return jnp.dot(u[:, gb * cb:(gb + 1) * cb], wb_ref[0, gb], preferred_element_type=F32)

    bu_bufs[0][...] = bu_dot(0)
    for gb in range(n_gb):
        bu_cur, bu_nxt = bu_bufs[gb % 2], bu_bufs[1 - gb % 2]
        sb_cur = sb_bufs[gb % 2]
        if gb + 1 < n_gb:
            bu_nxt[...] = bu_dot(gb + 1)
        sl = slice(gb * gl, (gb + 1) * gl)
        ar = jnp.broadcast_to(a_ref[0, 0:1, sl], (bsz, gl))
        ai = jnp.broadcast_to(a_ref[0, 1:2, sl], (bsz, gl))
        sr, si = st_scr[0, :, sl], st_scr[1, :, sl]
        for i in range(steps):
            t = jnp.where(d == 0, i, steps - 1 - i)
            r0 = pl.multiple_of(t * bsz, bsz)
            bur = bu_cur[pl.ds(r0, bsz), 0:gl]
            bui = bu_cur[pl.ds(r0, bsz), gl:2 * gl]
            sr, si = ar * sr - ai * si + bur, ar * si + ai * sr + bui
            sb_cur[pl.ds(r0, bsz), 0:gl] = sr.astype(BF16)
            sb_cur[pl.ds(r0, bsz), gl:2 * gl] = si.astype(BF16)
        st_scr[0, :, sl] = sr
        st_scr[1, :, sl] = si
        yv = jnp.dot(sb_cur[...], wc_ref[0, gb], preferred_element_type=F32)
        y_ref[0, :, gb * cb:(gb + 1) * cb] = yv.astype(y_ref.dtype)

    @pl.when(k == n_chunks - 1)
    def _():
        sfin_ref[0] = st_scr[...]


def _s5_scan(xt, g, shift, scale, wb, wc, a, s0, *, bsz):
    m, d = xt.shape
    seq = m // bsz
    steps = min(64, seq)
    rows = steps * bsz
    nc = seq // steps
    nstate = s0.shape[-1]
    n_gb = d // (S5_GB * S5_CH)
    gl = S5_GB * S5_STATE

    def chunk(dd, k):
        return k + dd * (nc - 1 - 2 * k)

    kern = functools.partial(_s5_kernel, n_chunks=nc, steps=steps, bsz=bsz)
    return pl.pallas_call(
        kern,
        grid=(2, nc),
        in_specs=[pl.BlockSpec((rows, d), lambda dd, k: (chunk(dd, k), 0)),
                  pl.BlockSpec((1, d), lambda dd, k: (0, 0)),
                  pl.BlockSpec((bsz, d), lambda dd, k: (0, 0)),
                  pl.BlockSpec((bsz, d), lambda dd, k: (0, 0)),
                  pl.BlockSpec((1, n_gb, S5_GB * S5_CH, 2 * gl), lambda dd, k: (dd, 0, 0, 0)),
                  pl.BlockSpec((1, n_gb, 2 * gl, S5_GB * S5_CH), lambda dd, k: (dd, 0, 0, 0)),
                  pl.BlockSpec((1, 2, nstate), lambda dd, k: (dd, 0, 0)),
                  pl.BlockSpec((1, 2, bsz, nstate), lambda dd, k: (dd, 0, 0, 0))],
        out_specs=[pl.BlockSpec((1, rows, d), lambda dd, k: (dd, chunk(dd, k), 0)),
                   pl.BlockSpec((1, 2, bsz, nstate), lambda dd, k: (dd, 0, 0, 0))],
        out_shape=[jax.ShapeDtypeStruct((2, m, d), BF16),
                   jax.ShapeDtypeStruct((2, 2, bsz, nstate), F32)],
        scratch_shapes=[pltpu.VMEM((2, bsz, nstate), F32),
                        pltpu.VMEM((rows, 2 * gl), F32), pltpu.VMEM((rows, 2 * gl), F32),
                        pltpu.VMEM((rows, 2 * gl), BF16), pltpu.VMEM((rows, 2 * gl), BF16)],
        compiler_params=_params("arbitrary", "arbitrary"),
        name="s5_scan",
    )(xt, g.reshape(1, d), shift, scale, wb, wc, a, s0)


def _gelu_tanh(v):
    return 0.5 * v * (1.0 + jnp.tanh(0.7978845608028654 * (v + 0.044715 * (v * v * v))))


def _s5_glu_kernel(x_ref, y_ref, g_ref, sh_ref, sc_ref, gate_ref, dsk_ref, w_ref, b_ref, o_ref, *, bsz):
    rows, dm = x_ref.shape
    x = x_ref[...]
    hn = _rms(x, g_ref[...]).reshape(rows // bsz, bsz, dm)
    h = (hn * (1.0 + sc_ref[...])[None] + sh_ref[...][None]).reshape(rows, dm)
    yv = dsk_ref[...] * h + y_ref[0].astype(F32) + y_ref[1].astype(F32)
    ge = _gelu_tanh(yv).astype(BF16)
    o = jnp.dot(ge, w_ref[...], preferred_element_type=F32) + b_ref[...]
    out = o[:, :dm] * jax.nn.sigmoid(o[:, dm:])
    res = x.reshape(rows // bsz, bsz, dm) + gate_ref[...][None] * out.reshape(rows // bsz, bsz, dm)
    o_ref[...] = res.reshape(rows, dm)


def _s5_glu(xt, y, g, shift, scale, gate, d_skip, w_glu, b_glu, *, bsz):
    m, d = xt.shape
    tm = min(512, m)
    kern = functools.partial(_s5_glu_kernel, bsz=bsz)
    full = lambda i: (0, 0)
    return pl.pallas_call(
        kern,
        grid=(m // tm,),
        in_specs=[pl.BlockSpec((tm, d), lambda i: (i, 0)),
                  pl.BlockSpec((2, tm, d), lambda i: (0, i, 0)),
                  pl.BlockSpec((1, d), full),
                  pl.BlockSpec((bsz, d), full),
                  pl.BlockSpec((bsz, d), full),
                  pl.BlockSpec((bsz, d), full),
                  pl.BlockSpec((1, d), full),
                  pl.BlockSpec((d, 2 * d), full),
                  pl.BlockSpec((1, 2 * d), full)],
        out_specs=pl.BlockSpec((tm, d), lambda i: (i, 0)),
        out_shape=jax.ShapeDtypeStruct((m, d), F32),
        compiler_params=_params("arbitrary"),
        name="s5_glu",
    )(xt, y, g.reshape(1, d), shift, scale, gate, d_skip.reshape(1, d), w_glu, b_glu.reshape(1, 2 * d))


MOE_TB = 512
MOE_TILE = 512
MOE_WIN = 256
MOE_PAD = 16
MOE_LANES = 128


def _moe_cap(n_tok, n_blocks):
    rows = n_tok + n_blocks * MOE_PAD + MOE_TILE
    return -(-rows // MOE_TILE) * MOE_TILE


def _moe_route_kernel(x_ref, g_ref, sh_ref, sc_ref, rwT_ref, rbT_ref, ut_ref,
                      sorted_ref, meta_ref, tab_ref,
                      buf_scr, zero_scr, run_smem, sems, *, bsz, n_exp, n_blocks, local_rows):
    b = pl.program_id(0)
    slot = lax.rem(b, 2)
    dm = x_ref.shape[-1]
    rows = x_ref.shape[0] * x_ref.shape[1] * x_ref.shape[2]

    @pl.when(b == 0)
    def _():
        for e in range(n_exp):
            run_smem[e] = 0
        zero_scr[...] = jnp.zeros_like(zero_scr)

    hn = _rms(x_ref[...].reshape(rows // bsz, bsz, dm), g_ref[...])
    h = (hn * (1.0 + sc_ref[...])[None] + sh_ref[...][None]).reshape(rows, dm)
    hb = h.astype(BF16)

    logits = _nt_dot(rwT_ref[...], h, precision=HIGHEST) + rbT_ref[...]
    sub = lax.broadcasted_iota(jnp.int32, logits.shape, 0)
    m1 = jnp.max(logits, axis=0, keepdims=True)
    i1 = jnp.min(jnp.where(logits == m1, sub, n_exp), axis=0, keepdims=True)
    rest = jnp.where(sub == i1, -jnp.inf, logits)
    m2 = jnp.max(rest, axis=0, keepdims=True)
    i2 = jnp.min(jnp.where(rest == m2, sub, n_exp), axis=0, keepdims=True)
    e2 = jnp.exp(m2 - m1)
    g1 = 1.0 / (1.0 + e2)
    g2 = e2 * g1
    sel1 = sub == i1
    sel2 = sub == i2
    oh = jnp.where(sel1, 1.0, 0.0) + jnp.where(sel2, 1.0, 0.0)
    cnt = jnp.sum(oh, axis=1, keepdims=True).astype(jnp.int32)
    cpad_v = ((cnt + (MOE_PAD - 1)) // MOE_PAD) * MOE_PAD
    rank = jnp.dot(oh.astype(BF16), ut_ref[...], preferred_element_type=F32)

    cpad, loc, run = [], [], []
    off = 0
    for e in range(n_exp):
        cpad.append(cpad_v[e, 0])
        loc.append(off)
        off = off + cpad[e]
        run.append(run_smem[e])
    sub1 = lax.broadcasted_iota(jnp.int32, (n_exp, 1), 0)
    loc_v = jnp.zeros((n_exp, 1), jnp.int32)
    run_v = jnp.zeros((n_exp, 1), jnp.int32)
    for e in range(n_exp):
        loc_v = jnp.where(sub1 == e, loc[e], loc_v)
        run_v = jnp.where(sub1 == e, run[e], run_v)
    loc_v = loc_v.astype(F32)
    run_v = run_v.astype(F32)

    def pick(sel, v):
        return jnp.sum(jnp.where(sel, v, 0.0), axis=0, keepdims=True)

    lp1 = pick(sel1, loc_v + rank).astype(jnp.int32)
    lp2 = pick(sel2, loc_v + rank).astype(jnp.int32)
    pos1 = pick(sel1, run_v + rank)
    pos2 = pick(sel2, run_v + rank)

    r = lax.broadcasted_iota(jnp.int32, (local_rows, rows), 0)
    onehot = jnp.where(r == lp1, 1.0, jnp.where(r == lp2, 1.0, 0.0)).astype(BF16)
    buf_scr[slot, 0:local_rows, :] = jnp.dot(onehot, hb, preferred_element_type=F32).astype(BF16)
    buf_scr[slot, local_rows:local_rows + MOE_TILE, :] = jnp.zeros((MOE_TILE, dm), BF16)

    rec = jnp.concatenate([i1.astype(F32), i2.astype(F32), pos1, pos2, g1, g2,
                           jnp.zeros((MOE_LANES - 6, rows), F32)], axis=0)
    meta_ref[...] = rec.T

    base = b * (2 * n_exp)
    for e in range(n_exp):
        tab_ref[base + e] = run[e]
        tab_ref[base + n_exp + e] = cpad[e]
        run_smem[e] = run[e] + cpad[e]

    def group_copy(e, slot_, loc_e, run_e):
        src = buf_scr.at[slot_, pl.ds(pl.multiple_of(loc_e, MOE_PAD), MOE_TILE)]
        dst = sorted_ref.at[e, pl.ds(pl.multiple_of(run_e, MOE_PAD), MOE_TILE)]
        return pltpu.make_async_copy(src, dst, sems.at[slot_, e])

    @pl.when(b > 0)
    def _():
        for e in range(n_exp):
            group_copy(e, 1 - slot, 0, 0).wait()

    for e in range(n_exp):
        group_copy(e, slot, loc[e], run[e]).start()

    @pl.when(b == n_blocks - 1)
    def _():
        fin = (n_blocks) * (2 * n_exp)
        for e in range(n_exp):
            group_copy(e, slot, 0, 0).wait()
        for e in range(n_exp):
            tot = run[e] + cpad[e]
            tab_ref[fin + e] = tot
            tab_ref[fin + n_exp + e] = 0
            dst = sorted_ref.at[e, pl.ds(pl.multiple_of(tot, MOE_PAD), MOE_TILE)]
            pltpu.make_async_copy(zero_scr, dst, sems.at[slot, e]).start()
        for e in range(n_exp):
            dst = sorted_ref.at[e, pl.ds(0, MOE_TILE)]
            pltpu.make_async_copy(zero_scr, dst, sems.at[slot, e]).wait()


def _moe_expert_kernel(te_ref, tr_ref, na_ref, xs_ref, wg_ref, wu_ref, wo_ref, y_ref, acc_scr, *, n_f):
    i = pl.program_id(0)
    f = pl.program_id(1)

    @pl.when(i < na_ref[0])
    def _():
        @pl.when(f == 0)
        def _():
            acc_scr[...] = jnp.zeros_like(acc_scr)

        xs = xs_ref[0]
        gv = jnp.dot(xs, wg_ref[0], preferred_element_type=F32)
        uv = jnp.dot(xs, wu_ref[0], preferred_element_type=F32)
        a = (_silu(gv) * uv).astype(BF16)
        acc_scr[...] += jnp.dot(a, wo_ref[0], preferred_element_type=F32)

        @pl.when(f == n_f - 1)
        def _():
            y_ref[0] = acc_scr[...].astype(y_ref.dtype)


def _moe_combine_kernel(tab_ref, end_ref, x_ref, meta_ref, gate_ref, gf_ref, ys_ref, o_ref,
                        win_scr, acc_scr, tr_scr, sems, *, bsz, n_exp):
    b = pl.program_id(0)
    slot = lax.rem(b, 2)
    n_cols, n_rows, _, dm = x_ref.shape
    rows = n_cols * n_rows * bsz
    n_win = MOE_TILE // MOE_WIN

    def window(blk, slot_, e, c):
        base = blk * (2 * n_exp)
        run_e = tab_ref[base + e]
        live = tab_ref[base + n_exp + e] > c * MOE_WIN
        ws = jnp.minimum(run_e + c * MOE_WIN, end_ref[e] - MOE_WIN)
        k = e * n_win + c
        cp = pltpu.make_async_copy(ys_ref.at[e, pl.ds(pl.multiple_of(ws, MOE_PAD), MOE_WIN)],
                                   win_scr.at[slot_, k], sems.at[slot_, k])
        return live, ws, k, cp

    def fetch(blk, slot_):
        for e in range(n_exp):
            for c in range(n_win):
                live, _, _, cp = window(blk, slot_, e, c)

                @pl.when(live)
                def _():
                    cp.start()

    @pl.when(b == 0)
    def _():
        fetch(0, 0)

    fetch(b + 1, 1 - slot)

    meta = meta_ref[...]
    e1, e2 = meta[:, 0:1], meta[:, 1:2]
    pos1, pos2 = meta[:, 2:3].astype(jnp.int32), meta[:, 3:4].astype(jnp.int32)
    g1, g2 = meta[:, 4:5], meta[:, 5:6]
    acc_scr[...] = jnp.zeros_like(acc_scr)
    lane = lax.broadcasted_iota(jnp.int32, (rows, MOE_WIN), 1)

    for e in range(n_exp):
        for c in range(n_win):
            live, ws, k, cp = window(b, slot, e, c)

            @pl.when(live)
            def _():
                cp.wait()
                p1 = jnp.where(e1 == e, jnp.where(pos1 - ws == lane, 1.0, 0.0), 0.0)
                p2 = jnp.where(e2 == e, jnp.where(pos2 - ws == lane, 1.0, 0.0), 0.0)
                pp = jnp.concatenate([p1, p2], axis=0).astype(BF16)
                acc_scr[...] += jnp.dot(pp, win_scr[slot, k], preferred_element_type=F32)

    mix = g1 * acc_scr[0:rows, :] + g2 * acc_scr[rows:2 * rows, :]
    upd = gate_ref[...][None] * mix.reshape(rows // bsz, bsz, dm)
    xo = x_ref[...].reshape(rows, dm) + upd.reshape(rows, dm)
    _fill_lane_tiles(_rms(xo, gf_ref[...]), tr_scr)
    for rl in range(n_rows):
        _strided_rows_out(
            tr_scr, lambda bb, lt: o_ref.at[bb, rl, :, lt * LANES:(lt + 1) * LANES],
            n_groups=bsz, group_rows=n_cols, stride=n_rows * bsz, base=rl * bsz)


def _moe(xt, g, shift, scale, gate, router_w, router_b, w_in, w_out, g_final, *, bsz, grid_rows):
    m, d = xt.shape
    n_exp, f, _ = w_out.shape
    nb = m // MOE_TB
    cap = _moe_cap(m, nb)
    local_rows = -(-(TOP_K * MOE_TB + n_exp * (MOE_PAD - 1)) // 128) * 128
    full1 = lambda i: (0, 0)

    blk_cols = 8
    blk_rows = MOE_TB // (blk_cols * bsz)
    assert blk_rows * blk_cols * bsz == MOE_TB and grid_rows % blk_rows == 0 and GRID_W % blk_cols == 0
    nrg = grid_rows // blk_rows
    x4 = xt.reshape(GRID_W, grid_rows, bsz, d)
    x_spec1 = pl.BlockSpec((blk_cols, blk_rows, bsz, d), lambda i: (i // nrg, i % nrg, 0, 0))

    ut = (jnp.arange(MOE_TB)[:, None] < jnp.arange(MOE_TB)[None, :]).astype(BF16)
    route = functools.partial(_moe_route_kernel, bsz=bsz, n_exp=n_exp, n_blocks=nb, local_rows=local_rows)
    sorted_h, meta, tab = pl.pallas_call(
        route,
        grid=(nb,),
        in_specs=[x_spec1,
                  pl.BlockSpec((1, d), full1),
                  pl.BlockSpec((bsz, d), full1),
                  pl.BlockSpec((bsz, d), full1),
                  pl.BlockSpec((n_exp, d), full1),
                  pl.BlockSpec((n_exp, 1), full1),
                  pl.BlockSpec((MOE_TB, MOE_TB), full1)],
        out_specs=[pl.BlockSpec(memory_space=pl.ANY),
                   pl.BlockSpec((MOE_TB, MOE_LANES), lambda i: (i, 0)),
                   pl.BlockSpec(memory_space=pltpu.SMEM)],
        out_shape=[jax.ShapeDtypeStruct((n_exp, cap, d), BF16),
                   jax.ShapeDtypeStruct((m, MOE_LANES), F32),
                   jax.ShapeDtypeStruct(((nb + 1) * 2 * n_exp,), jnp.int32)],
        scratch_shapes=[pltpu.VMEM((2, local_rows + MOE_TILE, d), BF16),
                        pltpu.VMEM((MOE_TILE, d), BF16),
                        pltpu.SMEM((n_exp,), jnp.int32),
                        pltpu.SemaphoreType.DMA((2, n_exp))],
        compiler_params=_params("arbitrary"),
        name="moe_route",
    )(x4, g.reshape(1, d), shift, scale, router_w.T, router_b.reshape(n_exp, 1), ut)

    tot = tab[nb * 2 * n_exp: nb * 2 * n_exp + n_exp]
    ntile = jnp.maximum((tot + MOE_TILE - 1) // MOE_TILE, 1)
    cum = jnp.cumsum(ntile)
    n_active = cum[-1:]
    max_tiles = (TOP_K * m + nb * n_exp * (MOE_PAD - 1)) // MOE_TILE + n_exp + 1
    idc = jnp.minimum(jnp.arange(max_tiles, dtype=jnp.int32), n_active - 1)
    tile_e = jnp.sum(idc[:, None] >= cum[None, :], axis=1).astype(jnp.int32)
    tile_r = (idc - (cum - ntile)[tile_e]).astype(jnp.int32)
    end = (ntile * MOE_TILE).astype(jnp.int32)

    tf = f // 2 if (f // 2) % 128 == 0 else f
    n_f = f // tf

    def f_idx(i, j, na):
        return jnp.where(i < na[0], j, n_f - 1)

    ys = pl.pallas_call(
        functools.partial(_moe_expert_kernel, n_f=n_f),
        grid_spec=pltpu.PrefetchScalarGridSpec(
            num_scalar_prefetch=3,
            grid=(max_tiles, n_f),
            in_specs=[pl.BlockSpec((1, MOE_TILE, d), lambda i, j, te, tr, na: (te[i], tr[i], 0)),
                      pl.BlockSpec((1, d, tf), lambda i, j, te, tr, na: (te[i], 0, f_idx(i, j, na))),
                      pl.BlockSpec((1, d, tf), lambda i, j, te, tr, na: (te[i], 0, n_f + f_idx(i, j, na))),
                      pl.BlockSpec((1, tf, d), lambda i, j, te, tr, na: (te[i], f_idx(i, j, na), 0))],
            out_specs=pl.BlockSpec((1, MOE_TILE, d), lambda i, j, te, tr, na: (te[i], tr[i], 0)),
            scratch_shapes=[pltpu.VMEM((MOE_TILE, d), F32)]),
        out_shape=jax.ShapeDtypeStruct((n_exp, cap, d), BF16),
        compiler_params=_params("arbitrary", "arbitrary"),
        name="moe_experts",
    )(tile_e, tile_r, n_active.astype(jnp.int32), sorted_h, w_in, w_in, w_out)

    n_win = MOE_TILE // MOE_WIN
    out = pl.pallas_call(
        functools.partial(_moe_combine_kernel, bsz=bsz, n_exp=n_exp),
        grid_spec=pltpu.PrefetchScalarGridSpec(
            num_scalar_prefetch=2,
            grid=(nb,),
            in_specs=[pl.BlockSpec((blk_cols, blk_rows, bsz, d), lambda i, tb, en: (i // nrg, i % nrg, 0, 0)),
                      pl.BlockSpec((MOE_TB, MOE_LANES), lambda i, tb, en: (i, 0)),
                      pl.BlockSpec((bsz, d), lambda i, tb, en: (0, 0)),
                      pl.BlockSpec((1, d), lambda i, tb, en: (0, 0)),
                      pl.BlockSpec(memory_space=pl.ANY)],
            out_specs=pl.BlockSpec((bsz, blk_rows, blk_cols, d), lambda i, tb, en: (0, i % nrg, i // nrg, 0)),
            scratch_shapes=[pltpu.VMEM((2, n_exp * n_win, MOE_WIN, d), BF16),
                            pltpu.VMEM((TOP_K * MOE_TB, d), F32),
                            pltpu.VMEM((d // LANES, MOE_TB, LANES), F32),
                            pltpu.SemaphoreType.DMA((2, n_exp * n_win))]),
        out_shape=jax.ShapeDtypeStruct((bsz, grid_rows, GRID_W, d), F32),
        compiler_params=_params("arbitrary"),
        name="moe_combine",
    )(tab, end, x4, meta, gate, g_final.reshape(1, d), ys)
    return out.reshape(bsz, grid_rows * GRID_W, d)


def _blockdiag(w, n_gb):
    two, g, r, c = w.shape
    eye = jnp.eye(S5_GB, dtype=w.dtype)
    out = jnp.einsum("dbgrc,gh->dbgrhc", w.reshape(two, n_gb, S5_GB, r, c), eye)
    return out.reshape(two, n_gb, S5_GB * r, S5_GB * c)


def kernel(x, c, ctx, c_ctx, ada_w, ada_b, norm_mix, norm_ffn, ssd_w_in, ssd_conv_w, ssd_conv_b, ssd_dt_bias, ssd_a_log, ssd_d, ssd_norm, ssd_w_out, s5_lam_re, s5_lam_im, s5_log_step, s5_b_re, s5_b_im, s5_c_re, s5_c_im, s5_d, s5_w_glu, s5_b_glu, ffn_w_in, ffn_w_out, moe_router_w, moe_router_b, moe_w_in, moe_w_out, norm_final):
    bsz, seq, d = x.shape
    ctx_len = ctx.shape[1]
    depth = ada_w.shape[0]
    assert depth == 2, "one SSD layer followed by one S5 layer"
    rows = seq // GRID_W

    pad_rows = (-(bsz + 1)) % 8
    cc = jnp.concatenate([c, c_ctx[None], jnp.zeros((pad_rows, d), F32)], axis=0)
    mods = _ada(cc, ada_w, ada_b)

    def mod(i, n):
        sl = slice(n * d, (n + 1) * d)
        return mods[i, :bsz, sl], mods[i, bsz:bsz + 1, sl]

    heads = ssd_d.shape[1]
    d_inner = heads * SSD_HEADDIM
    conv_dim = ssd_conv_w.shape[2]
    n_main = d_inner + conv_dim
    w_in0 = ssd_w_in[0]
    w_main = w_in0[:, :n_main].astype(BF16)
    w_dt = w_in0[:, n_main:]
    conv_w_full = jnp.concatenate([jnp.zeros((SSD_CONV, d_inner), F32), ssd_conv_w[0]], axis=1)
    conv_b_full = jnp.concatenate([jnp.zeros((d_inner,), F32), ssd_conv_b[0]])
    d_exp = jnp.repeat(ssd_d[0], SSD_HEADDIM)
    w_out0 = ssd_w_out[0].astype(BF16)
    gw = (heads // SSD_GROUPS) * SSD_HEADDIM

    (sh_x, sh_c), (sc_x, sc_c), (gt_x, gt_c) = mod(0, 0), mod(0, 1), mod(0, 2)
    r3 = lambda a: a[:, None, :]
    dt_bias = ssd_dt_bias[0].reshape(-1)
    a_log = ssd_a_log[0].reshape(-1)

    def ssd_mixer(tok3, shift, scale, gate, h0):
        b_, t_, _ = tok3.shape
        zx, dtc, dtcT = _ssd_inproj(tok3, norm_mix[0], r3(shift), r3(scale), w_main, conv_w_full,
                                    conv_b_full, w_dt, dt_bias, a_log, d_inner=d_inner, heads=heads)
        y, hfin = _ssd_scan(zx, dtc, dtcT, h0, d_inner=d_inner, heads=heads)
        out = _ssd_out(y.reshape(2, b_ * t_, d_inner), zx.reshape(b_ * t_, n_main), d_exp, ssd_norm[0],
                       w_out0, tok3.reshape(b_ * t_, d), r3(gate), seq=t_, d_inner=d_inner)
        return out, hfin

    h_zero = jnp.zeros((bsz, 2, SSD_GROUPS, SSD_STATE, gw), F32)
    ctx2, h_ctx = ssd_mixer(ctx, sh_c, sc_c, gt_c, h_zero)
    x2, _ = ssd_mixer(x, sh_x, sc_x, gt_x, h_ctx)

    ffn_in = ffn_w_in[0].astype(BF16)
    ffn_out = ffn_w_out[0].astype(BF16)
    (sh_x, sh_c), (sc_x, sc_c), (gt_x, gt_c) = mod(0, 3), mod(0, 4), mod(0, 5)
    xt = _ffn(x2.reshape(bsz, seq, d), norm_ffn[0], r3(sh_x), r3(sc_x), r3(gt_x), ffn_in, ffn_out,
              time_major=True).reshape(seq * bsz, d)
    ctx2 = _ffn(ctx2.reshape(bsz, ctx_len, d), norm_ffn[0], r3(sh_c), r3(sc_c), r3(gt_c), ffn_in, ffn_out)

    ct = ctx2.reshape(bsz, ctx_len, d).transpose(1, 0, 2).reshape(ctx_len * bsz, d)

    ar, ai, bbr, bbi = _s5_discretize(s5_lam_re[0], s5_lam_im[0], s5_log_step[0], s5_b_re[0], s5_b_im[0])
    n_groups = ar.shape[1]
    n_gb = n_groups // S5_GB
    wb = jnp.concatenate([_blockdiag(bbr.transpose(0, 1, 3, 2), n_gb),
                          _blockdiag(bbi.transpose(0, 1, 3, 2), n_gb)], axis=-1).astype(BF16)
    wc = jnp.concatenate([_blockdiag(s5_c_re[0].transpose(0, 1, 3, 2), n_gb),
                          _blockdiag(-s5_c_im[0].transpose(0, 1, 3, 2), n_gb)], axis=2).astype(BF16)
    a_coef = jnp.stack([ar.reshape(2, -1), ai.reshape(2, -1)], axis=1)
    nstate = a_coef.shape[-1]

    (sh_x, sh_c), (sc_x, sc_c), (gt_x, _) = mod(1, 0), mod(1, 1), mod(1, 2)
    bc = lambda a: jnp.broadcast_to(a, (bsz, d))
    s_zero = jnp.zeros((2, 2, bsz, nstate), F32)
    _, s_ctx = _s5_scan(ct, norm_mix[1], bc(sh_c), bc(sc_c), wb, wc, a_coef, s_zero, bsz=bsz)
    y, _ = _s5_scan(xt, norm_mix[1], sh_x, sc_x, wb, wc, a_coef, s_ctx, bsz=bsz)
    xt = _s5_glu(xt, y, norm_mix[1], sh_x, sc_x, gt_x, s5_d[0], s5_w_glu[0].astype(BF16), s5_b_glu[0], bsz=bsz)

    (sh_x, _), (sc_x, _), (gt_x, _) = mod(1, 3), mod(1, 4), mod(1, 5)
    return _moe(xt, norm_ffn[1], sh_x, sc_x, gt_x, moe_router_w[0], moe_router_b[0],
                moe_w_in[0].astype(BF16), moe_w_out[0].astype(BF16), norm_final, bsz=bsz, grid_rows=rows)
```

```python
import functools

import jax
import jax.numpy as jnp
from jax import lax
from jax.experimental import pallas as pl
from jax.experimental.pallas import tpu as pltpu

F32 = jnp.float32
BF16 = jnp.bfloat16
HIGHEST = lax.Precision.HIGHEST

EPS = 1e-6
GRID_W = 64
N_MOD = 6

SSD_HEADDIM = 64
SSD_GROUPS = 4
SSD_STATE = 128
SSD_CONV = 5
SSD_CHUNK = 128

S5_CH = 16
S5_STATE = 64
S5_GB = 8

TOP_K = 2

V7X_VMEM_BYTES = 64 * 1024 * 1024
VMEM_LIMIT = V7X_VMEM_BYTES - 4 * 1024 * 1024
NEG_BIG = -1e30
LANES = 128
LOG2E = 1.4426950408889634


def _fill_lane_tiles(res, tr_scr):
    for lt in range(res.shape[1] // LANES):
        tr_scr[lt] = res[:, lt * LANES:(lt + 1) * LANES]


def _strided_rows_out(tr_scr, dst, *, n_groups, group_rows, stride, base=0):
    for g in range(n_groups):
        for lt in range(tr_scr.shape[0]):
            dst(g, lt)[...] = tr_scr[lt, pl.ds(base + g, group_rows, stride=stride), :]


def _params(*sem):
    return pltpu.CompilerParams(dimension_semantics=sem, vmem_limit_bytes=VMEM_LIMIT)


def _silu(v):
    return v * jax.nn.sigmoid(v)


def _softplus(v):
    return jnp.maximum(v, 0.0) + jnp.log1p(jnp.exp(-jnp.abs(v)))


def _rms(x, g):
    inv = lax.rsqrt(jnp.mean(x * x, axis=-1, keepdims=True) + EPS)
    return (x * inv) * g


def _nt_dot(a, b, precision=None):
    return lax.dot_general(a, b, (((1,), (1,)), ((), ())), precision=precision,
                           preferred_element_type=F32)


def _ada_kernel(c_ref, w_ref, b_ref, o_ref):
    s = _silu(c_ref[...])
    o_ref[0] = jnp.dot(s, w_ref[0], precision=HIGHEST, preferred_element_type=F32) + b_ref[0]


def _ada(cc, ada_w, ada_b):
    depth, d, n = ada_w.shape
    r = cc.shape[0]
    tn = 1536
    return pl.pallas_call(
        _ada_kernel,
        grid=(depth, n // tn),
        in_specs=[pl.BlockSpec((r, d), lambda i, j: (0, 0)),
                  pl.BlockSpec((1, d, tn), lambda i, j: (i, 0, j)),
                  pl.BlockSpec((1, 1, tn), lambda i, j: (i, 0, j))],
        out_specs=pl.BlockSpec((1, r, tn), lambda i, j: (i, 0, j)),
        out_shape=jax.ShapeDtypeStruct((depth, r, n), F32),
        compiler_params=_params("arbitrary", "arbitrary"),
        name="ada_mod",
    )(cc, ada_w, ada_b.reshape(depth, 1, n))


def _ssd_inproj_kernel(x_ref, g_ref, sh_ref, sc_ref, w_ref, cw_ref, cb_ref, wdt_ref, wdtT_ref,
                       bdt_ref, bdtT_ref, alog_ref, alogT_ref,
                       o_ref, dt_ref, dtT_ref, h_scr, pad0_scr, pad1_scr, *, n_conv, n_cols, seq, heads,
                       row_chunk):
    j = pl.program_id(1)
    pad = 8
    bufs = (pad0_scr, pad1_scr)

    def matmul(buf):
        buf[pad:pad + seq, :] = jnp.dot(h_scr[...], w_ref[...], preferred_element_type=F32)

    def finish_conv(buf):
        half = (SSD_CONV - 1) // 2
        for r in range(0, seq, row_chunk):
            s = cb_ref[...]
            for k in range(SSD_CONV):
                lo = pad + r + k - half
                s = s + cw_ref[k:k + 1, :] * buf[lo:lo + row_chunk, :]
            o_ref[0, r:r + row_chunk, :] = (s * (1.0 + jnp.tanh(s))).astype(o_ref.dtype)

    def finish_plain(buf):
        o_ref[0] = buf[pad:pad + seq, :].astype(o_ref.dtype)

    @pl.when(j == 0)
    def _():
        for r in range(0, seq, row_chunk):
            rs = slice(r, r + row_chunk)
            h = _rms(x_ref[0, rs, :], g_ref[...]) * (1.0 + sc_ref[0]) + sh_ref[0]
            hb = h.astype(BF16)
            h_scr[rs, :] = hb
            hl = (h - hb.astype(F32)).astype(BF16)
            nh2 = 2 * heads
            p = jnp.dot(hb, wdt_ref[...], preferred_element_type=F32)
            q = jnp.dot(hl, wdt_ref[...], preferred_element_type=F32)
            dt = _softplus(p[:, :nh2] + (p[:, nh2:] + q[:, :nh2]) + bdt_ref[...])
            da = dt * (-jnp.exp(alog_ref[...]))
            pT = _nt_dot(wdtT_ref[...], hb)
            qT = _nt_dot(wdtT_ref[...], hl)
            dtT = _softplus(pT[:nh2, :] + (pT[nh2:, :] + qT[:nh2, :]) + bdtT_ref[...])
            daT = dtT * (-jnp.exp(alogT_ref[...]))
            for d in range(2):
                sl = slice(d * heads, (d + 1) * heads)
                dt_ref[d, 0, rs, :] = jnp.concatenate([dt[:, sl], da[:, sl]], axis=-1)
                dtT_ref[d, 0, :, rs] = jnp.concatenate([dtT[sl, :], daT[sl, :]], axis=0)
        zeros = jnp.zeros((pad, pad0_scr.shape[1]), F32)
        for buf in bufs:
            buf[0:pad, :] = zeros
            buf[pad + seq:2 * pad + seq, :] = zeros
        matmul(bufs[0])

    for par in (0, 1):
        cur, prev = bufs[par], bufs[1 - par]
        mine = lax.rem(j, 2) == par

        @pl.when(mine & (j >= 1) & (j <= n_conv))
        def _():
            matmul(cur)
            finish_conv(prev)

        @pl.when(mine & (j > n_conv) & (j < n_cols))
        def _():
            matmul(cur)
            finish_plain(prev)

        @pl.when(mine & (j > n_conv) & (j == n_cols))
        def _():
            finish_plain(prev)


def _ssd_inproj(x3, g, shift, scale, w_main, conv_w_full, conv_b_full, w_dt, dt_bias, a_log,
                *, d_inner, heads):
    bsz, seq, d = x3.shape
    n = w_main.shape[1]
    tn = 512
    nh2 = 2 * heads
    per_batch = shift.shape[0] > 1
    mod_map = (lambda b, j: (b, 0, 0)) if per_batch else (lambda b, j: (0, 0, 0))
    n_cols = n // tn
    n_plain = d_inner // tn
    n_conv = n_cols - n_plain
    assert 0 < n_conv < n_cols

    def col(j):
        return jnp.where(j < n_conv, j + n_plain, j - n_conv)

    cur_col = lambda b, j: (0, col(jnp.minimum(j, n_cols - 1)))
    fin_col = lambda b, j: (0, col(jnp.maximum(j - 1, 0)))
    kern = functools.partial(_ssd_inproj_kernel, n_conv=n_conv, n_cols=n_cols, seq=seq, heads=heads,
                             row_chunk=min(256, seq))
    w_hi = w_dt.astype(BF16)
    w_dt2 = jnp.concatenate([w_hi, (w_dt - w_hi.astype(F32)).astype(BF16)], axis=1)
    const2 = lambda b, j: (0, 0)
    return pl.pallas_call(
        kern,
        grid=(bsz, n_cols + 1),
        in_specs=[pl.BlockSpec((1, seq, d), lambda b, j: (b, 0, 0), pipeline_mode=pl.Buffered(1)),
                  pl.BlockSpec((1, d), const2),
                  pl.BlockSpec((1, 1, d), mod_map),
                  pl.BlockSpec((1, 1, d), mod_map),
                  pl.BlockSpec((d, tn), cur_col),
                  pl.BlockSpec((SSD_CONV, tn), fin_col),
                  pl.BlockSpec((1, tn), fin_col),
                  pl.BlockSpec((d, 2 * nh2), const2),
                  pl.BlockSpec((2 * nh2, d), const2),
                  pl.BlockSpec((1, nh2), const2),
                  pl.BlockSpec((nh2, 1), const2),
                  pl.BlockSpec((1, nh2), const2),
                  pl.BlockSpec((nh2, 1), const2)],
        out_specs=[pl.BlockSpec((1, seq, tn), lambda b, j: (b,) + fin_col(b, j)),
                   pl.BlockSpec((2, 1, seq, nh2), lambda b, j: (0, b, 0, 0)),
                   pl.BlockSpec((2, 1, nh2, seq), lambda b, j: (0, b, 0, 0))],
        out_shape=[jax.ShapeDtypeStruct((bsz, seq, n), BF16),
                   jax.ShapeDtypeStruct((2, bsz, seq, nh2), F32),
                   jax.ShapeDtypeStruct((2, bsz, nh2, seq), F32)],
        scratch_shapes=[pltpu.VMEM((seq, d), BF16), pltpu.VMEM((seq + 16, tn), F32),
                        pltpu.VMEM((seq + 16, tn), F32)],
        compiler_params=_params("arbitrary", "arbitrary"),
        name="ssd_inproj",
    )(x3, g.reshape(1, d), shift, scale, w_main, conv_w_full, conv_b_full.reshape(1, n),
      w_dt2, w_dt2.T, dt_bias.reshape(1, nh2), dt_bias.reshape(nh2, 1),
      a_log.reshape(1, nh2), a_log.reshape(nh2, 1))


def _ssd_scan_kernel(xs_ref, b_ref, c_ref, dtc_ref, dtcT_ref, h0_ref, y_ref, hfin_ref,
                     st_scr, *, heads, n_chunks):
    d = pl.program_id(1)
    k = pl.program_id(2)
    L = xs_ref.shape[1]
    hpg = heads // SSD_GROUPS
    pairs_per_group = hpg // 2
    gw = hpg * SSD_HEADDIM

    @pl.when(k == 0)
    def _():
        st_scr[...] = h0_ref[0, 0]

    row = lax.broadcasted_iota(jnp.int32, (L, L), 0)
    col = lax.broadcasted_iota(jnp.int32, (L, L), 1)
    mask = jnp.where(d == 0, row - col, col - row) >= 0
    mask_f = mask.astype(F32)

    dtc = dtc_ref[0, 0]
    dtcT = dtcT_ref[0, 0]
    assert L == 2 * SSD_HEADDIM, "the head-pair tiles assume chunk length == 2 * head dim"
    da = dtc[:, heads:]
    dtT, daT = dtcT[:heads, :], dtcT[heads:, :]
    cum = jnp.dot(mask_f, da, precision=HIGHEST, preferred_element_type=F32)
    cumT = _nt_dot(daT, mask_f, precision=HIGHEST)
    tot = jnp.sum(da, axis=0, keepdims=True)
    totT = jnp.sum(daT, axis=1, keepdims=True)
    etot = jnp.exp(tot)
    c2 = cum * LOG2E
    c2T = cumT * LOG2E
    wrow = jnp.exp(totT - cumT) * dtT

    lane = lax.broadcasted_iota(jnp.int32, (L, 2 * SSD_HEADDIM), 1)
    lo = lane < SSD_HEADDIM
    lo_row = lax.broadcasted_iota(jnp.int32, (1, 2 * SSD_HEADDIM), 1) < SSD_HEADDIM

    for g in range(SSD_GROUPS):
        bg = b_ref[0, :, g * SSD_STATE:(g + 1) * SSD_STATE]
        cg = c_ref[0, :, g * SSD_STATE:(g + 1) * SSD_STATE]
        scores = _nt_dot(cg, bg) * mask_f
        bgT = bg.astype(F32).T
        yoff = jnp.dot(cg, st_scr[g].astype(BF16), preferred_element_type=F32)
        for q in range(pairs_per_group):
            p = g * pairs_per_group + q
            h0, h1 = 2 * p, 2 * p + 1
            xs2 = xs_ref[0, :, p * 128:(p + 1) * 128]
            zero = jnp.zeros_like(xs2)
            rhs = jnp.concatenate([jnp.where(lo, xs2, zero), jnp.where(lo, zero, xs2)], axis=0)
            ms, bs, es = [], [], []
            for h in (h0, h1):
                col = jnp.broadcast_to(c2[:, h:h + 1], (L, L))
                dec = jnp.exp2(jnp.minimum(col - c2T[h:h + 1, :], 0.0))
                ms.append((scores * dec * dtT[h:h + 1, :]).astype(BF16))
                bs.append((bgT * wrow[h:h + 1, :]).astype(BF16))
                es.append(jnp.exp2(col))
            ydiag = jnp.dot(jnp.concatenate(ms, axis=1), rhs, preferred_element_type=F32)
            new = jnp.dot(jnp.concatenate(bs, axis=1), rhs, preferred_element_type=F32)
            sl = slice(q * 128, (q + 1) * 128)
            y2 = ydiag + yoff[:, sl] * jnp.where(lo, es[0], es[1])
            y_ref[0, 0, :, p * 128:(p + 1) * 128] = y2.astype(y_ref.dtype)
            et2 = jnp.where(lo_row, etot[:, h0:h0 + 1], etot[:, h1:h1 + 1])
            st_scr[g, :, sl] = st_scr[g, :, sl] * et2 + new

    @pl.when(k == n_chunks - 1)
    def _():
        hfin_ref[0, 0] = st_scr[...]


def _ssd_scan(zx, dtc, dtcT, h0, *, d_inner, heads):
    bsz, seq, _ = zx.shape
    L = min(SSD_CHUNK, seq)
    nc = seq // L
    bc_w = SSD_GROUPS * SSD_STATE
    gw = (heads // SSD_GROUPS) * SSD_HEADDIM
    nh2 = 2 * heads
    xs_blk = d_inner // d_inner
    b_blk = (2 * d_inner) // bc_w
    c_blk = b_blk + 1

    def chunk(d, k):
        return k + d * (nc - 1 - 2 * k)

    kern = functools.partial(_ssd_scan_kernel, heads=heads, n_chunks=nc)
    st_shape = (SSD_GROUPS, SSD_STATE, gw)
    return pl.pallas_call(
        kern,
        grid=(bsz, 2, nc),
        in_specs=[pl.BlockSpec((1, L, d_inner), lambda b, d, k: (b, chunk(d, k), xs_blk)),
                  pl.BlockSpec((1, L, bc_w), lambda b, d, k: (b, chunk(d, k), b_blk)),
                  pl.BlockSpec((1, L, bc_w), lambda b, d, k: (b, chunk(d, k), c_blk)),
                  pl.BlockSpec((1, 1, L, nh2), lambda b, d, k: (d, b, chunk(d, k), 0)),
                  pl.BlockSpec((1, 1, nh2, L), lambda b, d, k: (d, b, 0, chunk(d, k))),
                  pl.BlockSpec((1, 1) + st_shape, lambda b, d, k: (b, d, 0, 0, 0))],
        out_specs=[pl.BlockSpec((1, 1, L, d_inner), lambda b, d, k: (d, b, chunk(d, k), 0)),
                   pl.BlockSpec((1, 1) + st_shape, lambda b, d, k: (b, d, 0, 0, 0))],
        out_shape=[jax.ShapeDtypeStruct((2, bsz, seq, d_inner), BF16),
                   jax.ShapeDtypeStruct((bsz, 2) + st_shape, F32)],
        scratch_shapes=[pltpu.VMEM(st_shape, F32)],
        compiler_params=_params("arbitrary", "arbitrary", "arbitrary"),
        name="ssd_scan",
    )(zx, zx, zx, dtc, dtcT, h0)


def _ssd_out_kernel(y_ref, z_ref, xs_ref, dsk_ref, nw_ref, w_ref, x_ref, gate_ref, o_ref):
    y = y_ref[0].astype(F32) + y_ref[1].astype(F32) + dsk_ref[...] * xs_ref[...].astype(F32)
    yg = y * _silu(z_ref[...].astype(F32))
    yn = _rms(yg, nw_ref[...]).astype(BF16)
    out = jnp.dot(yn, w_ref[...], preferred_element_type=F32)
    o_ref[...] = x_ref[...] + gate_ref[0] * out


def _ssd_out(y, zx, d_exp, norm_w, w_out, x2, gate, *, seq, d_inner):
    m, d = x2.shape
    tm = min(512, seq)
    bpt = seq // tm
    per_batch = gate.shape[0] > 1
    gate_map = (lambda i: (i // bpt, 0, 0)) if per_batch else (lambda i: (0, 0, 0))
    return pl.pallas_call(
        _ssd_out_kernel,
        grid=(m // tm,),
        in_specs=[pl.BlockSpec((2, tm, d_inner), lambda i: (0, i, 0)),
                  pl.BlockSpec((tm, d_inner), lambda i: (i, 0)),
                  pl.BlockSpec((tm, d_inner), lambda i: (i, 1)),
                  pl.BlockSpec((1, d_inner), lambda i: (0, 0)),
                  pl.BlockSpec((1, d_inner), lambda i: (0, 0)),
                  pl.BlockSpec((d_inner, d), lambda i: (0, 0)),
                  pl.BlockSpec((tm, d), lambda i: (i, 0)),
                  pl.BlockSpec((1, 1, d), gate_map)],
        out_specs=pl.BlockSpec((tm, d), lambda i: (i, 0)),
        out_shape=jax.ShapeDtypeStruct((m, d), F32),
        compiler_params=_params("arbitrary"),
        name="ssd_out",
    )(y, zx, zx, d_exp.reshape(1, d_inner), norm_w.reshape(1, d_inner), w_out, x2, gate)


def _ffn_kernel(x_ref, g_ref, sh_ref, sc_ref, gate_ref, wg_ref, wu_ref, wo_ref, o_ref,
                h_scr, acc_scr, tr_scr, *, n_f, time_major):
    j = pl.program_id(1)
    bb, tt, dm = x_ref.shape

    @pl.when(j == 0)
    def _():
        h = _rms(x_ref[...], g_ref[...]) * (1.0 + sc_ref[...]) + sh_ref[...]
        h_scr[...] = h.reshape(bb * tt, dm).astype(BF16)
        acc_scr[...] = jnp.zeros_like(acc_scr)

    h = h_scr[...]
    gv = jnp.dot(h, wg_ref[...], preferred_element_type=F32)
    uv = jnp.dot(h, wu_ref[...], preferred_element_type=F32)
    a = (_silu(gv) * uv).astype(BF16)
    acc_scr[...] += jnp.dot(a, wo_ref[...], preferred_element_type=F32)

    @pl.when(j == n_f - 1)
    def _():
        res = x_ref[...] + gate_ref[...] * acc_scr[...].reshape(bb, tt, dm)
        if time_major:
            _fill_lane_tiles(res.reshape(bb * tt, dm), tr_scr)
            _strided_rows_out(tr_scr, lambda c, lt: o_ref.at[c, 0, :, lt * LANES:(lt + 1) * LANES],
                              n_groups=tt, group_rows=bb, stride=tt)
        else:
            o_ref[...] = res


def _ffn(x3, g, shift, scale, gate, w_in, w_out, *, time_major=False):
    bsz, seq, d = x3.shape
    f = w_out.shape[0]
    tf = f // 2 if (f // 2) % 128 == 0 else f
    n_f = f // tf
    per_batch = shift.shape[0] > 1
    if time_major:
        bb, tt = 8, GRID_W
        assert bsz % bb == 0 and seq % tt == 0 and per_batch
        rows = seq // tt
        x_map = lambda i, j: (i // rows, i % rows, 0)
        mod_map = lambda i, j: (i // rows, 0, 0)
        out_spec = pl.BlockSpec((tt, 1, bb, d), lambda i, j: (0, i % rows, i // rows, 0))
        out_shape = jax.ShapeDtypeStruct((tt, rows, bsz, d), F32)
        n_blocks = (bsz // bb) * rows
    else:
        bb, tt = 1, min(512, seq)
        bpt = seq // tt
        x_map = lambda i, j: (i // bpt, i % bpt, 0)
        mod_map = (lambda i, j: (i // bpt, 0, 0)) if per_batch else (lambda i, j: (0, 0, 0))
        out_spec = pl.BlockSpec((bb, tt, d), x_map)
        out_shape = jax.ShapeDtypeStruct((bsz, seq, d), F32)
        n_blocks = bsz * bpt
    mb = bb if per_batch else 1
    tm = bb * tt
    kern = functools.partial(_ffn_kernel, n_f=n_f, time_major=time_major)
    return pl.pallas_call(
        kern,
        grid=(n_blocks, n_f),
        in_specs=[pl.BlockSpec((bb, tt, d), x_map),
                  pl.BlockSpec((1, d), lambda i, j: (0, 0)),
                  pl.BlockSpec((mb, 1, d), mod_map),
                  pl.BlockSpec((mb, 1, d), mod_map),
                  pl.BlockSpec((mb, 1, d), mod_map),
                  pl.BlockSpec((d, tf), lambda i, j: (0, j)),
                  pl.BlockSpec((d, tf), lambda i, j: (0, n_f + j)),
                  pl.BlockSpec((tf, d), lambda i, j: (j, 0))],
        out_specs=out_spec,
        out_shape=out_shape,
        scratch_shapes=[pltpu.VMEM((tm, d), BF16), pltpu.VMEM((tm, d), F32),
                        pltpu.VMEM((d // LANES, tm if time_major else 8, LANES), F32)],
        compiler_params=_params("arbitrary", "arbitrary"),
        name="dense_swiglu",
    )(x3, g.reshape(1, d), shift, scale, gate, w_in, w_in, w_out)


def _s5_disc_kernel(lre_ref, lim_ref, step_ref, bre_ref, bim_ref, ar_ref, ai_ref, bbr_ref, bbi_ref):
    step = jnp.exp(step_ref[...])
    lre, lim = lre_ref[...], lim_ref[...]
    mag = jnp.exp(lre * step)
    ar = mag * jnp.cos(lim * step)
    ai = mag * jnp.sin(lim * step)
    den = lre * lre + lim * lim
    cr = ((ar - 1.0) * lre + ai * lim) / den
    ci = (ai * lre - (ar - 1.0) * lim) / den
    ar_ref[...] = ar
    ai_ref[...] = ai
    bbr_ref[...] = cr * bre_ref[...] - ci * bim_ref[...]
    bbi_ref[...] = cr * bim_ref[...] + ci * bre_ref[...]


def _s5_discretize(lam_re, lam_im, log_step, b_re, b_im):
    two, g, p = lam_re.shape
    c = b_re.shape[-1]
    lre = jnp.repeat(lam_re, c, axis=-1)
    lim = jnp.repeat(lam_im, c, axis=-1)
    shp = jax.ShapeDtypeStruct((two, g, p * c), F32)
    ar, ai, bbr, bbi = pl.pallas_call(
        _s5_disc_kernel, out_shape=[shp, shp, shp, shp], name="s5_discretize",
    )(lre, lim, log_step.reshape(two, g, 1), b_re.reshape(two, g, p * c), b_im.reshape(two, g, p * c))
    ar = ar.reshape(two, g, p, c)[..., 0]
    ai = ai.reshape(two, g, p, c)[..., 0]
    return ar, ai, bbr.reshape(two, g, p, c), bbi.reshape(two, g, p, c)


def _s5_kernel(x_ref, g_ref, sh_ref, sc_ref, wb_ref, wc_ref, a_ref, s0_ref, *rest, n_chunks, steps, bsz, n_cast):
    cast_in, (y_ref, sfin_ref), cast_out = rest[:n_cast], rest[n_cast:n_cast + 2], rest[n_cast + 2:2 * n_cast + 2]
    st_scr, bu0_scr, bu1_scr, sb0_scr, sb1_scr = rest[2 * n_cast + 2:]
    for src, dst in zip(cast_in, cast_out):
        dst[...] = src[...].astype(dst.dtype)
    bu_bufs = (bu0_scr, bu1_scr)
    sb_bufs = (sb0_scr, sb1_scr)
    d = pl.program_id(0)
    k = pl.program_id(1)
    rows, dm = x_ref.shape
    gl = S5_GB * S5_STATE
    cb = S5_GB * S5_CH

    @pl.when(k == 0)
    def _():
        st_scr[...] = s0_ref[0]

    hn = _rms(x_ref[...], g_ref[...]).reshape(steps, bsz, dm)
    h = hn * (1.0 + sc_ref[...])[None] + sh_ref[...][None]
    u = h.reshape(rows, dm).astype(BF16)

    n_gb = dm // cb

    def bu_dot(gb):
        return jnp.dot(u[:, gb * cb:(gb + 1) * cb], wb_ref[0, gb], preferred_element_type=F32)

    bu_bufs[0][...] = bu_dot(0)
    for gb in range(n_gb):
        bu_cur, bu_nxt = bu_bufs[gb % 2], bu_bufs[1 - gb % 2]
        sb_cur = sb_bufs[gb % 2]
        if gb + 1 < n_gb:
            bu_nxt[...] = bu_dot(gb + 1)
        sl = slice(gb * gl, (gb + 1) * gl)
        ar = jnp.broadcast_to(a_ref[0, 0:1, sl], (bsz, gl))
        ai = jnp.broadcast_to(a_ref[0, 1:2, sl], (bsz, gl))
        sr, si = st_scr[0, :, sl], st_scr[1, :, sl]
        for i in range(steps):
            t = jnp.where(d == 0, i, steps - 1 - i)
            r0 = pl.multiple_of(t * bsz, bsz)
            bur = bu_cur[pl.ds(r0, bsz), 0:gl]
            bui = bu_cur[pl.ds(r0, bsz), gl:2 * gl]
            sr, si = ar * sr - ai * si + bur, ar * si + ai * sr + bui
            sb_cur[pl.ds(r0, bsz), 0:gl] = sr.astype(BF16)
            sb_cur[pl.ds(r0, bsz), gl:2 * gl] = si.astype(BF16)
        st_scr[0, :, sl] = sr
        st_scr[1, :, sl] = si
        yv = jnp.dot(sb_cur[...], wc_ref[0, gb], preferred_element_type=F32)
        y_ref[0, :, gb * cb:(gb + 1) * cb] = yv.astype(y_ref.dtype)

    @pl.when(k == n_chunks - 1)
    def _():
        sfin_ref[0] = st_scr[...]


def _s5_scan(xt, g, shift, scale, wb, wc, a, s0, *, bsz, cast=()):
    m, d = xt.shape
    seq = m // bsz
    steps = min(64, seq)
    rows = steps * bsz
    nc = seq // steps
    nstate = s0.shape[-1]
    n_gb = d // (S5_GB * S5_CH)
    gl = S5_GB * S5_STATE

    def chunk(dd, k):
        return k + dd * (nc - 1 - 2 * k)

    n_steps = 2 * nc
    cast_specs, cast_shapes = [], []
    for w in cast:
        slab = w.shape[0] // n_steps
        assert slab * n_steps == w.shape[0] and slab % 16 == 0
        cast_specs.append(pl.BlockSpec((slab, w.shape[1]), lambda dd, k: (dd * nc + k, 0)))
        cast_shapes.append(jax.ShapeDtypeStruct(w.shape, BF16))

    kern = functools.partial(_s5_kernel, n_chunks=nc, steps=steps, bsz=bsz, n_cast=len(cast))
    return pl.pallas_call(
        kern,
        grid=(2, nc),
        in_specs=[pl.BlockSpec((rows, d), lambda dd, k: (chunk(dd, k), 0)),
                  pl.BlockSpec((1, d), lambda dd, k: (0, 0)),
                  pl.BlockSpec((bsz, d), lambda dd, k: (0, 0)),
                  pl.BlockSpec((bsz, d), lambda dd, k: (0, 0)),
                  pl.BlockSpec((1, n_gb, S5_GB * S5_CH, 2 * gl), lambda dd, k: (dd, 0, 0, 0)),
                  pl.BlockSpec((1, n_gb, 2 * gl, S5_GB * S5_CH), lambda dd, k: (dd, 0, 0, 0)),
                  pl.BlockSpec((1, 2, nstate), lambda dd, k: (dd, 0, 0)),
                  pl.BlockSpec((1, 2, bsz, nstate), lambda dd, k: (dd, 0, 0, 0))] + cast_specs,
        out_specs=[pl.BlockSpec((1, rows, d), lambda dd, k: (dd, chunk(dd, k), 0)),
                   pl.BlockSpec((1, 2, bsz, nstate), lambda dd, k: (dd, 0, 0, 0))] + cast_specs,
        out_shape=[jax.ShapeDtypeStruct((2, m, d), BF16),
                   jax.ShapeDtypeStruct((2, 2, bsz, nstate), F32)] + cast_shapes,
        scratch_shapes=[pltpu.VMEM((2, bsz, nstate), F32),
                        pltpu.VMEM((rows, 2 * gl), F32), pltpu.VMEM((rows, 2 * gl), F32),
                        pltpu.VMEM((rows, 2 * gl), BF16), pltpu.VMEM((rows, 2 * gl), BF16)],
        compiler_params=_params("arbitrary", "arbitrary"),
        name="s5_scan",
    )(xt, g.reshape(1, d), shift, scale, wb, wc, a, s0, *cast)


def _gelu_tanh(v):
    return 0.5 * v * (1.0 + jnp.tanh(0.7978845608028654 * (v + 0.044715 * (v * v * v))))


def _s5_glu_kernel(x_ref, y_ref, g_ref, sh_ref, sc_ref, gate_ref, dsk_ref, w_ref, b_ref, o_ref, *, bsz):
    rows, dm = x_ref.shape
    x = x_ref[...]
    hn = _rms(x, g_ref[...]).reshape(rows // bsz, bsz, dm)
    h = (hn * (1.0 + sc_ref[...])[None] + sh_ref[...][None]).reshape(rows, dm)
    yv = dsk_ref[...] * h + y_ref[0].astype(F32) + y_ref[1].astype(F32)
    ge = _gelu_tanh(yv).astype(BF16)
    o = jnp.dot(ge, w_ref[...], preferred_element_type=F32) + b_ref[...]
    out = o[:, :dm] * jax.nn.sigmoid(o[:, dm:])
    res = x.reshape(rows // bsz, bsz, dm) + gate_ref[...][None] * out.reshape(rows // bsz, bsz, dm)
    o_ref[...] = res.reshape(rows, dm)


def _s5_glu(xt, y, g, shift, scale, gate, d_skip, w_glu, b_glu, *, bsz):
    m, d = xt.shape
    tm = min(512, m)
    kern = functools.partial(_s5_glu_kernel, bsz=bsz)
    full = lambda i: (0, 0)
    return pl.pallas_call(
        kern,
        grid=(m // tm,),
        in_specs=[pl.BlockSpec((tm, d), lambda i: (i, 0)),
                  pl.BlockSpec((2, tm, d), lambda i: (0, i, 0)),
                  pl.BlockSpec((1, d), full),
                  pl.BlockSpec((bsz, d), full),
                  pl.BlockSpec((bsz, d), full),
                  pl.BlockSpec((bsz, d), full),
                  pl.BlockSpec((1, d), full),
                  pl.BlockSpec((d, 2 * d), full),
                  pl.BlockSpec((1, 2 * d), full)],
        out_specs=pl.BlockSpec((tm, d), lambda i: (i, 0)),
        out_shape=jax.ShapeDtypeStruct((m, d), F32),
        compiler_params=_params("arbitrary"),
        name="s5_glu",
    )(xt, y, g.reshape(1, d), shift, scale, gate, d_skip.reshape(1, d), w_glu, b_glu.reshape(1, 2 * d))


MOE_TB = 512
MOE_TILE = 512
MOE_WIN = 256
MOE_PAD = 16
MOE_LANES = 128


def _moe_cap(n_tok, n_blocks):
    rows = n_tok + n_blocks * MOE_PAD + MOE_TILE
    return -(-rows // MOE_TILE) * MOE_TILE


def _moe_route_kernel(x_ref, g_ref, sh_ref, sc_ref, rwT_ref, rbT_ref, ut_ref,
                      sorted_ref, meta_ref, tab_ref,
                      buf_scr, zero_scr, run_smem, sems, *, bsz, n_exp, n_blocks, local_rows):
    b = pl.program_id(0)
    slot = lax.rem(b, 2)
    dm = x_ref.shape[-1]
    rows = x_ref.shape[0] * x_ref.shape[1] * x_ref.shape[2]

    @pl.when(b == 0)
    def _():
        for e in range(n_exp):
            run_smem[e] = 0
        zero_scr[...] = jnp.zeros_like(zero_scr)

    hn = _rms(x_ref[...].reshape(rows // bsz, bsz, dm), g_ref[...])
    h = (hn * (1.0 + sc_ref[...])[None] + sh_ref[...][None]).reshape(rows, dm)
    hb = h.astype(BF16)

    logits = _nt_dot(rwT_ref[...], h, precision=HIGHEST) + rbT_ref[...]
    sub = lax.broadcasted_iota(jnp.int32, logits.shape, 0)
    m1 = jnp.max(logits, axis=0, keepdims=True)
    i1 = jnp.min(jnp.where(logits == m1, sub, n_exp), axis=0, keepdims=True)
    rest = jnp.where(sub == i1, -jnp.inf, logits)
    m2 = jnp.max(rest, axis=0, keepdims=True)
    i2 = jnp.min(jnp.where(rest == m2, sub, n_exp), axis=0, keepdims=True)
    e2 = jnp.exp(m2 - m1)
    g1 = 1.0 / (1.0 + e2)
    g2 = e2 * g1
    sel1 = sub == i1
    sel2 = sub == i2
    oh = jnp.where(sel1, 1.0, 0.0) + jnp.where(sel2, 1.0, 0.0)
    cnt = jnp.sum(oh, axis=1, keepdims=True).astype(jnp.int32)
    cpad_v = ((cnt + (MOE_PAD - 1)) // MOE_PAD) * MOE_PAD
    rank = jnp.dot(oh.astype(BF16), ut_ref[...], preferred_element_type=F32)

    cpad, loc, run = [], [], []
    off = 0
    for e in range(n_exp):
        cpad.append(cpad_v[e, 0])
        loc.append(off)
        off = off + cpad[e]
        run.append(run_smem[e])
    sub1 = lax.broadcasted_iota(jnp.int32, (n_exp, 1), 0)
    loc_v = jnp.zeros((n_exp, 1), jnp.int32)
    run_v = jnp.zeros((n_exp, 1), jnp.int32)
    for e in range(n_exp):
        loc_v = jnp.where(sub1 == e, loc[e], loc_v)
        run_v = jnp.where(sub1 == e, run[e], run_v)
    loc_v = loc_v.astype(F32)
    run_v = run_v.astype(F32)

    def pick(sel, v):
        return jnp.sum(jnp.where(sel, v, 0.0), axis=0, keepdims=True)

    lp1 = pick(sel1, loc_v + rank).astype(jnp.int32)
    lp2 = pick(sel2, loc_v + rank).astype(jnp.int32)
    pos1 = pick(sel1, run_v + rank)
    pos2 = pick(sel2, run_v + rank)

    r = lax.broadcasted_iota(jnp.int32, (local_rows, rows), 0)
    onehot = jnp.where(r == lp1, 1.0, jnp.where(r == lp2, 1.0, 0.0)).astype(BF16)
    buf_scr[slot, 0:local_rows, :] = jnp.dot(onehot, hb, preferred_element_type=F32).astype(BF16)
    buf_scr[slot, local_rows:local_rows + MOE_TILE, :] = jnp.zeros((MOE_TILE, dm), BF16)

    rec = jnp.concatenate([i1.astype(F32), i2.astype(F32), pos1, pos2, g1, g2,
                           jnp.zeros((MOE_LANES - 6, rows), F32)], axis=0)
    meta_ref[...] = rec.T

    base = b * (2 * n_exp)
    for e in range(n_exp):
        tab_ref[base + e] = run[e]
        tab_ref[base + n_exp + e] = cpad[e]
        run_smem[e] = run[e] + cpad[e]

    def group_copy(e, slot_, loc_e, run_e):
        src = buf_scr.at[slot_, pl.ds(pl.multiple_of(loc_e, MOE_PAD), MOE_TILE)]
        dst = sorted_ref.at[e, pl.ds(pl.multiple_of(run_e, MOE_PAD), MOE_TILE)]
        return pltpu.make_async_copy(src, dst, sems.at[slot_, e])

    @pl.when(b > 0)
    def _():
        for e in range(n_exp):
            group_copy(e, 1 - slot, 0, 0).wait()

    for e in range(n_exp):
        group_copy(e, slot, loc[e], run[e]).start()

    @pl.when(b == n_blocks - 1)
    def _():
        fin = (n_blocks) * (2 * n_exp)
        for e in range(n_exp):
            group_copy(e, slot, 0, 0).wait()
        for e in range(n_exp):
            tot = run[e] + cpad[e]
            tab_ref[fin + e] = tot
            tab_ref[fin + n_exp + e] = 0
            dst = sorted_ref.at[e, pl.ds(pl.multiple_of(tot, MOE_PAD), MOE_TILE)]
            pltpu.make_async_copy(zero_scr, dst, sems.at[slot, e]).start()
        for e in range(n_exp):
            dst = sorted_ref.at[e, pl.ds(0, MOE_TILE)]
            pltpu.make_async_copy(zero_scr, dst, sems.at[slot, e]).wait()


def _moe_expert_kernel(te_ref, tr_ref, na_ref, xs_ref, wg_ref, wu_ref, wo_ref, y_ref, acc_scr, *, n_f):
    i = pl.program_id(0)
    f = pl.program_id(1)

    @pl.when(i < na_ref[0])
    def _():
        @pl.when(f == 0)
        def _():
            acc_scr[...] = jnp.zeros_like(acc_scr)

        xs = xs_ref[0]
        gv = jnp.dot(xs, wg_ref[0], preferred_element_type=F32)
        uv = jnp.dot(xs, wu_ref[0], preferred_element_type=F32)
        a = (_silu(gv) * uv).astype(BF16)
        acc_scr[...] += jnp.dot(a, wo_ref[0], preferred_element_type=F32)

        @pl.when(f == n_f - 1)
        def _():
            y_ref[0] = acc_scr[...].astype(y_ref.dtype)


def _moe_combine_kernel(tab_ref, end_ref, x_ref, meta_ref, gate_ref, gf_ref, ys_ref, o_ref,
                        win_scr, acc_scr, tr_scr, sems, *, bsz, n_exp):
    b = pl.program_id(0)
    slot = lax.rem(b, 2)
    n_cols, n_rows, _, dm = x_ref.shape
    rows = n_cols * n_rows * bsz
    n_win = MOE_TILE // MOE_WIN

    def window(blk, slot_, e, c):
        base = blk * (2 * n_exp)
        run_e = tab_ref[base + e]
        live = tab_ref[base + n_exp + e] > c * MOE_WIN
        ws = jnp.minimum(run_e + c * MOE_WIN, end_ref[e] - MOE_WIN)
        k = e * n_win + c
        cp = pltpu.make_async_copy(ys_ref.at[e, pl.ds(pl.multiple_of(ws, MOE_PAD), MOE_WIN)],
                                   win_scr.at[slot_, k], sems.at[slot_, k])
        return live, ws, k, cp

    def fetch(blk, slot_):
        for e in range(n_exp):
            for c in range(n_win):
                live, _, _, cp = window(blk, slot_, e, c)

                @pl.when(live)
                def _():
                    cp.start()

    @pl.when(b == 0)
    def _():
        fetch(0, 0)

    fetch(b + 1, 1 - slot)

    meta = meta_ref[...]
    e1, e2 = meta[:, 0:1], meta[:, 1:2]
    pos1, pos2 = meta[:, 2:3].astype(jnp.int32), meta[:, 3:4].astype(jnp.int32)
    g1, g2 = meta[:, 4:5], meta[:, 5:6]
    acc_scr[...] = jnp.zeros_like(acc_scr)
    lane = lax.broadcasted_iota(jnp.int32, (rows, MOE_WIN), 1)

    for e in range(n_exp):
        for c in range(n_win):
            live, ws, k, cp = window(b, slot, e, c)

            @pl.when(live)
            def _():
                cp.wait()
                rel = jnp.where(e1 == e, pos1, pos2) - ws
                hit = jnp.where(e1 == e, 1.0, jnp.where(e2 == e, 1.0, 0.0))
                pe = jnp.where(rel == lane, hit, 0.0).astype(BF16)
                ge = jnp.where(e1 == e, g1, jnp.where(e2 == e, g2, 0.0))
                acc_scr[...] += ge * jnp.dot(pe, win_scr[slot, k], preferred_element_type=F32)

    upd = gate_ref[...][None] * acc_scr[...].reshape(rows // bsz, bsz, dm)
    xo = x_ref[...].reshape(rows, dm) + upd.reshape(rows, dm)
    _fill_lane_tiles(_rms(xo, gf_ref[...]), tr_scr)
    for rl in range(n_rows):
        _strided_rows_out(
            tr_scr, lambda bb, lt: o_ref.at[bb, rl, :, lt * LANES:(lt + 1) * LANES],
            n_groups=bsz, group_rows=n_cols, stride=n_rows * bsz, base=rl * bsz)


def _moe(xt, g, shift, scale, gate, router_w, router_b, w_in, w_out, g_final, *, bsz, grid_rows):
    m, d = xt.shape
    n_exp, f, _ = w_out.shape
    nb = m // MOE_TB
    cap = _moe_cap(m, nb)
    local_rows = -(-(TOP_K * MOE_TB + n_exp * (MOE_PAD - 1)) // 128) * 128
    full1 = lambda i: (0, 0)

    blk_cols = 8
    blk_rows = MOE_TB // (blk_cols * bsz)
    assert blk_rows * blk_cols * bsz == MOE_TB and grid_rows % blk_rows == 0 and GRID_W % blk_cols == 0
    nrg = grid_rows // blk_rows
    x4 = xt.reshape(GRID_W, grid_rows, bsz, d)
    x_spec1 = pl.BlockSpec((blk_cols, blk_rows, bsz, d), lambda i: (i // nrg, i % nrg, 0, 0))

    ut = (jnp.arange(MOE_TB)[:, None] < jnp.arange(MOE_TB)[None, :]).astype(BF16)
    route = functools.partial(_moe_route_kernel, bsz=bsz, n_exp=n_exp, n_blocks=nb, local_rows=local_rows)
    sorted_h, meta, tab = pl.pallas_call(
        route,
        grid=(nb,),
        in_specs=[x_spec1,
                  pl.BlockSpec((1, d), full1),
                  pl.BlockSpec((bsz, d), full1),
                  pl.BlockSpec((bsz, d), full1),
                  pl.BlockSpec((n_exp, d), full1),
                  pl.BlockSpec((n_exp, 1), full1),
                  pl.BlockSpec((MOE_TB, MOE_TB), full1)],
        out_specs=[pl.BlockSpec(memory_space=pl.ANY),
                   pl.BlockSpec((MOE_TB, MOE_LANES), lambda i: (i, 0)),
                   pl.BlockSpec(memory_space=pltpu.SMEM)],
        out_shape=[jax.ShapeDtypeStruct((n_exp, cap, d), BF16),
                   jax.ShapeDtypeStruct((m, MOE_LANES), F32),
                   jax.ShapeDtypeStruct(((nb + 1) * 2 * n_exp,), jnp.int32)],
        scratch_shapes=[pltpu.VMEM((2, local_rows + MOE_TILE, d), BF16),
                        pltpu.VMEM((MOE_TILE, d), BF16),
                        pltpu.SMEM((n_exp,), jnp.int32),
                        pltpu.SemaphoreType.DMA((2, n_exp))],
        compiler_params=_params("arbitrary"),
        name="moe_route",
    )(x4, g.reshape(1, d), shift, scale, router_w.T, router_b.reshape(n_exp, 1), ut)

    tot = tab[nb * 2 * n_exp: nb * 2 * n_exp + n_exp]
    ntile = jnp.maximum((tot + MOE_TILE - 1) // MOE_TILE, 1)
    cum = jnp.cumsum(ntile)
    n_active = cum[-1:]
    max_tiles = (TOP_K * m + nb * n_exp * (MOE_PAD - 1)) // MOE_TILE + n_exp + 1
    idc = jnp.minimum(jnp.arange(max_tiles, dtype=jnp.int32), n_active - 1)
    tile_e = jnp.sum(idc[:, None] >= cum[None, :], axis=1).astype(jnp.int32)
    tile_r = (idc - (cum - ntile)[tile_e]).astype(jnp.int32)
    end = (ntile * MOE_TILE).astype(jnp.int32)

    tf = f // 2 if (f // 2) % 128 == 0 else f
    n_f = f // tf

    def f_idx(i, j, na):
        return jnp.where(i < na[0], j, n_f - 1)

    ys = pl.pallas_call(
        functools.partial(_moe_expert_kernel, n_f=n_f),
        grid_spec=pltpu.PrefetchScalarGridSpec(
            num_scalar_prefetch=3,
            grid=(max_tiles, n_f),
            in_specs=[pl.BlockSpec((1, MOE_TILE, d), lambda i, j, te, tr, na: (te[i], tr[i], 0)),
                      pl.BlockSpec((1, d, tf), lambda i, j, te, tr, na: (te[i], 0, f_idx(i, j, na))),
                      pl.BlockSpec((1, d, tf), lambda i, j, te, tr, na: (te[i], 0, n_f + f_idx(i, j, na))),
                      pl.BlockSpec((1, tf, d), lambda i, j, te, tr, na: (te[i], f_idx(i, j, na), 0))],
            out_specs=pl.BlockSpec((1, MOE_TILE, d), lambda i, j, te, tr, na: (te[i], tr[i], 0)),
            scratch_shapes=[pltpu.VMEM((MOE_TILE, d), F32)]),
        out_shape=jax.ShapeDtypeStruct((n_exp, cap, d), BF16),
        compiler_params=_params("arbitrary", "arbitrary"),
        name="moe_experts",
    )(tile_e, tile_r, n_active.astype(jnp.int32), sorted_h, w_in, w_in, w_out)

    n_win = MOE_TILE // MOE_WIN
    out = pl.pallas_call(
        functools.partial(_moe_combine_kernel, bsz=bsz, n_exp=n_exp),
        grid_spec=pltpu.PrefetchScalarGridSpec(
            num_scalar_prefetch=2,
            grid=(nb,),
            in_specs=[pl.BlockSpec((blk_cols, blk_rows, bsz, d), lambda i, tb, en: (i // nrg, i % nrg, 0, 0)),
                      pl.BlockSpec((MOE_TB, MOE_LANES), lambda i, tb, en: (i, 0)),
                      pl.BlockSpec((bsz, d), lambda i, tb, en: (0, 0)),
                      pl.BlockSpec((1, d), lambda i, tb, en: (0, 0)),
                      pl.BlockSpec(memory_space=pl.ANY)],
            out_specs=pl.BlockSpec((bsz, blk_rows, blk_cols, d), lambda i, tb, en: (0, i % nrg, i // nrg, 0)),
            scratch_shapes=[pltpu.VMEM((2, n_exp * n_win, MOE_WIN, d), BF16),
                            pltpu.VMEM((MOE_TB, d), F32),
                            pltpu.VMEM((d // LANES, MOE_TB, LANES), F32),
                            pltpu.SemaphoreType.DMA((2, n_exp * n_win))]),
        out_shape=jax.ShapeDtypeStruct((bsz, grid_rows, GRID_W, d), F32),
        compiler_params=_params("arbitrary"),
        name="moe_combine",
    )(tab, end, x4, meta, gate, g_final.reshape(1, d), ys)
    return out.reshape(bsz, grid_rows * GRID_W, d)


def _blockdiag(w, n_gb):
    two, g, r, c = w.shape
    eye = jnp.eye(S5_GB, dtype=w.dtype)
    out = jnp.einsum("dbgrc,gh->dbgrhc", w.reshape(two, n_gb, S5_GB, r, c), eye)
    return out.reshape(two, n_gb, S5_GB * r, S5_GB * c)


def kernel(x, c, ctx, c_ctx, ada_w, ada_b, norm_mix, norm_ffn, ssd_w_in, ssd_conv_w, ssd_conv_b, ssd_dt_bias, ssd_a_log, ssd_d, ssd_norm, ssd_w_out, s5_lam_re, s5_lam_im, s5_log_step, s5_b_re, s5_b_im, s5_c_re, s5_c_im, s5_d, s5_w_glu, s5_b_glu, ffn_w_in, ffn_w_out, moe_router_w, moe_router_b, moe_w_in, moe_w_out, norm_final):
    bsz, seq, d = x.shape
    ctx_len = ctx.shape[1]
    depth = ada_w.shape[0]
    assert depth == 2, "one SSD layer followed by one S5 layer"
    rows = seq // GRID_W

    pad_rows = (-(bsz + 1)) % 8
    cc = jnp.concatenate([c, c_ctx[None], jnp.zeros((pad_rows, d), F32)], axis=0)
    mods = _ada(cc, ada_w, ada_b)

    def mod(i, n):
        sl = slice(n * d, (n + 1) * d)
        return mods[i, :bsz, sl], mods[i, bsz:bsz + 1, sl]

    heads = ssd_d.shape[1]
    d_inner = heads * SSD_HEADDIM
    conv_dim = ssd_conv_w.shape[2]
    n_main = d_inner + conv_dim
    w_in0 = ssd_w_in[0]
    w_main = w_in0[:, :n_main].astype(BF16)
    w_dt = w_in0[:, n_main:]
    conv_w_full = jnp.concatenate([jnp.zeros((SSD_CONV, d_inner), F32), 0.5 * ssd_conv_w[0]], axis=1)
    conv_b_full = jnp.concatenate([jnp.zeros((d_inner,), F32), 0.5 * ssd_conv_b[0]])
    d_exp = jnp.repeat(ssd_d[0], SSD_HEADDIM)
    w_out0 = ssd_w_out[0].astype(BF16)
    gw = (heads // SSD_GROUPS) * SSD_HEADDIM

    (sh_x, sh_c), (sc_x, sc_c), (gt_x, gt_c) = mod(0, 0), mod(0, 1), mod(0, 2)
    r3 = lambda a: a[:, None, :]
    dt_bias = ssd_dt_bias[0].reshape(-1)
    a_log = ssd_a_log[0].reshape(-1)

    def ssd_mixer(tok3, shift, scale, gate, h0):
        b_, t_, _ = tok3.shape
        zx, dtc, dtcT = _ssd_inproj(tok3, norm_mix[0], r3(shift), r3(scale), w_main, conv_w_full,
                                    conv_b_full, w_dt, dt_bias, a_log, d_inner=d_inner, heads=heads)
        y, hfin = _ssd_scan(zx, dtc, dtcT, h0, d_inner=d_inner, heads=heads)
        out = _ssd_out(y.reshape(2, b_ * t_, d_inner), zx.reshape(b_ * t_, n_main), d_exp, ssd_norm[0],
                       w_out0, tok3.reshape(b_ * t_, d), r3(gate), seq=t_, d_inner=d_inner)
        return out, hfin

    h_zero = jnp.zeros((bsz, 2, SSD_GROUPS, SSD_STATE, gw), F32)
    ctx2, h_ctx = ssd_mixer(ctx, sh_c, sc_c, gt_c, h_zero)
    x2, _ = ssd_mixer(x, sh_x, sc_x, gt_x, h_ctx)

    ffn_in = ffn_w_in[0].astype(BF16)
    ffn_out = ffn_w_out[0].astype(BF16)
    (sh_x, sh_c), (sc_x, sc_c), (gt_x, gt_c) = mod(0, 3), mod(0, 4), mod(0, 5)
    xt = _ffn(x2.reshape(bsz, seq, d), norm_ffn[0], r3(sh_x), r3(sc_x), r3(gt_x), ffn_in, ffn_out,
              time_major=True).reshape(seq * bsz, d)
    ctx2 = _ffn(ctx2.reshape(bsz, ctx_len, d), norm_ffn[0], r3(sh_c), r3(sc_c), r3(gt_c), ffn_in, ffn_out)

    ct = ctx2.reshape(bsz, ctx_len, d).transpose(1, 0, 2).reshape(ctx_len * bsz, d)

    ar, ai, bbr, bbi = _s5_discretize(s5_lam_re[0], s5_lam_im[0], s5_log_step[0], s5_b_re[0], s5_b_im[0])
    n_groups = ar.shape[1]
    n_gb = n_groups // S5_GB
    wb = jnp.concatenate([_blockdiag(bbr.transpose(0, 1, 3, 2), n_gb),
                          _blockdiag(bbi.transpose(0, 1, 3, 2), n_gb)], axis=-1).astype(BF16)
    wc = jnp.concatenate([_blockdiag(s5_c_re[0].transpose(0, 1, 3, 2), n_gb),
                          _blockdiag(-s5_c_im[0].transpose(0, 1, 3, 2), n_gb)], axis=2).astype(BF16)
    a_coef = jnp.stack([ar.reshape(2, -1), ai.reshape(2, -1)], axis=1)
    nstate = a_coef.shape[-1]

    (sh_x, sh_c), (sc_x, sc_c), (gt_x, _) = mod(1, 0), mod(1, 1), mod(1, 2)
    bc = lambda a: jnp.broadcast_to(a, (bsz, d))
    s_zero = jnp.zeros((2, 2, bsz, nstate), F32)
    _, s_ctx = _s5_scan(ct, norm_mix[1], bc(sh_c), bc(sc_c), wb, wc, a_coef, s_zero, bsz=bsz)
    n_exp, _, two_f = moe_w_in.shape[1:]
    y, _, w_in_b, w_out_b = _s5_scan(xt, norm_mix[1], sh_x, sc_x, wb, wc, a_coef, s_ctx, bsz=bsz,
                                     cast=(moe_w_in[0].reshape(n_exp * d, two_f),
                                           moe_w_out[0].reshape(n_exp * (two_f // 2), d)))
    w_in_b = w_in_b.reshape(n_exp, d, two_f)
    w_out_b = w_out_b.reshape(n_exp, two_f // 2, d)
    xt = _s5_glu(xt, y, norm_mix[1], sh_x, sc_x, gt_x, s5_d[0], s5_w_glu[0].astype(BF16), s5_b_glu[0], bsz=bsz)

    (sh_x, _), (sc_x, _), (gt_x, _) = mod(1, 3), mod(1, 4), mod(1, 5)
    return _moe(xt, norm_ffn[1], sh_x, sc_x, gt_x, moe_router_w[0], moe_router_b[0],
                w_in_b, w_out_b, norm_final, bsz=bsz, grid_rows=rows)
```

```python
import functools

import jax
import jax.numpy as jnp
from jax import lax
from jax.experimental import pallas as pl
from jax.experimental.pallas import tpu as pltpu

F32 = jnp.float32
BF16 = jnp.bfloat16
HIGHEST = lax.Precision.HIGHEST

EPS = 1e-6
GRID_W = 64
N_MOD = 6

SSD_HEADDIM = 64
SSD_GROUPS = 4
SSD_STATE = 128
SSD_CONV = 5
SSD_CHUNK = 128

S5_CH = 16
S5_STATE = 64
S5_GB = 8

TOP_K = 2

V7X_VMEM_BYTES = 64 * 1024 * 1024
VMEM_LIMIT = V7X_VMEM_BYTES - 4 * 1024 * 1024
NEG_BIG = -1e30
LANES = 128
LOG2E = 1.4426950408889634


def _fill_lane_tiles(res, tr_scr):
    for lt in range(res.shape[1] // LANES):
        tr_scr[lt] = res[:, lt * LANES:(lt + 1) * LANES]


def _strided_rows_out(tr_scr, dst, *, n_groups, group_rows, stride, base=0):
    for g in range(n_groups):
        for lt in range(tr_scr.shape[0]):
            dst(g, lt)[...] = tr_scr[lt, pl.ds(base + g, group_rows, stride=stride), :]


def _params(*sem):
    return pltpu.CompilerParams(dimension_semantics=sem, vmem_limit_bytes=VMEM_LIMIT)


def _silu(v):
    return v * jax.nn.sigmoid(v)


def _softplus(v):
    return jnp.maximum(v, 0.0) + jnp.log1p(jnp.exp(-jnp.abs(v)))


def _rms(x, g):
    inv = lax.rsqrt(jnp.mean(x * x, axis=-1, keepdims=True) + EPS)
    return (x * inv) * g


def _nt_dot(a, b, precision=None):
    return lax.dot_general(a, b, (((1,), (1,)), ((), ())), precision=precision,
                           preferred_element_type=F32)


def _ada_kernel(c_ref, w_ref, b_ref, o_ref):
    s = _silu(c_ref[...])
    o_ref[0] = jnp.dot(s, w_ref[0], precision=HIGHEST, preferred_element_type=F32) + b_ref[0]


def _ada(cc, ada_w, ada_b):
    depth, d, n = ada_w.shape
    r = cc.shape[0]
    tn = 1536
    return pl.pallas_call(
        _ada_kernel,
        grid=(depth, n // tn),
        in_specs=[pl.BlockSpec((r, d), lambda i, j: (0, 0)),
                  pl.BlockSpec((1, d, tn), lambda i, j: (i, 0, j)),
                  pl.BlockSpec((1, 1, tn), lambda i, j: (i, 0, j))],
        out_specs=pl.BlockSpec((1, r, tn), lambda i, j: (i, 0, j)),
        out_shape=jax.ShapeDtypeStruct((depth, r, n), F32),
        compiler_params=_params("arbitrary", "arbitrary"),
        name="ada_mod",
    )(cc, ada_w, ada_b.reshape(depth, 1, n))


def _ssd_inproj_kernel(x_ref, g_ref, sh_ref, sc_ref, w_ref, cw_ref, cb_ref, wdt_ref, wdtT_ref,
                       bdt_ref, bdtT_ref, alog_ref, alogT_ref,
                       o_ref, dt_ref, dtT_ref, h_scr, pad_scr, *, n_plain, seq, heads, row_chunk, dt_chunk):
    j = pl.program_id(1)
    pad = 8

    @pl.when(j == 0)
    def _():
        for r in range(0, seq, dt_chunk):
            rs = slice(r, r + dt_chunk)
            h = _rms(x_ref[0, rs, :], g_ref[...]) * (1.0 + sc_ref[0]) + sh_ref[0]
            hb = h.astype(BF16)
            h_scr[rs, :] = hb
            hl = (h - hb.astype(F32)).astype(BF16)
            nh2 = 2 * heads
            p = jnp.dot(hb, wdt_ref[...], preferred_element_type=F32)
            q = jnp.dot(hl, wdt_ref[...], preferred_element_type=F32)
            dt = _softplus(p[:, :nh2] + (p[:, nh2:] + q[:, :nh2]) + bdt_ref[...])
            da = dt * (-jnp.exp(alog_ref[...]))
            pT = _nt_dot(wdtT_ref[...], hb)
            qT = _nt_dot(wdtT_ref[...], hl)
            dtT = _softplus(pT[:nh2, :] + (pT[nh2:, :] + qT[:nh2, :]) + bdtT_ref[...])
            daT = dtT * (-jnp.exp(alogT_ref[...]))
            for d in range(2):
                sl = slice(d * heads, (d + 1) * heads)
                dt_ref[d, 0, rs, :] = jnp.concatenate([dt[:, sl], da[:, sl]], axis=-1)
                dtT_ref[d, 0, :, rs] = jnp.concatenate([dtT[sl, :], daT[sl, :]], axis=0)
        zeros = jnp.zeros((pad, pad_scr.shape[1]), F32)
        pad_scr[0:pad, :] = zeros
        pad_scr[pad + seq:2 * pad + seq, :] = zeros

    acc = jnp.dot(h_scr[...], w_ref[...], preferred_element_type=F32)

    @pl.when(j < n_plain)
    def _():
        o_ref[0] = acc.astype(o_ref.dtype)

    @pl.when(j >= n_plain)
    def _():
        pad_scr[pad:pad + seq, :] = acc
        half = (SSD_CONV - 1) // 2
        n_all = row_chunk + 2 * pad
        for r in range(0, seq, row_chunk):
            a = pad_scr[r:r + n_all, :]
            s = cb_ref[...] + cw_ref[half:half + 1, :] * a[pad:pad + row_chunk, :]
            for k in range(SSD_CONV):
                if k != half:
                    shifted = pltpu.roll(a, (half - k) % n_all, axis=0)
                    s = s + cw_ref[k:k + 1, :] * shifted[pad:pad + row_chunk, :]
            o_ref[0, r:r + row_chunk, :] = (s * (1.0 + jnp.tanh(s))).astype(o_ref.dtype)


def _ssd_inproj(x3, g, shift, scale, w_main, conv_w_full, conv_b_full, w_dt, dt_bias, a_log,
                *, d_inner, heads):
    bsz, seq, d = x3.shape
    n = w_main.shape[1]
    tn = 512
    nh2 = 2 * heads
    per_batch = shift.shape[0] > 1
    mod_map = (lambda b, j: (b, 0, 0)) if per_batch else (lambda b, j: (0, 0, 0))
    kern = functools.partial(_ssd_inproj_kernel, n_plain=d_inner // tn, seq=seq, heads=heads,
                             row_chunk=min(64, seq), dt_chunk=min(256, seq))
    w_hi = w_dt.astype(BF16)
    w_dt2 = jnp.concatenate([w_hi, (w_dt - w_hi.astype(F32)).astype(BF16)], axis=1)
    const2 = lambda b, j: (0, 0)
    return pl.pallas_call(
        kern,
        grid=(bsz, n // tn),
        in_specs=[pl.BlockSpec((1, seq, d), lambda b, j: (b, 0, 0), pipeline_mode=pl.Buffered(1)),
                  pl.BlockSpec((1, d), const2),
                  pl.BlockSpec((1, 1, d), mod_map),
                  pl.BlockSpec((1, 1, d), mod_map),
                  pl.BlockSpec((d, tn), lambda b, j: (0, j)),
                  pl.BlockSpec((SSD_CONV, tn), lambda b, j: (0, j)),
                  pl.BlockSpec((1, tn), lambda b, j: (0, j)),
                  pl.BlockSpec((d, 2 * nh2), const2),
                  pl.BlockSpec((2 * nh2, d), const2),
                  pl.BlockSpec((1, nh2), const2),
                  pl.BlockSpec((nh2, 1), const2),
                  pl.BlockSpec((1, nh2), const2),
                  pl.BlockSpec((nh2, 1), const2)],
        out_specs=[pl.BlockSpec((1, seq, tn), lambda b, j: (b, 0, j)),
                   pl.BlockSpec((2, 1, seq, nh2), lambda b, j: (0, b, 0, 0)),
                   pl.BlockSpec((2, 1, nh2, seq), lambda b, j: (0, b, 0, 0))],
        out_shape=[jax.ShapeDtypeStruct((bsz, seq, n), BF16),
                   jax.ShapeDtypeStruct((2, bsz, seq, nh2), F32),
                   jax.ShapeDtypeStruct((2, bsz, nh2, seq), F32)],
        scratch_shapes=[pltpu.VMEM((seq, d), BF16), pltpu.VMEM((seq + 16, tn), F32)],
        compiler_params=_params("arbitrary", "arbitrary"),
        name="ssd_inproj",
    )(x3, g.reshape(1, d), shift, scale, w_main, conv_w_full, conv_b_full.reshape(1, n),
      w_dt2, w_dt2.T, dt_bias.reshape(1, nh2), dt_bias.reshape(nh2, 1),
      a_log.reshape(1, nh2), a_log.reshape(nh2, 1))


def _ssd_scan_kernel(xs_ref, b_ref, c_ref, dtc_ref, dtcT_ref, h0_ref, y_ref, hfin_ref,
                     st_scr, *, heads, n_chunks):
    d = pl.program_id(1)
    k = pl.program_id(2)
    L = xs_ref.shape[1]
    hpg = heads // SSD_GROUPS
    pairs_per_group = hpg // 2
    gw = hpg * SSD_HEADDIM

    @pl.when(k == 0)
    def _():
        st_scr[...] = h0_ref[0, 0]

    row = lax.broadcasted_iota(jnp.int32, (L, L), 0)
    col = lax.broadcasted_iota(jnp.int32, (L, L), 1)
    mask = jnp.where(d == 0, row - col, col - row) >= 0
    mask_f = mask.astype(F32)

    dtc = dtc_ref[0, 0]
    dtcT = dtcT_ref[0, 0]
    assert L == 2 * SSD_HEADDIM, "the head-pair tiles assume chunk length == 2 * head dim"
    da = dtc[:, heads:]
    dtT, daT = dtcT[:heads, :], dtcT[heads:, :]
    cum = jnp.dot(mask_f, da, precision=HIGHEST, preferred_element_type=F32)
    cumT = _nt_dot(daT, mask_f, precision=HIGHEST)
    tot = jnp.sum(da, axis=0, keepdims=True)
    totT = jnp.sum(daT, axis=1, keepdims=True)
    etot = jnp.exp(tot)
    c2 = cum * LOG2E
    c2T = cumT * LOG2E
    wrow = jnp.exp(totT - cumT) * dtT

    lane = lax.broadcasted_iota(jnp.int32, (L, 2 * SSD_HEADDIM), 1)
    lo = lane < SSD_HEADDIM
    lo_row = lax.broadcasted_iota(jnp.int32, (1, 2 * SSD_HEADDIM), 1) < SSD_HEADDIM

    for g in range(SSD_GROUPS):
        bg = b_ref[0, :, g * SSD_STATE:(g + 1) * SSD_STATE]
        cg = c_ref[0, :, g * SSD_STATE:(g + 1) * SSD_STATE]
        scores = _nt_dot(cg, bg) * mask_f
        bgT = bg.astype(F32).T
        yoff = jnp.dot(cg, st_scr[g].astype(BF16), preferred_element_type=F32)
        for q in range(pairs_per_group):
            p = g * pairs_per_group + q
            h0, h1 = 2 * p, 2 * p + 1
            xs2 = xs_ref[0, :, p * 128:(p + 1) * 128]
            zero = jnp.zeros_like(xs2)
            rhs = jnp.concatenate([jnp.where(lo, xs2, zero), jnp.where(lo, zero, xs2)], axis=0)
            ms, bs, es = [], [], []
            for h in (h0, h1):
                col = jnp.broadcast_to(c2[:, h:h + 1], (L, L))
                dec = jnp.exp2(jnp.minimum(col - c2T[h:h + 1, :], 0.0))
                ms.append((scores * dec * dtT[h:h + 1, :]).astype(BF16))
                bs.append((bgT * wrow[h:h + 1, :]).astype(BF16))
                es.append(jnp.exp2(col))
            ydiag = jnp.dot(jnp.concatenate(ms, axis=1), rhs, preferred_element_type=F32)
            new = jnp.dot(jnp.concatenate(bs, axis=1), rhs, preferred_element_type=F32)
            sl = slice(q * 128, (q + 1) * 128)
            y2 = ydiag + yoff[:, sl] * jnp.where(lo, es[0], es[1])
            y_ref[0, 0, :, p * 128:(p + 1) * 128] = y2.astype(y_ref.dtype)
            et2 = jnp.where(lo_row, etot[:, h0:h0 + 1], etot[:, h1:h1 + 1])
            st_scr[g, :, sl] = st_scr[g, :, sl] * et2 + new

    @pl.when(k == n_chunks - 1)
    def _():
        hfin_ref[0, 0] = st_scr[...]


def _ssd_scan(zx, dtc, dtcT, h0, *, d_inner, heads):
    bsz, seq, _ = zx.shape
    L = min(SSD_CHUNK, seq)
    nc = seq // L
    bc_w = SSD_GROUPS * SSD_STATE
    gw = (heads // SSD_GROUPS) * SSD_HEADDIM
    nh2 = 2 * heads
    xs_blk = d_inner // d_inner
    b_blk = (2 * d_inner) // bc_w
    c_blk = b_blk + 1

    def chunk(d, k):
        return k + d * (nc - 1 - 2 * k)

    kern = functools.partial(_ssd_scan_kernel, heads=heads, n_chunks=nc)
    st_shape = (SSD_GROUPS, SSD_STATE, gw)
    return pl.pallas_call(
        kern,
        grid=(bsz, 2, nc),
        in_specs=[pl.BlockSpec((1, L, d_inner), lambda b, d, k: (b, chunk(d, k), xs_blk)),
                  pl.BlockSpec((1, L, bc_w), lambda b, d, k: (b, chunk(d, k), b_blk)),
                  pl.BlockSpec((1, L, bc_w), lambda b, d, k: (b, chunk(d, k), c_blk)),
                  pl.BlockSpec((1, 1, L, nh2), lambda b, d, k: (d, b, chunk(d, k), 0)),
                  pl.BlockSpec((1, 1, nh2, L), lambda b, d, k: (d, b, 0, chunk(d, k))),
                  pl.BlockSpec((1, 1) + st_shape, lambda b, d, k: (b, d, 0, 0, 0))],
        out_specs=[pl.BlockSpec((1, 1, L, d_inner), lambda b, d, k: (d, b, chunk(d, k), 0)),
                   pl.BlockSpec((1, 1) + st_shape, lambda b, d, k: (b, d, 0, 0, 0))],
        out_shape=[jax.ShapeDtypeStruct((2, bsz, seq, d_inner), BF16),
                   jax.ShapeDtypeStruct((bsz, 2) + st_shape, F32)],
        scratch_shapes=[pltpu.VMEM(st_shape, F32)],
        compiler_params=_params("arbitrary", "arbitrary", "arbitrary"),
        name="ssd_scan",
    )(zx, zx, zx, dtc, dtcT, h0)


def _ssd_out_kernel(y_ref, z_ref, xs_ref, dsk_ref, nw_ref, w_ref, x_ref, gate_ref, o_ref):
    y = y_ref[0].astype(F32) + y_ref[1].astype(F32) + dsk_ref[...] * xs_ref[...].astype(F32)
    yg = y * _silu(z_ref[...].astype(F32))
    yn = _rms(yg, nw_ref[...]).astype(BF16)
    out = jnp.dot(yn, w_ref[...], preferred_element_type=F32)
    o_ref[...] = x_ref[...] + gate_ref[0] * out


def _ssd_out(y, zx, d_exp, norm_w, w_out, x2, gate, *, seq, d_inner):
    m, d = x2.shape
    tm = min(512, seq)
    bpt = seq // tm
    per_batch = gate.shape[0] > 1
    gate_map = (lambda i: (i // bpt, 0, 0)) if per_batch else (lambda i: (0, 0, 0))
    return pl.pallas_call(
        _ssd_out_kernel,
        grid=(m // tm,),
        in_specs=[pl.BlockSpec((2, tm, d_inner), lambda i: (0, i, 0)),
                  pl.BlockSpec((tm, d_inner), lambda i: (i, 0)),
                  pl.BlockSpec((tm, d_inner), lambda i: (i, 1)),
                  pl.BlockSpec((1, d_inner), lambda i: (0, 0)),
                  pl.BlockSpec((1, d_inner), lambda i: (0, 0)),
                  pl.BlockSpec((d_inner, d), lambda i: (0, 0)),
                  pl.BlockSpec((tm, d), lambda i: (i, 0)),
                  pl.BlockSpec((1, 1, d), gate_map)],
        out_specs=pl.BlockSpec((tm, d), lambda i: (i, 0)),
        out_shape=jax.ShapeDtypeStruct((m, d), F32),
        compiler_params=_params("arbitrary"),
        name="ssd_out",
    )(y, zx, zx, d_exp.reshape(1, d_inner), norm_w.reshape(1, d_inner), w_out, x2, gate)


def _ffn_kernel(x_ref, g_ref, sh_ref, sc_ref, gate_ref, wg_ref, wu_ref, wo_ref, o_ref,
                h_scr, acc_scr, tr_scr, *, n_f, time_major):
    j = pl.program_id(1)
    bb, tt, dm = x_ref.shape

    @pl.when(j == 0)
    def _():
        h = _rms(x_ref[...], g_ref[...]) * (1.0 + sc_ref[...]) + sh_ref[...]
        h_scr[...] = h.reshape(bb * tt, dm).astype(BF16)
        acc_scr[...] = jnp.zeros_like(acc_scr)

    h = h_scr[...]
    gv = jnp.dot(h, wg_ref[...], preferred_element_type=F32)
    uv = jnp.dot(h, wu_ref[...], preferred_element_type=F32)
    a = (_silu(gv) * uv).astype(BF16)
    acc_scr[...] += jnp.dot(a, wo_ref[...], preferred_element_type=F32)

    @pl.when(j == n_f - 1)
    def _():
        res = x_ref[...] + gate_ref[...] * acc_scr[...].reshape(bb, tt, dm)
        if time_major:
            _fill_lane_tiles(res.reshape(bb * tt, dm), tr_scr)
            _strided_rows_out(tr_scr, lambda c, lt: o_ref.at[c, 0, :, lt * LANES:(lt + 1) * LANES],
                              n_groups=tt, group_rows=bb, stride=tt)
        else:
            o_ref[...] = res


def _ffn(x3, g, shift, scale, gate, w_in, w_out, *, time_major=False):
    bsz, seq, d = x3.shape
    f = w_out.shape[0]
    tf = f // 2 if (f // 2) % 128 == 0 else f
    n_f = f // tf
    per_batch = shift.shape[0] > 1
    if time_major:
        bb, tt = 8, GRID_W
        assert bsz % bb == 0 and seq % tt == 0 and per_batch
        rows = seq // tt
        x_map = lambda i, j: (i // rows, i % rows, 0)
        mod_map = lambda i, j: (i // rows, 0, 0)
        out_spec = pl.BlockSpec((tt, 1, bb, d), lambda i, j: (0, i % rows, i // rows, 0))
        out_shape = jax.ShapeDtypeStruct((tt, rows, bsz, d), F32)
        n_blocks = (bsz // bb) * rows
    else:
        bb, tt = 1, min(512, seq)
        bpt = seq // tt
        x_map = lambda i, j: (i // bpt, i % bpt, 0)
        mod_map = (lambda i, j: (i // bpt, 0, 0)) if per_batch else (lambda i, j: (0, 0, 0))
        out_spec = pl.BlockSpec((bb, tt, d), x_map)
        out_shape = jax.ShapeDtypeStruct((bsz, seq, d), F32)
        n_blocks = bsz * bpt
    mb = bb if per_batch else 1
    tm = bb * tt
    kern = functools.partial(_ffn_kernel, n_f=n_f, time_major=time_major)
    return pl.pallas_call(
        kern,
        grid=(n_blocks, n_f),
        in_specs=[pl.BlockSpec((bb, tt, d), x_map),
                  pl.BlockSpec((1, d), lambda i, j: (0, 0)),
                  pl.BlockSpec((mb, 1, d), mod_map),
                  pl.BlockSpec((mb, 1, d), mod_map),
                  pl.BlockSpec((mb, 1, d), mod_map),
                  pl.BlockSpec((d, tf), lambda i, j: (0, j)),
                  pl.BlockSpec((d, tf), lambda i, j: (0, n_f + j)),
                  pl.BlockSpec((tf, d), lambda i, j: (j, 0))],
        out_specs=out_spec,
        out_shape=out_shape,
        scratch_shapes=[pltpu.VMEM((tm, d), BF16), pltpu.VMEM((tm, d), F32),
                        pltpu.VMEM((d // LANES, tm if time_major else 8, LANES), F32)],
        compiler_params=_params("arbitrary", "arbitrary"),
        name="dense_swiglu",
    )(x3, g.reshape(1, d), shift, scale, gate, w_in, w_in, w_out)


def _s5_disc_kernel(lre_ref, lim_ref, step_ref, bre_ref, bim_ref, ar_ref, ai_ref, bbr_ref, bbi_ref):
    step = jnp.exp(step_ref[...])
    lre, lim = lre_ref[...], lim_ref[...]
    mag = jnp.exp(lre * step)
    ar = mag * jnp.cos(lim * step)
    ai = mag * jnp.sin(lim * step)
    den = lre * lre + lim * lim
    cr = ((ar - 1.0) * lre + ai * lim) / den
    ci = (ai * lre - (ar - 1.0) * lim) / den
    ar_ref[...] = ar
    ai_ref[...] = ai
    bbr_ref[...] = cr * bre_ref[...] - ci * bim_ref[...]
    bbi_ref[...] = cr * bim_ref[...] + ci * bre_ref[...]


def _s5_discretize(lam_re, lam_im, log_step, b_re, b_im):
    two, g, p = lam_re.shape
    c = b_re.shape[-1]
    lre = jnp.repeat(lam_re, c, axis=-1)
    lim = jnp.repeat(lam_im, c, axis=-1)
    shp = jax.ShapeDtypeStruct((two, g, p * c), F32)
    ar, ai, bbr, bbi = pl.pallas_call(
        _s5_disc_kernel, out_shape=[shp, shp, shp, shp], name="s5_discretize",
    )(lre, lim, log_step.reshape(two, g, 1), b_re.reshape(two, g, p * c), b_im.reshape(two, g, p * c))
    ar = ar.reshape(two, g, p, c)[..., 0]
    ai = ai.reshape(two, g, p, c)[..., 0]
    return ar, ai, bbr.reshape(two, g, p, c), bbi.reshape(two, g, p, c)


def _s5_kernel(x_ref, g_ref, sh_ref, sc_ref, wb_ref, wc_ref, a_ref, s0_ref, *rest, n_chunks, steps, bsz, n_cast):
    cast_in, (y_ref, sfin_ref), cast_out = rest[:n_cast], rest[n_cast:n_cast + 2], rest[n_cast + 2:2 * n_cast + 2]
    st_scr, bu0_scr, bu1_scr, sb0_scr, sb1_scr = rest[2 * n_cast + 2:]
    for src, dst in zip(cast_in, cast_out):
        dst[...] = src[...].astype(dst.dtype)
    bu_bufs = (bu0_scr, bu1_scr)
    sb_bufs = (sb0_scr, sb1_scr)
    d = pl.program_id(0)
    k = pl.program_id(1)
    rows, dm = x_ref.shape
    gl = S5_GB * S5_STATE
    cb = S5_GB * S5_CH

    @pl.when(k == 0)
    def _():
        st_scr[...] = s0_ref[0]

    hn = _rms(x_ref[...], g_ref[...]).reshape(steps, bsz, dm)
    h = hn * (1.0 + sc_ref[...])[None] + sh_ref[...][None]
    u = h.reshape(rows, dm).astype(BF16)

    n_gb = dm // cb

    def bu_dot(gb):
        return jnp.dot(u[:, gb * cb:(gb + 1) * cb], wb_ref[0, gb], preferred_element_type=F32)

    bu_bufs[0][...] = bu_dot(0)
    for gb in range(n_gb):
        bu_cur, bu_nxt = bu_bufs[gb % 2], bu_bufs[1 - gb % 2]
        sb_cur = sb_bufs[gb % 2]
        if gb + 1 < n_gb:
            bu_nxt[...] = bu_dot(gb + 1)
        sl = slice(gb * gl, (gb + 1) * gl)
        ar = jnp.broadcast_to(a_ref[0, 0:1, sl], (bsz, gl))
        ai = jnp.broadcast_to(a_ref[0, 1:2, sl], (bsz, gl))
        sr, si = st_scr[0, :, sl], st_scr[1, :, sl]
        for i in range(steps):
            t = jnp.where(d == 0, i, steps - 1 - i)
            r0 = pl.multiple_of(t * bsz, bsz)
            bur = bu_cur[pl.ds(r0, bsz), 0:gl]
            bui = bu_cur[pl.ds(r0, bsz), gl:2 * gl]
            sr, si = ar * sr - ai * si + bur, ar * si + ai * sr + bui
            sb_cur[pl.ds(r0, bsz), 0:gl] = sr.astype(BF16)
            sb_cur[pl.ds(r0, bsz), gl:2 * gl] = si.astype(BF16)
        st_scr[0, :, sl] = sr
        st_scr[1, :, sl] = si
        yv = jnp.dot(sb_cur[...], wc_ref[0, gb], preferred_element_type=F32)
        y_ref[0, :, gb * cb:(gb + 1) * cb] = yv.astype(y_ref.dtype)

    @pl.when(k == n_chunks - 1)
    def _():
        sfin_ref[0] = st_scr[...]


def _s5_scan(xt, g, shift, scale, wb, wc, a, s0, *, bsz, cast=()):
    m, d = xt.shape
    seq = m // bsz
    steps = min(64, seq)
    rows = steps * bsz
    nc = seq // steps
    nstate = s0.shape[-1]
    n_gb = d // (S5_GB * S5_CH)
    gl = S5_GB * S5_STATE

    def chunk(dd, k):
        return k + dd * (nc - 1 - 2 * k)

    n_steps = 2 * nc
    cast_specs, cast_shapes = [], []
    for w in cast:
        slab = w.shape[0] // n_steps
        assert slab * n_steps == w.shape[0] and slab % 16 == 0
        cast_specs.append(pl.BlockSpec((slab, w.shape[1]), lambda dd, k: (dd * nc + k, 0)))
        cast_shapes.append(jax.ShapeDtypeStruct(w.shape, BF16))

    kern = functools.partial(_s5_kernel, n_chunks=nc, steps=steps, bsz=bsz, n_cast=len(cast))
    return pl.pallas_call(
        kern,
        grid=(2, nc),
        in_specs=[pl.BlockSpec((rows, d), lambda dd, k: (chunk(dd, k), 0)),
                  pl.BlockSpec((1, d), lambda dd, k: (0, 0)),
                  pl.BlockSpec((bsz, d), lambda dd, k: (0, 0)),
                  pl.BlockSpec((bsz, d), lambda dd, k: (0, 0)),
                  pl.BlockSpec((1, n_gb, S5_GB * S5_CH, 2 * gl), lambda dd, k: (dd, 0, 0, 0)),
                  pl.BlockSpec((1, n_gb, 2 * gl, S5_GB * S5_CH), lambda dd, k: (dd, 0, 0, 0)),
                  pl.BlockSpec((1, 2, nstate), lambda dd, k: (dd, 0, 0)),
                  pl.BlockSpec((1, 2, bsz, nstate), lambda dd, k: (dd, 0, 0, 0))] + cast_specs,
        out_specs=[pl.BlockSpec((1, rows, d), lambda dd, k: (dd, chunk(dd, k), 0)),
                   pl.BlockSpec((1, 2, bsz, nstate), lambda dd, k: (dd, 0, 0, 0))] + cast_specs,
        out_shape=[jax.ShapeDtypeStruct((2, m, d), BF16),
                   jax.ShapeDtypeStruct((2, 2, bsz, nstate), F32)] + cast_shapes,
        scratch_shapes=[pltpu.VMEM((2, bsz, nstate), F32),
                        pltpu.VMEM((rows, 2 * gl), F32), pltpu.VMEM((rows, 2 * gl), F32),
                        pltpu.VMEM((rows, 2 * gl), BF16), pltpu.VMEM((rows, 2 * gl), BF16)],
        compiler_params=_params("arbitrary", "arbitrary"),
        name="s5_scan",
    )(xt, g.reshape(1, d), shift, scale, wb, wc, a, s0, *cast)


def _gelu_tanh(v):
    return 0.5 * v * (1.0 + jnp.tanh(0.7978845608028654 * (v + 0.044715 * (v * v * v))))


def _s5_glu_kernel(x_ref, y_ref, g_ref, sh_ref, sc_ref, gate_ref, dsk_ref, w_ref, b_ref, o_ref, *, bsz):
    rows, dm = x_ref.shape
    x = x_ref[...]
    hn = _rms(x, g_ref[...]).reshape(rows // bsz, bsz, dm)
    h = (hn * (1.0 + sc_ref[...])[None] + sh_ref[...][None]).reshape(rows, dm)
    yv = dsk_ref[...] * h + y_ref[0].astype(F32) + y_ref[1].astype(F32)
    ge = _gelu_tanh(yv).astype(BF16)
    o = jnp.dot(ge, w_ref[...], preferred_element_type=F32) + b_ref[...]
    out = o[:, :dm] * jax.nn.sigmoid(o[:, dm:])
    res = x.reshape(rows // bsz, bsz, dm) + gate_ref[...][None] * out.reshape(rows // bsz, bsz, dm)
    o_ref[...] = res.reshape(rows, dm)


def _s5_glu(xt, y, g, shift, scale, gate, d_skip, w_glu, b_glu, *, bsz):
    m, d = xt.shape
    tm = min(512, m)
    kern = functools.partial(_s5_glu_kernel, bsz=bsz)
    full = lambda i: (0, 0)
    return pl.pallas_call(
        kern,
        grid=(m // tm,),
        in_specs=[pl.BlockSpec((tm, d), lambda i: (i, 0)),
                  pl.BlockSpec((2, tm, d), lambda i: (0, i, 0)),
                  pl.BlockSpec((1, d), full),
                  pl.BlockSpec((bsz, d), full),
                  pl.BlockSpec((bsz, d), full),
                  pl.BlockSpec((bsz, d), full),
                  pl.BlockSpec((1, d), full),
                  pl.BlockSpec((d, 2 * d), full),
                  pl.BlockSpec((1, 2 * d), full)],
        out_specs=pl.BlockSpec((tm, d), lambda i: (i, 0)),
        out_shape=jax.ShapeDtypeStruct((m, d), F32),
        compiler_params=_params("arbitrary"),
        name="s5_glu",
    )(xt, y, g.reshape(1, d), shift, scale, gate, d_skip.reshape(1, d), w_glu, b_glu.reshape(1, 2 * d))


MOE_TB = 512
MOE_TILE = 512
MOE_WIN = 256
MOE_PAD = 16
MOE_LANES = 128


def _moe_cap(n_tok, n_blocks):
    rows = n_tok + n_blocks * MOE_PAD + MOE_TILE
    return -(-rows // MOE_TILE) * MOE_TILE


def _moe_route_kernel(x_ref, g_ref, sh_ref, sc_ref, rwT_ref, rbT_ref, ut_ref,
                      sorted_ref, meta_ref, tab_ref,
                      buf_scr, zero_scr, run_smem, sems, *, bsz, n_exp, n_blocks, local_rows):
    b = pl.program_id(0)
    slot = lax.rem(b, 2)
    dm = x_ref.shape[-1]
    rows = x_ref.shape[0] * x_ref.shape[1] * x_ref.shape[2]

    @pl.when(b == 0)
    def _():
        for e in range(n_exp):
            run_smem[e] = 0
        zero_scr[...] = jnp.zeros_like(zero_scr)

    hn = _rms(x_ref[...].reshape(rows // bsz, bsz, dm), g_ref[...])
    h = (hn * (1.0 + sc_ref[...])[None] + sh_ref[...][None]).reshape(rows, dm)
    hb = h.astype(BF16)

    logits = _nt_dot(rwT_ref[...], h, precision=HIGHEST) + rbT_ref[...]
    sub = lax.broadcasted_iota(jnp.int32, logits.shape, 0)
    m1 = jnp.max(logits, axis=0, keepdims=True)
    i1 = jnp.min(jnp.where(logits == m1, sub, n_exp), axis=0, keepdims=True)
    rest = jnp.where(sub == i1, -jnp.inf, logits)
    m2 = jnp.max(rest, axis=0, keepdims=True)
    i2 = jnp.min(jnp.where(rest == m2, sub, n_exp), axis=0, keepdims=True)
    e2 = jnp.exp(m2 - m1)
    g1 = 1.0 / (1.0 + e2)
    g2 = e2 * g1
    sel1 = sub == i1
    sel2 = sub == i2
    oh = jnp.where(sel1, 1.0, 0.0) + jnp.where(sel2, 1.0, 0.0)
    cnt = jnp.sum(oh, axis=1, keepdims=True).astype(jnp.int32)
    cpad_v = ((cnt + (MOE_PAD - 1)) // MOE_PAD) * MOE_PAD
    rank = jnp.dot(oh.astype(BF16), ut_ref[...], preferred_element_type=F32)

    cpad, loc, run = [], [], []
    off = 0
    for e in range(n_exp):
        cpad.append(cpad_v[e, 0])
        loc.append(off)
        off = off + cpad[e]
        run.append(run_smem[e])
    sub1 = lax.broadcasted_iota(jnp.int32, (n_exp, 1), 0)
    loc_v = jnp.zeros((n_exp, 1), jnp.int32)
    run_v = jnp.zeros((n_exp, 1), jnp.int32)
    for e in range(n_exp):
        loc_v = jnp.where(sub1 == e, loc[e], loc_v)
        run_v = jnp.where(sub1 == e, run[e], run_v)
    loc_v = loc_v.astype(F32)
    run_v = run_v.astype(F32)

    def pick(sel, v):
        return jnp.sum(jnp.where(sel, v, 0.0), axis=0, keepdims=True)

    lp1 = pick(sel1, loc_v + rank).astype(jnp.int32)
    lp2 = pick(sel2, loc_v + rank).astype(jnp.int32)
    pos1 = pick(sel1, run_v + rank)
    pos2 = pick(sel2, run_v + rank)

    r = lax.broadcasted_iota(jnp.int32, (local_rows, rows), 0)
    onehot = jnp.where(r == lp1, 1.0, jnp.where(r == lp2, 1.0, 0.0)).astype(BF16)
    buf_scr[slot, 0:local_rows, :] = jnp.dot(onehot, hb, preferred_element_type=F32).astype(BF16)
    buf_scr[slot, local_rows:local_rows + MOE_TILE, :] = jnp.zeros((MOE_TILE, dm), BF16)

    rec = jnp.concatenate([i1.astype(F32), i2.astype(F32), pos1, pos2, g1, g2,
                           jnp.zeros((MOE_LANES - 6, rows), F32)], axis=0)
    meta_ref[...] = rec.T

    base = b * (2 * n_exp)
    for e in range(n_exp):
        tab_ref[base + e] = run[e]
        tab_ref[base + n_exp + e] = cpad[e]
        run_smem[e] = run[e] + cpad[e]

    def group_copy(e, slot_, loc_e, run_e):
        src = buf_scr.at[slot_, pl.ds(pl.multiple_of(loc_e, MOE_PAD), MOE_TILE)]
        dst = sorted_ref.at[e, pl.ds(pl.multiple_of(run_e, MOE_PAD), MOE_TILE)]
        return pltpu.make_async_copy(src, dst, sems.at[slot_, e])

    @pl.when(b > 0)
    def _():
        for e in range(n_exp):
            group_copy(e, 1 - slot, 0, 0).wait()

    for e in range(n_exp):
        group_copy(e, slot, loc[e], run[e]).start()

    @pl.when(b == n_blocks - 1)
    def _():
        fin = (n_blocks) * (2 * n_exp)
        for e in range(n_exp):
            group_copy(e, slot, 0, 0).wait()
        for e in range(n_exp):
            tot = run[e] + cpad[e]
            tab_ref[fin + e] = tot
            tab_ref[fin + n_exp + e] = 0
            dst = sorted_ref.at[e, pl.ds(pl.multiple_of(tot, MOE_PAD), MOE_TILE)]
            pltpu.make_async_copy(zero_scr, dst, sems.at[slot, e]).start()
        for e in range(n_exp):
            dst = sorted_ref.at[e, pl.ds(0, MOE_TILE)]
            pltpu.make_async_copy(zero_scr, dst, sems.at[slot, e]).wait()


def _moe_expert_kernel(te_ref, tr_ref, na_ref, xs_ref, wg_ref, wu_ref, wo_ref, y_ref, acc_scr, *, n_f):
    i = pl.program_id(0)
    f = pl.program_id(1)

    @pl.when(i < na_ref[0])
    def _():
        @pl.when(f == 0)
        def _():
            acc_scr[...] = jnp.zeros_like(acc_scr)

        xs = xs_ref[0]
        gv = jnp.dot(xs, wg_ref[0], preferred_element_type=F32)
        uv = jnp.dot(xs, wu_ref[0], preferred_element_type=F32)
        a = (_silu(gv) * uv).astype(BF16)
        acc_scr[...] += jnp.dot(a, wo_ref[0], preferred_element_type=F32)

        @pl.when(f == n_f - 1)
        def _():
            y_ref[0] = acc_scr[...].astype(y_ref.dtype)


def _moe_combine_kernel(tab_ref, end_ref, x_ref, meta_ref, gate_ref, gf_ref, ys_ref, o_ref,
                        win_scr, acc_scr, tr_scr, sems, *, bsz, n_exp):
    b = pl.program_id(0)
    slot = lax.rem(b, 2)
    n_cols, n_rows, _, dm = x_ref.shape
    rows = n_cols * n_rows * bsz
    n_win = MOE_TILE // MOE_WIN

    def window(blk, slot_, e, c):
        base = blk * (2 * n_exp)
        run_e = tab_ref[base + e]
        live = tab_ref[base + n_exp + e] > c * MOE_WIN
        ws = jnp.minimum(run_e + c * MOE_WIN, end_ref[e] - MOE_WIN)
        k = e * n_win + c
        cp = pltpu.make_async_copy(ys_ref.at[e, pl.ds(pl.multiple_of(ws, MOE_PAD), MOE_WIN)],
                                   win_scr.at[slot_, k], sems.at[slot_, k])
        return live, ws, k, cp

    def fetch(blk, slot_):
        for e in range(n_exp):
            for c in range(n_win):
                live, _, _, cp = window(blk, slot_, e, c)

                @pl.when(live)
                def _():
                    cp.start()

    @pl.when(b == 0)
    def _():
        fetch(0, 0)

    fetch(b + 1, 1 - slot)

    meta = meta_ref[...]
    e1, e2 = meta[:, 0:1], meta[:, 1:2]
    pos1, pos2 = meta[:, 2:3].astype(jnp.int32), meta[:, 3:4].astype(jnp.int32)
    g1, g2 = meta[:, 4:5], meta[:, 5:6]
    acc_scr[...] = jnp.zeros_like(acc_scr)
    lane = lax.broadcasted_iota(jnp.int32, (rows, MOE_WIN), 1)

    for e in range(n_exp):
        for c in range(n_win):
            live, ws, k, cp = window(b, slot, e, c)

            @pl.when(live)
            def _():
                cp.wait()
                rel = jnp.where(e1 == e, pos1, pos2) - ws
                hit = jnp.where(e1 == e, 1.0, jnp.where(e2 == e, 1.0, 0.0))
                pe = jnp.where(rel == lane, hit, 0.0).astype(BF16)
                ge = jnp.where(e1 == e, g1, jnp.where(e2 == e, g2, 0.0))
                acc_scr[...] += ge * jnp.dot(pe, win_scr[slot, k], preferred_element_type=F32)

    upd = gate_ref[...][None] * acc_scr[...].reshape(rows // bsz, bsz, dm)
    xo = x_ref[...].reshape(rows, dm) + upd.reshape(rows, dm)
    _fill_lane_tiles(_rms(xo, gf_ref[...]), tr_scr)
    for rl in range(n_rows):
        _strided_rows_out(
            tr_scr, lambda bb, lt: o_ref.at[bb, rl, :, lt * LANES:(lt + 1) * LANES],
            n_groups=bsz, group_rows=n_cols, stride=n_rows * bsz, base=rl * bsz)


def _moe(xt, g, shift, scale, gate, router_w, router_b, w_in, w_out, g_final, *, bsz, grid_rows):
    m, d = xt.shape
    n_exp, f, _ = w_out.shape
    nb = m // MOE_TB
    cap = _moe_cap(m, nb)
    local_rows = -(-(TOP_K * MOE_TB + n_exp * (MOE_PAD - 1)) // 128) * 128
    full1 = lambda i: (0, 0)

    blk_cols = 8
    blk_rows = MOE_TB // (blk_cols * bsz)
    assert blk_rows * blk_cols * bsz == MOE_TB and grid_rows % blk_rows == 0 and GRID_W % blk_cols == 0
    nrg = grid_rows // blk_rows
    x4 = xt.reshape(GRID_W, grid_rows, bsz, d)
    x_spec1 = pl.BlockSpec((blk_cols, blk_rows, bsz, d), lambda i: (i // nrg, i % nrg, 0, 0))

    ut = (jnp.arange(MOE_TB)[:, None] < jnp.arange(MOE_TB)[None, :]).astype(BF16)
    route = functools.partial(_moe_route_kernel, bsz=bsz, n_exp=n_exp, n_blocks=nb, local_rows=local_rows)
    sorted_h, meta, tab = pl.pallas_call(
        route,
        grid=(nb,),
        in_specs=[x_spec1,
                  pl.BlockSpec((1, d), full1),
                  pl.BlockSpec((bsz, d), full1),
                  pl.BlockSpec((bsz, d), full1),
                  pl.BlockSpec((n_exp, d), full1),
                  pl.BlockSpec((n_exp, 1), full1),
                  pl.BlockSpec((MOE_TB, MOE_TB), full1)],
        out_specs=[pl.BlockSpec(memory_space=pl.ANY),
                   pl.BlockSpec((MOE_TB, MOE_LANES), lambda i: (i, 0)),
                   pl.BlockSpec(memory_space=pltpu.SMEM)],
        out_shape=[jax.ShapeDtypeStruct((n_exp, cap, d), BF16),
                   jax.ShapeDtypeStruct((m, MOE_LANES), F32),
                   jax.ShapeDtypeStruct(((nb + 1) * 2 * n_exp,), jnp.int32)],
        scratch_shapes=[pltpu.VMEM((2, local_rows + MOE_TILE, d), BF16),
                        pltpu.VMEM((MOE_TILE, d), BF16),
                        pltpu.SMEM((n_exp,), jnp.int32),
                        pltpu.SemaphoreType.DMA((2, n_exp))],
        compiler_params=_params("arbitrary"),
        name="moe_route",
    )(x4, g.reshape(1, d), shift, scale, router_w.T, router_b.reshape(n_exp, 1), ut)

    tot = tab[nb * 2 * n_exp: nb * 2 * n_exp + n_exp]
    ntile = jnp.maximum((tot + MOE_TILE - 1) // MOE_TILE, 1)
    cum = jnp.cumsum(ntile)
    n_active = cum[-1:]
    max_tiles = (TOP_K * m + nb * n_exp * (MOE_PAD - 1)) // MOE_TILE + n_exp + 1
    idc = jnp.minimum(jnp.arange(max_tiles, dtype=jnp.int32), n_active - 1)
    tile_e = jnp.sum(idc[:, None] >= cum[None, :], axis=1).astype(jnp.int32)
    tile_r = (idc - (cum - ntile)[tile_e]).astype(jnp.int32)
    end = (ntile * MOE_TILE).astype(jnp.int32)

    tf = f // 2 if (f // 2) % 128 == 0 else f
    n_f = f // tf

    def f_idx(i, j, na):
        return jnp.where(i < na[0], j, n_f - 1)

    ys = pl.pallas_call(
        functools.partial(_moe_expert_kernel, n_f=n_f),
        grid_spec=pltpu.PrefetchScalarGridSpec(
            num_scalar_prefetch=3,
            grid=(max_tiles, n_f),
            in_specs=[pl.BlockSpec((1, MOE_TILE, d), lambda i, j, te, tr, na: (te[i], tr[i], 0)),
                      pl.BlockSpec((1, d, tf), lambda i, j, te, tr, na: (te[i], 0, f_idx(i, j, na))),
                      pl.BlockSpec((1, d, tf), lambda i, j, te, tr, na: (te[i], 0, n_f + f_idx(i, j, na))),
                      pl.BlockSpec((1, tf, d), lambda i, j, te, tr, na: (te[i], f_idx(i, j, na), 0))],
            out_specs=pl.BlockSpec((1, MOE_TILE, d), lambda i, j, te, tr, na: (te[i], tr[i], 0)),
            scratch_shapes=[pltpu.VMEM((MOE_TILE, d), F32)]),
        out_shape=jax.ShapeDtypeStruct((n_exp, cap, d), BF16),
        compiler_params=_params("arbitrary", "arbitrary"),
        name="moe_experts",
    )(tile_e, tile_r, n_active.astype(jnp.int32), sorted_h, w_in, w_in, w_out)

    n_win = MOE_TILE // MOE_WIN
    out = pl.pallas_call(
        functools.partial(_moe_combine_kernel, bsz=bsz, n_exp=n_exp),
        grid_spec=pltpu.PrefetchScalarGridSpec(
            num_scalar_prefetch=2,
            grid=(nb,),
            in_specs=[pl.BlockSpec((blk_cols, blk_rows, bsz, d), lambda i, tb, en: (i // nrg, i % nrg, 0, 0)),
                      pl.BlockSpec((MOE_TB, MOE_LANES), lambda i, tb, en: (i, 0)),
                      pl.BlockSpec((bsz, d), lambda i, tb, en: (0, 0)),
                      pl.BlockSpec((1, d), lambda i, tb, en: (0, 0)),
                      pl.BlockSpec(memory_space=pl.ANY)],
            out_specs=pl.BlockSpec((bsz, blk_rows, blk_cols, d), lambda i, tb, en: (0, i % nrg, i // nrg, 0)),
            scratch_shapes=[pltpu.VMEM((2, n_exp * n_win, MOE_WIN, d), BF16),
                            pltpu.VMEM((MOE_TB, d), F32),
                            pltpu.VMEM((d // LANES, MOE_TB, LANES), F32),
                            pltpu.SemaphoreType.DMA((2, n_exp * n_win))]),
        out_shape=jax.ShapeDtypeStruct((bsz, grid_rows, GRID_W, d), F32),
        compiler_params=_params("arbitrary"),
        name="moe_combine",
    )(tab, end, x4, meta, gate, g_final.reshape(1, d), ys)
    return out.reshape(bsz, grid_rows * GRID_W, d)


def _blockdiag(w, n_gb):
    two, g, r, c = w.shape
    eye = jnp.eye(S5_GB, dtype=w.dtype)
    out = jnp.einsum("dbgrc,gh->dbgrhc", w.reshape(two, n_gb, S5_GB, r, c), eye)
    return out.reshape(two, n_gb, S5_GB * r, S5_GB * c)


def kernel(x, c, ctx, c_ctx, ada_w, ada_b, norm_mix, norm_ffn, ssd_w_in, ssd_conv_w, ssd_conv_b, ssd_dt_bias, ssd_a_log, ssd_d, ssd_norm, ssd_w_out, s5_lam_re, s5_lam_im, s5_log_step, s5_b_re, s5_b_im, s5_c_re, s5_c_im, s5_d, s5_w_glu, s5_b_glu, ffn_w_in, ffn_w_out, moe_router_w, moe_router_b, moe_w_in, moe_w_out, norm_final):
    bsz, seq, d = x.shape
    ctx_len = ctx.shape[1]
    depth = ada_w.shape[0]
    assert depth == 2, "one SSD layer followed by one S5 layer"
    rows = seq // GRID_W

    pad_rows = (-(bsz + 1)) % 8
    cc = jnp.concatenate([c, c_ctx[None], jnp.zeros((pad_rows, d), F32)], axis=0)
    mods = _ada(cc, ada_w, ada_b)

    def mod(i, n):
        sl = slice(n * d, (n + 1) * d)
        return mods[i, :bsz, sl], mods[i, bsz:bsz + 1, sl]

    heads = ssd_d.shape[1]
    d_inner = heads * SSD_HEADDIM
    conv_dim = ssd_conv_w.shape[2]
    n_main = d_inner + conv_dim
    w_in0 = ssd_w_in[0]
    w_main = w_in0[:, :n_main].astype(BF16)
    w_dt = w_in0[:, n_main:]
    conv_w_full = jnp.concatenate([jnp.zeros((SSD_CONV, d_inner), F32), 0.5 * ssd_conv_w[0]], axis=1)
    conv_b_full = jnp.concatenate([jnp.zeros((d_inner,), F32), 0.5 * ssd_conv_b[0]])
    d_exp = jnp.repeat(ssd_d[0], SSD_HEADDIM)
    w_out0 = ssd_w_out[0].astype(BF16)
    gw = (heads // SSD_GROUPS) * SSD_HEADDIM

    (sh_x, sh_c), (sc_x, sc_c), (gt_x, gt_c) = mod(0, 0), mod(0, 1), mod(0, 2)
    r3 = lambda a: a[:, None, :]
    dt_bias = ssd_dt_bias[0].reshape(-1)
    a_log = ssd_a_log[0].reshape(-1)

    def ssd_mixer(tok3, shift, scale, gate, h0):
        b_, t_, _ = tok3.shape
        zx, dtc, dtcT = _ssd_inproj(tok3, norm_mix[0], r3(shift), r3(scale), w_main, conv_w_full,
                                    conv_b_full, w_dt, dt_bias, a_log, d_inner=d_inner, heads=heads)
        y, hfin = _ssd_scan(zx, dtc, dtcT, h0, d_inner=d_inner, heads=heads)
        out = _ssd_out(y.reshape(2, b_ * t_, d_inner), zx.reshape(b_ * t_, n_main), d_exp, ssd_norm[0],
                       w_out0, tok3.reshape(b_ * t_, d), r3(gate), seq=t_, d_inner=d_inner)
        return out, hfin

    h_zero = jnp.zeros((bsz, 2, SSD_GROUPS, SSD_STATE, gw), F32)
    ctx2, h_ctx = ssd_mixer(ctx, sh_c, sc_c, gt_c, h_zero)
    x2, _ = ssd_mixer(x, sh_x, sc_x, gt_x, h_ctx)

    ffn_in = ffn_w_in[0].astype(BF16)
    ffn_out = ffn_w_out[0].astype(BF16)
    (sh_x, sh_c), (sc_x, sc_c), (gt_x, gt_c) = mod(0, 3), mod(0, 4), mod(0, 5)
    xt = _ffn(x2.reshape(bsz, seq, d), norm_ffn[0], r3(sh_x), r3(sc_x), r3(gt_x), ffn_in, ffn_out,
              time_major=True).reshape(seq * bsz, d)
    ctx2 = _ffn(ctx2.reshape(bsz, ctx_len, d), norm_ffn[0], r3(sh_c), r3(sc_c), r3(gt_c), ffn_in, ffn_out)

    ct = ctx2.reshape(bsz, ctx_len, d).transpose(1, 0, 2).reshape(ctx_len * bsz, d)

    ar, ai, bbr, bbi = _s5_discretize(s5_lam_re[0], s5_lam_im[0], s5_log_step[0], s5_b_re[0], s5_b_im[0])
    n_groups = ar.shape[1]
    n_gb = n_groups // S5_GB
    wb = jnp.concatenate([_blockdiag(bbr.transpose(0, 1, 3, 2), n_gb),
                          _blockdiag(bbi.transpose(0, 1, 3, 2), n_gb)], axis=-1).astype(BF16)
    wc = jnp.concatenate([_blockdiag(s5_c_re[0].transpose(0, 1, 3, 2), n_gb),
                          _blockdiag(-s5_c_im[0].transpose(0, 1, 3, 2), n_gb)], axis=2).astype(BF16)
    a_coef = jnp.stack([ar.reshape(2, -1), ai.reshape(2, -1)], axis=1)
    nstate = a_coef.shape[-1]

    (sh_x, sh_c), (sc_x, sc_c), (gt_x, _) = mod(1, 0), mod(1, 1), mod(1, 2)
    bc = lambda a: jnp.broadcast_to(a, (bsz, d))
    s_zero = jnp.zeros((2, 2, bsz, nstate), F32)
    _, s_ctx = _s5_scan(ct, norm_mix[1], bc(sh_c), bc(sc_c), wb, wc, a_coef, s_zero, bsz=bsz)
    n_exp, _, two_f = moe_w_in.shape[1:]
    y, _, w_in_b, w_out_b = _s5_scan(xt, norm_mix[1], sh_x, sc_x, wb, wc, a_coef, s_ctx, bsz=bsz,
                                     cast=(moe_w_in[0].reshape(n_exp * d, two_f),
                                           moe_w_out[0].reshape(n_exp * (two_f // 2), d)))
    w_in_b = w_in_b.reshape(n_exp, d, two_f)
    w_out_b = w_out_b.reshape(n_exp, two_f // 2, d)
    xt = _s5_glu(xt, y, norm_mix[1], sh_x, sc_x, gt_x, s5_d[0], s5_w_glu[0].astype(BF16), s5_b_glu[0], bsz=bsz)

    (sh_x, _), (sc_x, _), (gt_x, _) = mod(1, 3), mod(1, 4), mod(1, 5)
    return _moe(xt, norm_ffn[1], sh_x, sc_x, gt_x, moe_router_w[0], moe_router_b[0],
                w_in_b, w_out_b, norm_final, bsz=bsz, grid_rows=rows)
```

```python
import functools

import jax
import jax.numpy as jnp
from jax import lax
from jax.experimental import pallas as pl
from jax.experimental.pallas import tpu as pltpu

F32 = jnp.float32
BF16 = jnp.bfloat16
HIGHEST = lax.Precision.HIGHEST

EPS = 1e-6
GRID_W = 64
N_MOD = 6

SSD_HEADDIM = 64
SSD_GROUPS = 4
SSD_STATE = 128
SSD_CONV = 5
SSD_CHUNK = 128

S5_CH = 16
S5_STATE = 64
S5_GB = 8

TOP_K = 2

V7X_VMEM_BYTES = 64 * 1024 * 1024
VMEM_LIMIT = V7X_VMEM_BYTES - 4 * 1024 * 1024
NEG_BIG = -1e30
LANES = 128
LOG2E = 1.4426950408889634


def _fill_lane_tiles(res, tr_scr):
    for lt in range(res.shape[1] // LANES):
        tr_scr[lt] = res[:, lt * LANES:(lt + 1) * LANES]


def _strided_rows_out(tr_scr, dst, *, n_groups, group_rows, stride, base=0):
    for g in range(n_groups):
        for lt in range(tr_scr.shape[0]):
            dst(g, lt)[...] = tr_scr[lt, pl.ds(base + g, group_rows, stride=stride), :]


def _params(*sem):
    return pltpu.CompilerParams(dimension_semantics=sem, vmem_limit_bytes=VMEM_LIMIT)


def _silu(v):
    return v * jax.nn.sigmoid(v)


def _softplus(v):
    return jnp.maximum(v, 0.0) + jnp.log1p(jnp.exp(-jnp.abs(v)))


def _rms(x, g):
    inv = lax.rsqrt(jnp.mean(x * x, axis=-1, keepdims=True) + EPS)
    return (x * inv) * g


def _nt_dot(a, b, precision=None):
    return lax.dot_general(a, b, (((1,), (1,)), ((), ())), precision=precision,
                           preferred_element_type=F32)


def _ada_kernel(c_ref, w_ref, b_ref, o_ref):
    s = _silu(c_ref[...])
    o_ref[0] = jnp.dot(s, w_ref[0], precision=HIGHEST, preferred_element_type=F32) + b_ref[0]


def _ada(cc, ada_w, ada_b):
    depth, d, n = ada_w.shape
    r = cc.shape[0]
    tn = 1536
    return pl.pallas_call(
        _ada_kernel,
        grid=(depth, n // tn),
        in_specs=[pl.BlockSpec((r, d), lambda i, j: (0, 0)),
                  pl.BlockSpec((1, d, tn), lambda i, j: (i, 0, j)),
                  pl.BlockSpec((1, 1, tn), lambda i, j: (i, 0, j))],
        out_specs=pl.BlockSpec((1, r, tn), lambda i, j: (i, 0, j)),
        out_shape=jax.ShapeDtypeStruct((depth, r, n), F32),
        compiler_params=_params("arbitrary", "arbitrary"),
        name="ada_mod",
    )(cc, ada_w, ada_b.reshape(depth, 1, n))


def _ssd_inproj_kernel(x_ref, g_ref, sh_ref, sc_ref, w_ref, cw_ref, cb_ref, wdt_ref, wdtT_ref,
                       bdt_ref, bdtT_ref, alog_ref, alogT_ref,
                       o_ref, dt_ref, dtT_ref, h_scr, pad_scr, *, n_plain, seq, heads, row_chunk, dt_chunk):
    j = pl.program_id(1)
    pad = 8

    @pl.when(j == 0)
    def _():
        for r in range(0, seq, dt_chunk):
            rs = slice(r, r + dt_chunk)
            h = _rms(x_ref[0, rs, :], g_ref[...]) * (1.0 + sc_ref[0]) + sh_ref[0]
            hb = h.astype(BF16)
            h_scr[rs, :] = hb
            hl = (h - hb.astype(F32)).astype(BF16)
            nh2 = 2 * heads
            p = jnp.dot(hb, wdt_ref[...], preferred_element_type=F32)
            q = jnp.dot(hl, wdt_ref[...], preferred_element_type=F32)
            dt = _softplus(p[:, :nh2] + (p[:, nh2:] + q[:, :nh2]) + bdt_ref[...])
            da = dt * (-jnp.exp(alog_ref[...]))
            pT = _nt_dot(wdtT_ref[...], hb)
            qT = _nt_dot(wdtT_ref[...], hl)
            dtT = _softplus(pT[:nh2, :] + (pT[nh2:, :] + qT[:nh2, :]) + bdtT_ref[...])
            daT = dtT * (-jnp.exp(alogT_ref[...]))
            for d in range(2):
                sl = slice(d * heads, (d + 1) * heads)
                dt_ref[d, 0, rs, :] = jnp.concatenate([dt[:, sl], da[:, sl]], axis=-1)
                dtT_ref[d, 0, :, rs] = jnp.concatenate([dtT[sl, :], daT[sl, :]], axis=0)
        zeros = jnp.zeros((pad, pad_scr.shape[1]), F32)
        pad_scr[0:pad, :] = zeros
        pad_scr[pad + seq:2 * pad + seq, :] = zeros

    acc = jnp.dot(h_scr[...], w_ref[...], preferred_element_type=F32)

    @pl.when(j < n_plain)
    def _():
        o_ref[0] = acc.astype(o_ref.dtype)

    @pl.when(j >= n_plain)
    def _():
        pad_scr[pad:pad + seq, :] = acc
        half = (SSD_CONV - 1) // 2
        n_all = row_chunk + 2 * pad
        for r in range(0, seq, row_chunk):
            a = pad_scr[r:r + n_all, :]
            s = cb_ref[...] + cw_ref[half:half + 1, :] * a[pad:pad + row_chunk, :]
            for k in range(SSD_CONV):
                if k != half:
                    shifted = pltpu.roll(a, (half - k) % n_all, axis=0)
                    s = s + cw_ref[k:k + 1, :] * shifted[pad:pad + row_chunk, :]
            o_ref[0, r:r + row_chunk, :] = (s * (1.0 + jnp.tanh(s))).astype(o_ref.dtype)


def _ssd_inproj(x3, g, shift, scale, w_main, conv_w_full, conv_b_full, w_dt, dt_bias, a_log,
                *, d_inner, heads):
    bsz, seq, d = x3.shape
    n = w_main.shape[1]
    tn = 512
    nh2 = 2 * heads
    per_batch = shift.shape[0] > 1
    mod_map = (lambda b, j: (b, 0, 0)) if per_batch else (lambda b, j: (0, 0, 0))
    kern = functools.partial(_ssd_inproj_kernel, n_plain=d_inner // tn, seq=seq, heads=heads,
                             row_chunk=min(64, seq), dt_chunk=min(256, seq))
    w_hi = w_dt.astype(BF16)
    w_dt2 = jnp.concatenate([w_hi, (w_dt - w_hi.astype(F32)).astype(BF16)], axis=1)
    const2 = lambda b, j: (0, 0)
    return pl.pallas_call(
        kern,
        grid=(bsz, n // tn),
        in_specs=[pl.BlockSpec((1, seq, d), lambda b, j: (b, 0, 0), pipeline_mode=pl.Buffered(1)),
                  pl.BlockSpec((1, d), const2),
                  pl.BlockSpec((1, 1, d), mod_map),
                  pl.BlockSpec((1, 1, d), mod_map),
                  pl.BlockSpec((d, tn), lambda b, j: (0, j)),
                  pl.BlockSpec((SSD_CONV, tn), lambda b, j: (0, j)),
                  pl.BlockSpec((1, tn), lambda b, j: (0, j)),
                  pl.BlockSpec((d, 2 * nh2), const2),
                  pl.BlockSpec((2 * nh2, d), const2),
                  pl.BlockSpec((1, nh2), const2),
                  pl.BlockSpec((nh2, 1), const2),
                  pl.BlockSpec((1, nh2), const2),
                  pl.BlockSpec((nh2, 1), const2)],
        out_specs=[pl.BlockSpec((1, seq, tn), lambda b, j: (b, 0, j)),
                   pl.BlockSpec((2, 1, seq, nh2), lambda b, j: (0, b, 0, 0)),
                   pl.BlockSpec((2, 1, nh2, seq), lambda b, j: (0, b, 0, 0))],
        out_shape=[jax.ShapeDtypeStruct((bsz, seq, n), BF16),
                   jax.ShapeDtypeStruct((2, bsz, seq, nh2), F32),
                   jax.ShapeDtypeStruct((2, bsz, nh2, seq), F32)],
        scratch_shapes=[pltpu.VMEM((seq, d), BF16), pltpu.VMEM((seq + 16, tn), F32)],
        compiler_params=_params("arbitrary", "arbitrary"),
        name="ssd_inproj",
    )(x3, g.reshape(1, d), shift, scale, w_main, conv_w_full, conv_b_full.reshape(1, n),
      w_dt2, w_dt2.T, dt_bias.reshape(1, nh2), dt_bias.reshape(nh2, 1),
      a_log.reshape(1, nh2), a_log.reshape(nh2, 1))


def _ssd_scan_kernel(xs_ref, b_ref, c_ref, dtc_ref, dtcT_ref, h0_ref, y_ref, hfin_ref,
                     st_scr, *, heads, n_chunks):
    d = pl.program_id(1)
    k = pl.program_id(2)
    L = xs_ref.shape[1]
    hpg = heads // SSD_GROUPS
    pairs_per_group = hpg // 2
    gw = hpg * SSD_HEADDIM

    @pl.when(k == 0)
    def _():
        st_scr[...] = h0_ref[0, 0]

    row = lax.broadcasted_iota(jnp.int32, (L, L), 0)
    col = lax.broadcasted_iota(jnp.int32, (L, L), 1)
    mask = jnp.where(d == 0, row - col, col - row) >= 0
    mask_f = mask.astype(F32)

    dtc = dtc_ref[0, 0]
    dtcT = dtcT_ref[0, 0]
    assert L == 2 * SSD_HEADDIM, "the head-pair tiles assume chunk length == 2 * head dim"
    da = dtc[:, heads:]
    dtT, daT = dtcT[:heads, :], dtcT[heads:, :]
    cum = jnp.dot(mask_f, da, precision=HIGHEST, preferred_element_type=F32)
    cumT = _nt_dot(daT, mask_f, precision=HIGHEST)
    tot = jnp.sum(da, axis=0, keepdims=True)
    totT = jnp.sum(daT, axis=1, keepdims=True)
    etot = jnp.exp(tot)
    c2 = cum * LOG2E
    c2T = cumT * LOG2E
    wrow = jnp.exp(totT - cumT) * dtT

    lane = lax.broadcasted_iota(jnp.int32, (L, 2 * SSD_HEADDIM), 1)
    lo = lane < SSD_HEADDIM
    lo_row = lax.broadcasted_iota(jnp.int32, (1, 2 * SSD_HEADDIM), 1) < SSD_HEADDIM

    for g in range(SSD_GROUPS):
        bg = b_ref[0, :, g * SSD_STATE:(g + 1) * SSD_STATE]
        cg = c_ref[0, :, g * SSD_STATE:(g + 1) * SSD_STATE]
        scores = _nt_dot(cg, bg) * mask_f
        bgT = bg.astype(F32).T
        yoff = jnp.dot(cg, st_scr[g].astype(BF16), preferred_element_type=F32)
        for q in range(pairs_per_group):
            p = g * pairs_per_group + q
            h0, h1 = 2 * p, 2 * p + 1
            xs2 = xs_ref[0, :, p * 128:(p + 1) * 128]
            zero = jnp.zeros_like(xs2)
            rhs = jnp.concatenate([jnp.where(lo, xs2, zero), jnp.where(lo, zero, xs2)], axis=0)
            ms, bs, es = [], [], []
            for h in (h0, h1):
                col = jnp.broadcast_to(c2[:, h:h + 1], (L, L))
                dec = jnp.exp2(jnp.minimum(col - c2T[h:h + 1, :], 0.0))
                ms.append((scores * dec * dtT[h:h + 1, :]).astype(BF16))
                bs.append((bgT * wrow[h:h + 1, :]).astype(BF16))
                es.append(jnp.exp2(col))
            ydiag = jnp.dot(jnp.concatenate(ms, axis=1), rhs, preferred_element_type=F32)
            new = jnp.dot(jnp.concatenate(bs, axis=1), rhs, preferred_element_type=F32)
            sl = slice(q * 128, (q + 1) * 128)
            y2 = ydiag + yoff[:, sl] * jnp.where(lo, es[0], es[1])
            y_ref[0, 0, :, p * 128:(p + 1) * 128] = y2.astype(y_ref.dtype)
            et2 = jnp.where(lo_row, etot[:, h0:h0 + 1], etot[:, h1:h1 + 1])
            st_scr[g, :, sl] = st_scr[g, :, sl] * et2 + new

    @pl.when(k == n_chunks - 1)
    def _():
        hfin_ref[0, 0] = st_scr[...]


def _ssd_scan(zx, dtc, dtcT, h0, *, d_inner, heads):
    bsz, seq, _ = zx.shape
    L = min(SSD_CHUNK, seq)
    nc = seq // L
    bc_w = SSD_GROUPS * SSD_STATE
    gw = (heads // SSD_GROUPS) * SSD_HEADDIM
    nh2 = 2 * heads
    xs_blk = d_inner // d_inner
    b_blk = (2 * d_inner) // bc_w
    c_blk = b_blk + 1

    def chunk(d, k):
        return k + d * (nc - 1 - 2 * k)

    kern = functools.partial(_ssd_scan_kernel, heads=heads, n_chunks=nc)
    st_shape = (SSD_GROUPS, SSD_STATE, gw)
    return pl.pallas_call(
        kern,
        grid=(bsz, 2, nc),
        in_specs=[pl.BlockSpec((1, L, d_inner), lambda b, d, k: (b, chunk(d, k), xs_blk)),
                  pl.BlockSpec((1, L, bc_w), lambda b, d, k: (b, chunk(d, k), b_blk)),
                  pl.BlockSpec((1, L, bc_w), lambda b, d, k: (b, chunk(d, k), c_blk)),
                  pl.BlockSpec((1, 1, L, nh2), lambda b, d, k: (d, b, chunk(d, k), 0)),
                  pl.BlockSpec((1, 1, nh2, L), lambda b, d, k: (d, b, 0, chunk(d, k))),
                  pl.BlockSpec((1, 1) + st_shape, lambda b, d, k: (b, d, 0, 0, 0))],
        out_specs=[pl.BlockSpec((1, 1, L, d_inner), lambda b, d, k: (d, b, chunk(d, k), 0)),
                   pl.BlockSpec((1, 1) + st_shape, lambda b, d, k: (b, d, 0, 0, 0))],
        out_shape=[jax.ShapeDtypeStruct((2, bsz, seq, d_inner), BF16),
                   jax.ShapeDtypeStruct((bsz, 2) + st_shape, F32)],
        scratch_shapes=[pltpu.VMEM(st_shape, F32)],
        compiler_params=_params("arbitrary", "arbitrary", "arbitrary"),
        name="ssd_scan",
    )(zx, zx, zx, dtc, dtcT, h0)


def _ssd_out_kernel(y_ref, z_ref, xs_ref, dsk_ref, nw_ref, w_ref, x_ref, gate_ref, o_ref):
    y = y_ref[0].astype(F32) + y_ref[1].astype(F32) + dsk_ref[...] * xs_ref[...].astype(F32)
    yg = y * _silu(z_ref[...].astype(F32))
    yn = _rms(yg, nw_ref[...]).astype(BF16)
    out = jnp.dot(yn, w_ref[...], preferred_element_type=F32)
    o_ref[...] = x_ref[...] + gate_ref[0] * out


def _ssd_out(y, zx, d_exp, norm_w, w_out, x2, gate, *, seq, d_inner):
    m, d = x2.shape
    tm = min(512, seq)
    bpt = seq // tm
    per_batch = gate.shape[0] > 1
    gate_map = (lambda i: (i // bpt, 0, 0)) if per_batch else (lambda i: (0, 0, 0))
    return pl.pallas_call(
        _ssd_out_kernel,
        grid=(m // tm,),
        in_specs=[pl.BlockSpec((2, tm, d_inner), lambda i: (0, i, 0)),
                  pl.BlockSpec((tm, d_inner), lambda i: (i, 0)),
                  pl.BlockSpec((tm, d_inner), lambda i: (i, 1)),
                  pl.BlockSpec((1, d_inner), lambda i: (0, 0)),
                  pl.BlockSpec((1, d_inner), lambda i: (0, 0)),
                  pl.BlockSpec((d_inner, d), lambda i: (0, 0)),
                  pl.BlockSpec((tm, d), lambda i: (i, 0)),
                  pl.BlockSpec((1, 1, d), gate_map)],
        out_specs=pl.BlockSpec((tm, d), lambda i: (i, 0)),
        out_shape=jax.ShapeDtypeStruct((m, d), F32),
        compiler_params=_params("arbitrary"),
        name="ssd_out",
    )(y, zx, zx, d_exp.reshape(1, d_inner), norm_w.reshape(1, d_inner), w_out, x2, gate)


def _ffn_kernel(x_ref, g_ref, sh_ref, sc_ref, gate_ref, wg_ref, wu_ref, wo_ref, o_ref,
                h_scr, acc_scr, tr_scr, *, n_f, time_major):
    j = pl.program_id(1)
    bb, tt, dm = x_ref.shape

    @pl.when(j == 0)
    def _():
        h = _rms(x_ref[...], g_ref[...]) * (1.0 + sc_ref[...]) + sh_ref[...]
        h_scr[...] = h.reshape(bb * tt, dm).astype(BF16)
        acc_scr[...] = jnp.zeros_like(acc_scr)

    h = h_scr[...]
    gv = jnp.dot(h, wg_ref[...], preferred_element_type=F32)
    uv = jnp.dot(h, wu_ref[...], preferred_element_type=F32)
    a = (_silu(gv) * uv).astype(BF16)
    acc_scr[...] += jnp.dot(a, wo_ref[...], preferred_element_type=F32)

    @pl.when(j == n_f - 1)
    def _():
        res = x_ref[...] + gate_ref[...] * acc_scr[...].reshape(bb, tt, dm)
        if time_major:
            _fill_lane_tiles(res.reshape(bb * tt, dm), tr_scr)
            _strided_rows_out(tr_scr, lambda c, lt: o_ref.at[c, 0, :, lt * LANES:(lt + 1) * LANES],
                              n_groups=tt, group_rows=bb, stride=tt)
        else:
            o_ref[...] = res


def _ffn(x3, g, shift, scale, gate, w_in, w_out, *, time_major=False):
    bsz, seq, d = x3.shape
    f = w_out.shape[0]
    tf = f // 2 if (f // 2) % 128 == 0 else f
    n_f = f // tf
    per_batch = shift.shape[0] > 1
    if time_major:
        bb, tt = 8, GRID_W
        assert bsz % bb == 0 and seq % tt == 0 and per_batch
        rows = seq // tt
        x_map = lambda i, j: (i // rows, i % rows, 0)
        mod_map = lambda i, j: (i // rows, 0, 0)
        out_spec = pl.BlockSpec((tt, 1, bb, d), lambda i, j: (0, i % rows, i // rows, 0))
        out_shape = jax.ShapeDtypeStruct((tt, rows, bsz, d), F32)
        n_blocks = (bsz // bb) * rows
    else:
        bb, tt = 1, min(512, seq)
        bpt = seq // tt
        x_map = lambda i, j: (i // bpt, i % bpt, 0)
        mod_map = (lambda i, j: (i // bpt, 0, 0)) if per_batch else (lambda i, j: (0, 0, 0))
        out_spec = pl.BlockSpec((bb, tt, d), x_map)
        out_shape = jax.ShapeDtypeStruct((bsz, seq, d), F32)
        n_blocks = bsz * bpt
    mb = bb if per_batch else 1
    tm = bb * tt
    kern = functools.partial(_ffn_kernel, n_f=n_f, time_major=time_major)
    return pl.pallas_call(
        kern,
        grid=(n_blocks, n_f),
        in_specs=[pl.BlockSpec((bb, tt, d), x_map),
                  pl.BlockSpec((1, d), lambda i, j: (0, 0)),
                  pl.BlockSpec((mb, 1, d), mod_map),
                  pl.BlockSpec((mb, 1, d), mod_map),
                  pl.BlockSpec((mb, 1, d), mod_map),
                  pl.BlockSpec((d, tf), lambda i, j: (0, j)),
                  pl.BlockSpec((d, tf), lambda i, j: (0, n_f + j)),
                  pl.BlockSpec((tf, d), lambda i, j: (j, 0))],
        out_specs=out_spec,
        out_shape=out_shape,
        scratch_shapes=[pltpu.VMEM((tm, d), BF16), pltpu.VMEM((tm, d), F32),
                        pltpu.VMEM((d // LANES, tm if time_major else 8, LANES), F32)],
        compiler_params=_params("arbitrary", "arbitrary"),
        name="dense_swiglu",
    )(x3, g.reshape(1, d), shift, scale, gate, w_in, w_in, w_out)


def _s5_disc_kernel(lre_ref, lim_ref, step_ref, bre_ref, bim_ref, ar_ref, ai_ref, bbr_ref, bbi_ref):
    step = jnp.exp(step_ref[...])
    lre, lim = lre_ref[...], lim_ref[...]
    mag = jnp.exp(lre * step)
    ar = mag * jnp.cos(lim * step)
    ai = mag * jnp.sin(lim * step)
    den = lre * lre + lim * lim
    cr = ((ar - 1.0) * lre + ai * lim) / den
    ci = (ai * lre - (ar - 1.0) * lim) / den
    ar_ref[...] = ar
    ai_ref[...] = ai
    bbr_ref[...] = cr * bre_ref[...] - ci * bim_ref[...]
    bbi_ref[...] = cr * bim_ref[...] + ci * bre_ref[...]


def _s5_discretize(lam_re, lam_im, log_step, b_re, b_im):
    two, g, p = lam_re.shape
    c = b_re.shape[-1]
    lre = jnp.repeat(lam_re, c, axis=-1)
    lim = jnp.repeat(lam_im, c, axis=-1)
    shp = jax.ShapeDtypeStruct((two, g, p * c), F32)
    ar, ai, bbr, bbi = pl.pallas_call(
        _s5_disc_kernel, out_shape=[shp, shp, shp, shp], name="s5_discretize",
    )(lre, lim, log_step.reshape(two, g, 1), b_re.reshape(two, g, p * c), b_im.reshape(two, g, p * c))
    ar = ar.reshape(two, g, p, c)[..., 0]
    ai = ai.reshape(two, g, p, c)[..., 0]
    return ar, ai, bbr.reshape(two, g, p, c), bbi.reshape(two, g, p, c)


def _s5_kernel(x_ref, g_ref, sh_ref, sc_ref, wb_ref, wc_ref, a_ref, s0_ref, *rest, n_chunks, steps, bsz, n_cast):
    cast_in, (y_ref, sfin_ref), cast_out = rest[:n_cast], rest[n_cast:n_cast + 2], rest[n_cast + 2:2 * n_cast + 2]
    st_scr, bu0_scr, bu1_scr, sb0_scr, sb1_scr = rest[2 * n_cast + 2:]
    for src, dst in zip(cast_in, cast_out):
        dst[...] = src[...].astype(dst.dtype)
    bu_bufs = (bu0_scr, bu1_scr)
    sb_bufs = (sb0_scr, sb1_scr)
    d = pl.program_id(0)
    k = pl.program_id(1)
    rows, dm = x_ref.shape
    gl = S5_GB * S5_STATE
    cb = S5_GB * S5_CH

    @pl.when(k == 0)
    def _():
        st_scr[...] = s0_ref[0]

    hn = _rms(x_ref[...], g_ref[...]).reshape(steps, bsz, dm)
    h = hn * (1.0 + sc_ref[...])[None] + sh_ref[...][None]
    u = h.reshape(rows, dm).astype(BF16)

    n_gb = dm // cb

    def bu_dot(gb):
        return jnp.dot(u[:, gb * cb:(gb + 1) * cb], wb_ref[0, gb], preferred_element_type=F32)

    def sweep(reverse):
        bu_bufs[0][...] = bu_dot(0)
        for gb in range(n_gb):
            bu_cur, bu_nxt = bu_bufs[gb % 2], bu_bufs[1 - gb % 2]
            sb_cur = sb_bufs[gb % 2]
            if gb + 1 < n_gb:
                bu_nxt[...] = bu_dot(gb + 1)
            sl = slice(gb * gl, (gb + 1) * gl)
            ar = jnp.broadcast_to(a_ref[0, 0:1, sl], (bsz, gl))
            ai = jnp.broadcast_to(a_ref[0, 1:2, sl], (bsz, gl))
            sr, si = st_scr[0, :, sl], st_scr[1, :, sl]
            for i in range(steps):
                r0 = (steps - 1 - i if reverse else i) * bsz
                bur = bu_cur[r0:r0 + bsz, 0:gl]
                bui = bu_cur[r0:r0 + bsz, gl:2 * gl]
                sr, si = ar * sr - ai * si + bur, ar * si + ai * sr + bui
                sb_cur[r0:r0 + bsz, 0:gl] = sr.astype(BF16)
                sb_cur[r0:r0 + bsz, gl:2 * gl] = si.astype(BF16)
            st_scr[0, :, sl] = sr
            st_scr[1, :, sl] = si
            yv = jnp.dot(sb_cur[...], wc_ref[0, gb], preferred_element_type=F32)
            y_ref[0, :, gb * cb:(gb + 1) * cb] = yv.astype(y_ref.dtype)

    @pl.when(d == 0)
    def _():
        sweep(False)

    @pl.when(d == 1)
    def _():
        sweep(True)

    @pl.when(k == n_chunks - 1)
    def _():
        sfin_ref[0] = st_scr[...]


def _s5_scan(xt, g, shift, scale, wb, wc, a, s0, *, bsz, cast=()):
    m, d = xt.shape
    seq = m // bsz
    steps = min(64, seq)
    rows = steps * bsz
    nc = seq // steps
    nstate = s0.shape[-1]
    n_gb = d // (S5_GB * S5_CH)
    gl = S5_GB * S5_STATE

    def chunk(dd, k):
        return k + dd * (nc - 1 - 2 * k)

    n_steps = 2 * nc
    cast_specs, cast_shapes = [], []
    for w in cast:
        slab = w.shape[0] // n_steps
        assert slab * n_steps == w.shape[0] and slab % 16 == 0
        cast_specs.append(pl.BlockSpec((slab, w.shape[1]), lambda dd, k: (dd * nc + k, 0)))
        cast_shapes.append(jax.ShapeDtypeStruct(w.shape, BF16))

    kern = functools.partial(_s5_kernel, n_chunks=nc, steps=steps, bsz=bsz, n_cast=len(cast))
    return pl.pallas_call(
        kern,
        grid=(2, nc),
        in_specs=[pl.BlockSpec((rows, d), lambda dd, k: (chunk(dd, k), 0)),
                  pl.BlockSpec((1, d), lambda dd, k: (0, 0)),
                  pl.BlockSpec((bsz, d), lambda dd, k: (0, 0)),
                  pl.BlockSpec((bsz, d), lambda dd, k: (0, 0)),
                  pl.BlockSpec((1, n_gb, S5_GB * S5_CH, 2 * gl), lambda dd, k: (dd, 0, 0, 0),
                               pipeline_mode=pl.Buffered(1)),
                  pl.BlockSpec((1, n_gb, 2 * gl, S5_GB * S5_CH), lambda dd, k: (dd, 0, 0, 0),
                               pipeline_mode=pl.Buffered(1)),
                  pl.BlockSpec((1, 2, nstate), lambda dd, k: (dd, 0, 0)),
                  pl.BlockSpec((1, 2, bsz, nstate), lambda dd, k: (dd, 0, 0, 0))] + cast_specs,
        out_specs=[pl.BlockSpec((1, rows, d), lambda dd, k: (dd, chunk(dd, k), 0)),
                   pl.BlockSpec((1, 2, bsz, nstate), lambda dd, k: (dd, 0, 0, 0))] + cast_specs,
        out_shape=[jax.ShapeDtypeStruct((2, m, d), BF16),
                   jax.ShapeDtypeStruct((2, 2, bsz, nstate), F32)] + cast_shapes,
        scratch_shapes=[pltpu.VMEM((2, bsz, nstate), F32),
                        pltpu.VMEM((rows, 2 * gl), F32), pltpu.VMEM((rows, 2 * gl), F32),
                        pltpu.VMEM((rows, 2 * gl), BF16), pltpu.VMEM((rows, 2 * gl), BF16)],
        compiler_params=_params("arbitrary", "arbitrary"),
        name="s5_scan",
    )(xt, g.reshape(1, d), shift, scale, wb, wc, a, s0, *cast)


def _gelu_tanh(v):
    return 0.5 * v * (1.0 + jnp.tanh(0.7978845608028654 * (v + 0.044715 * (v * v * v))))


def _s5_glu_kernel(x_ref, y_ref, g_ref, sh_ref, sc_ref, gate_ref, dsk_ref, w_ref, b_ref, o_ref, *, bsz):
    rows, dm = x_ref.shape
    x = x_ref[...]
    hn = _rms(x, g_ref[...]).reshape(rows // bsz, bsz, dm)
    h = (hn * (1.0 + sc_ref[...])[None] + sh_ref[...][None]).reshape(rows, dm)
    yv = dsk_ref[...] * h + y_ref[0].astype(F32) + y_ref[1].astype(F32)
    ge = _gelu_tanh(yv).astype(BF16)
    o = jnp.dot(ge, w_ref[...], preferred_element_type=F32) + b_ref[...]
    out = o[:, :dm] * jax.nn.sigmoid(o[:, dm:])
    res = x.reshape(rows // bsz, bsz, dm) + gate_ref[...][None] * out.reshape(rows // bsz, bsz, dm)
    o_ref[...] = res.reshape(rows, dm)


def _s5_glu(xt, y, g, shift, scale, gate, d_skip, w_glu, b_glu, *, bsz):
    m, d = xt.shape
    tm = min(512, m)
    kern = functools.partial(_s5_glu_kernel, bsz=bsz)
    full = lambda i: (0, 0)
    return pl.pallas_call(
        kern,
        grid=(m // tm,),
        in_specs=[pl.BlockSpec((tm, d), lambda i: (i, 0)),
                  pl.BlockSpec((2, tm, d), lambda i: (0, i, 0)),
                  pl.BlockSpec((1, d), full),
                  pl.BlockSpec((bsz, d), full),
                  pl.BlockSpec((bsz, d), full),
                  pl.BlockSpec((bsz, d), full),
                  pl.BlockSpec((1, d), full),
                  pl.BlockSpec((d, 2 * d), full),
                  pl.BlockSpec((1, 2 * d), full)],
        out_specs=pl.BlockSpec((tm, d), lambda i: (i, 0)),
        out_shape=jax.ShapeDtypeStruct((m, d), F32),
        compiler_params=_params("arbitrary"),
        name="s5_glu",
    )(xt, y, g.reshape(1, d), shift, scale, gate, d_skip.reshape(1, d), w_glu, b_glu.reshape(1, 2 * d))


MOE_TB = 512
MOE_TILE = 512
MOE_WIN = 256
MOE_PAD = 16
MOE_LANES = 128


def _moe_cap(n_tok, n_blocks):
    rows = n_tok + n_blocks * MOE_PAD + MOE_TILE
    return -(-rows // MOE_TILE) * MOE_TILE


def _moe_route_kernel(x_ref, g_ref, sh_ref, sc_ref, rwT_ref, rbT_ref, ut_ref,
                      sorted_ref, meta_ref, tab_ref,
                      buf_scr, zero_scr, run_smem, sems, *, bsz, n_exp, n_blocks, local_rows):
    b = pl.program_id(0)
    slot = lax.rem(b, 2)
    dm = x_ref.shape[-1]
    rows = x_ref.shape[0] * x_ref.shape[1] * x_ref.shape[2]

    @pl.when(b == 0)
    def _():
        for e in range(n_exp):
            run_smem[e] = 0
        zero_scr[...] = jnp.zeros_like(zero_scr)

    hn = _rms(x_ref[...].reshape(rows // bsz, bsz, dm), g_ref[...])
    h = (hn * (1.0 + sc_ref[...])[None] + sh_ref[...][None]).reshape(rows, dm)
    hb = h.astype(BF16)

    logits = _nt_dot(rwT_ref[...], h, precision=HIGHEST) + rbT_ref[...]
    sub = lax.broadcasted_iota(jnp.int32, logits.shape, 0)
    m1 = jnp.max(logits, axis=0, keepdims=True)
    i1 = jnp.min(jnp.where(logits == m1, sub, n_exp), axis=0, keepdims=True)
    rest = jnp.where(sub == i1, -jnp.inf, logits)
    m2 = jnp.max(rest, axis=0, keepdims=True)
    i2 = jnp.min(jnp.where(rest == m2, sub, n_exp), axis=0, keepdims=True)
    e2 = jnp.exp(m2 - m1)
    g1 = 1.0 / (1.0 + e2)
    g2 = e2 * g1
    sel1 = sub == i1
    sel2 = sub == i2
    oh = jnp.where(sel1, 1.0, 0.0) + jnp.where(sel2, 1.0, 0.0)
    cnt = jnp.sum(oh, axis=1, keepdims=True).astype(jnp.int32)
    cpad_v = ((cnt + (MOE_PAD - 1)) // MOE_PAD) * MOE_PAD
    rank = jnp.dot(oh.astype(BF16), ut_ref[...], preferred_element_type=F32)

    cpad, loc, run = [], [], []
    off = 0
    for e in range(n_exp):
        cpad.append(cpad_v[e, 0])
        loc.append(off)
        off = off + cpad[e]
        run.append(run_smem[e])
    sub1 = lax.broadcasted_iota(jnp.int32, (n_exp, 1), 0)
    loc_v = jnp.zeros((n_exp, 1), jnp.int32)
    run_v = jnp.zeros((n_exp, 1), jnp.int32)
    for e in range(n_exp):
        loc_v = jnp.where(sub1 == e, loc[e], loc_v)
        run_v = jnp.where(sub1 == e, run[e], run_v)
    loc_v = loc_v.astype(F32)
    run_v = run_v.astype(F32)

    def pick(sel, v):
        return jnp.sum(jnp.where(sel, v, 0.0), axis=0, keepdims=True)

    lp1 = pick(sel1, loc_v + rank).astype(jnp.int32)
    lp2 = pick(sel2, loc_v + rank).astype(jnp.int32)
    pos1 = pick(sel1, run_v + rank)
    pos2 = pick(sel2, run_v + rank)

    r = lax.broadcasted_iota(jnp.int32, (local_rows, rows), 0)
    onehot = jnp.where(r == lp1, 1.0, jnp.where(r == lp2, 1.0, 0.0)).astype(BF16)
    buf_scr[slot, 0:local_rows, :] = jnp.dot(onehot, hb, preferred_element_type=F32).astype(BF16)
    buf_scr[slot, local_rows:local_rows + MOE_TILE, :] = jnp.zeros((MOE_TILE, dm), BF16)

    rec = jnp.concatenate([i1.astype(F32), i2.astype(F32), pos1, pos2, g1, g2,
                           jnp.zeros((MOE_LANES - 6, rows), F32)], axis=0)
    meta_ref[...] = rec.T

    base = b * (2 * n_exp)
    for e in range(n_exp):
        tab_ref[base + e] = run[e]
        tab_ref[base + n_exp + e] = cpad[e]
        run_smem[e] = run[e] + cpad[e]

    def group_copy(e, slot_, loc_e, run_e):
        src = buf_scr.at[slot_, pl.ds(pl.multiple_of(loc_e, MOE_PAD), MOE_TILE)]
        dst = sorted_ref.at[e, pl.ds(pl.multiple_of(run_e, MOE_PAD), MOE_TILE)]
        return pltpu.make_async_copy(src, dst, sems.at[slot_, e])

    @pl.when(b > 0)
    def _():
        for e in range(n_exp):
            group_copy(e, 1 - slot, 0, 0).wait()

    for e in range(n_exp):
        group_copy(e, slot, loc[e], run[e]).start()

    @pl.when(b == n_blocks - 1)
    def _():
        fin = (n_blocks) * (2 * n_exp)
        for e in range(n_exp):
            group_copy(e, slot, 0, 0).wait()
        for e in range(n_exp):
            tot = run[e] + cpad[e]
            tab_ref[fin + e] = tot
            tab_ref[fin + n_exp + e] = 0
            dst = sorted_ref.at[e, pl.ds(pl.multiple_of(tot, MOE_PAD), MOE_TILE)]
            pltpu.make_async_copy(zero_scr, dst, sems.at[slot, e]).start()
        for e in range(n_exp):
            dst = sorted_ref.at[e, pl.ds(0, MOE_TILE)]
            pltpu.make_async_copy(zero_scr, dst, sems.at[slot, e]).wait()


def _moe_expert_kernel(te_ref, tr_ref, na_ref, xs_ref, wg_ref, wu_ref, wo_ref, y_ref, acc_scr, *, n_f):
    i = pl.program_id(0)
    f = pl.program_id(1)

    @pl.when(i < na_ref[0])
    def _():
        @pl.when(f == 0)
        def _():
            acc_scr[...] = jnp.zeros_like(acc_scr)

        xs = xs_ref[0]
        gv = jnp.dot(xs, wg_ref[0], preferred_element_type=F32)
        uv = jnp.dot(xs, wu_ref[0], preferred_element_type=F32)
        a = (_silu(gv) * uv).astype(BF16)
        acc_scr[...] += jnp.dot(a, wo_ref[0], preferred_element_type=F32)

        @pl.when(f == n_f - 1)
        def _():
            y_ref[0] = acc_scr[...].astype(y_ref.dtype)


def _moe_combine_kernel(tab_ref, end_ref, x_ref, meta_ref, gate_ref, gf_ref, ys_ref, o_ref,
                        win_scr, acc_scr, tr_scr, sems, *, bsz, n_exp):
    b = pl.program_id(0)
    slot = lax.rem(b, 2)
    n_cols, n_rows, _, dm = x_ref.shape
    rows = n_cols * n_rows * bsz
    n_win = MOE_TILE // MOE_WIN

    def window(blk, slot_, e, c):
        base = blk * (2 * n_exp)
        run_e = tab_ref[base + e]
        live = tab_ref[base + n_exp + e] > c * MOE_WIN
        ws = jnp.minimum(run_e + c * MOE_WIN, end_ref[e] - MOE_WIN)
        k = e * n_win + c
        cp = pltpu.make_async_copy(ys_ref.at[e, pl.ds(pl.multiple_of(ws, MOE_PAD), MOE_WIN)],
                                   win_scr.at[slot_, k], sems.at[slot_, k])
        return live, ws, k, cp

    def fetch(blk, slot_):
        for e in range(n_exp):
            for c in range(n_win):
                live, _, _, cp = window(blk, slot_, e, c)

                @pl.when(live)
                def _():
                    cp.start()

    @pl.when(b == 0)
    def _():
        fetch(0, 0)

    fetch(b + 1, 1 - slot)

    meta = meta_ref[...]
    e1, e2 = meta[:, 0:1], meta[:, 1:2]
    pos1, pos2 = meta[:, 2:3].astype(jnp.int32), meta[:, 3:4].astype(jnp.int32)
    g1, g2 = meta[:, 4:5], meta[:, 5:6]
    acc_scr[...] = jnp.zeros_like(acc_scr)
    lane = lax.broadcasted_iota(jnp.int32, (rows, MOE_WIN), 1)

    for e in range(n_exp):
        for c in range(n_win):
            live, ws, k, cp = window(b, slot, e, c)

            @pl.when(live)
            def _():
                cp.wait()
                rel = jnp.where(e1 == e, pos1, pos2) - ws
                hit = jnp.where(e1 == e, 1.0, jnp.where(e2 == e, 1.0, 0.0))
                pe = jnp.where(rel == lane, hit, 0.0).astype(BF16)
                ge = jnp.where(e1 == e, g1, jnp.where(e2 == e, g2, 0.0))
                acc_scr[...] += ge * jnp.dot(pe, win_scr[slot, k], preferred_element_type=F32)

    upd = gate_ref[...][None] * acc_scr[...].reshape(rows // bsz, bsz, dm)
    xo = x_ref[...].reshape(rows, dm) + upd.reshape(rows, dm)
    _fill_lane_tiles(_rms(xo, gf_ref[...]), tr_scr)
    for rl in range(n_rows):
        _strided_rows_out(
            tr_scr, lambda bb, lt: o_ref.at[bb, rl, :, lt * LANES:(lt + 1) * LANES],
            n_groups=bsz, group_rows=n_cols, stride=n_rows * bsz, base=rl * bsz)


def _moe(xt, g, shift, scale, gate, router_w, router_b, w_in, w_out, g_final, *, bsz, grid_rows):
    m, d = xt.shape
    n_exp, f, _ = w_out.shape
    nb = m // MOE_TB
    cap = _moe_cap(m, nb)
    local_rows = -(-(TOP_K * MOE_TB + n_exp * (MOE_PAD - 1)) // 128) * 128
    full1 = lambda i: (0, 0)

    blk_cols = 8
    blk_rows = MOE_TB // (blk_cols * bsz)
    assert blk_rows * blk_cols * bsz == MOE_TB and grid_rows % blk_rows == 0 and GRID_W % blk_cols == 0
    nrg = grid_rows // blk_rows
    x4 = xt.reshape(GRID_W, grid_rows, bsz, d)
    x_spec1 = pl.BlockSpec((blk_cols, blk_rows, bsz, d), lambda i: (i // nrg, i % nrg, 0, 0))

    ut = (jnp.arange(MOE_TB)[:, None] < jnp.arange(MOE_TB)[None, :]).astype(BF16)
    route = functools.partial(_moe_route_kernel, bsz=bsz, n_exp=n_exp, n_blocks=nb, local_rows=local_rows)
    sorted_h, meta, tab = pl.pallas_call(
        route,
        grid=(nb,),
        in_specs=[x_spec1,
                  pl.BlockSpec((1, d), full1),
                  pl.BlockSpec((bsz, d), full1),
                  pl.BlockSpec((bsz, d), full1),
                  pl.BlockSpec((n_exp, d), full1),
                  pl.BlockSpec((n_exp, 1), full1),
                  pl.BlockSpec((MOE_TB, MOE_TB), full1)],
        out_specs=[pl.BlockSpec(memory_space=pl.ANY),
                   pl.BlockSpec((MOE_TB, MOE_LANES), lambda i: (i, 0)),
                   pl.BlockSpec(memory_space=pltpu.SMEM)],
        out_shape=[jax.ShapeDtypeStruct((n_exp, cap, d), BF16),
                   jax.ShapeDtypeStruct((m, MOE_LANES), F32),
                   jax.ShapeDtypeStruct(((nb + 1) * 2 * n_exp,), jnp.int32)],
        scratch_shapes=[pltpu.VMEM((2, local_rows + MOE_TILE, d), BF16),
                        pltpu.VMEM((MOE_TILE, d), BF16),
                        pltpu.SMEM((n_exp,), jnp.int32),
                        pltpu.SemaphoreType.DMA((2, n_exp))],
        compiler_params=_params("arbitrary"),
        name="moe_route",
    )(x4, g.reshape(1, d), shift, scale, router_w.T, router_b.reshape(n_exp, 1), ut)

    tot = tab[nb * 2 * n_exp: nb * 2 * n_exp + n_exp]
    ntile = jnp.maximum((tot + MOE_TILE - 1) // MOE_TILE, 1)
    cum = jnp.cumsum(ntile)
    n_active = cum[-1:]
    max_tiles = (TOP_K * m + nb * n_exp * (MOE_PAD - 1)) // MOE_TILE + n_exp + 1
    idc = jnp.minimum(jnp.arange(max_tiles, dtype=jnp.int32), n_active - 1)
    tile_e = jnp.sum(idc[:, None] >= cum[None, :], axis=1).astype(jnp.int32)
    tile_r = (idc - (cum - ntile)[tile_e]).astype(jnp.int32)
    end = (ntile * MOE_TILE).astype(jnp.int32)

    tf = f // 2 if (f // 2) % 128 == 0 else f
    n_f = f // tf

    def f_idx(i, j, na):
        return jnp.where(i < na[0], j, n_f - 1)

    ys = pl.pallas_call(
        functools.partial(_moe_expert_kernel, n_f=n_f),
        grid_spec=pltpu.PrefetchScalarGridSpec(
            num_scalar_prefetch=3,
            grid=(max_tiles, n_f),
            in_specs=[pl.BlockSpec((1, MOE_TILE, d), lambda i, j, te, tr, na: (te[i], tr[i], 0)),
                      pl.BlockSpec((1, d, tf), lambda i, j, te, tr, na: (te[i], 0, f_idx(i, j, na))),
                      pl.BlockSpec((1, d, tf), lambda i, j, te, tr, na: (te[i], 0, n_f + f_idx(i, j, na))),
                      pl.BlockSpec((1, tf, d), lambda i, j, te, tr, na: (te[i], f_idx(i, j, na), 0))],
            out_specs=pl.BlockSpec((1, MOE_TILE, d), lambda i, j, te, tr, na: (te[i], tr[i], 0)),
            scratch_shapes=[pltpu.VMEM((MOE_TILE, d), F32)]),
        out_shape=jax.ShapeDtypeStruct((n_exp, cap, d), BF16),
        compiler_params=_params("arbitrary", "arbitrary"),
        name="moe_experts",
    )(tile_e, tile_r, n_active.astype(jnp.int32), sorted_h, w_in, w_in, w_out)

    n_win = MOE_TILE // MOE_WIN
    out = pl.pallas_call(
        functools.partial(_moe_combine_kernel, bsz=bsz, n_exp=n_exp),
        grid_spec=pltpu.PrefetchScalarGridSpec(
            num_scalar_prefetch=2,
            grid=(nb,),
            in_specs=[pl.BlockSpec((blk_cols, blk_rows, bsz, d), lambda i, tb, en: (i // nrg, i % nrg, 0, 0)),
                      pl.BlockSpec((MOE_TB, MOE_LANES), lambda i, tb, en: (i, 0)),
                      pl.BlockSpec((bsz, d), lambda i, tb, en: (0, 0)),
                      pl.BlockSpec((1, d), lambda i, tb, en: (0, 0)),
                      pl.BlockSpec(memory_space=pl.ANY)],
            out_specs=pl.BlockSpec((bsz, blk_rows, blk_cols, d), lambda i, tb, en: (0, i % nrg, i // nrg, 0)),
            scratch_shapes=[pltpu.VMEM((2, n_exp * n_win, MOE_WIN, d), BF16),
                            pltpu.VMEM((MOE_TB, d), F32),
                            pltpu.VMEM((d // LANES, MOE_TB, LANES), F32),
                            pltpu.SemaphoreType.DMA((2, n_exp * n_win))]),
        out_shape=jax.ShapeDtypeStruct((bsz, grid_rows, GRID_W, d), F32),
        compiler_params=_params("arbitrary"),
        name="moe_combine",
    )(tab, end, x4, meta, gate, g_final.reshape(1, d), ys)
    return out.reshape(bsz, grid_rows * GRID_W, d)


def _blockdiag(w, n_gb):
    two, g, r, c = w.shape
    eye = jnp.eye(S5_GB, dtype=w.dtype)
    out = jnp.einsum("dbgrc,gh->dbgrhc", w.reshape(two, n_gb, S5_GB, r, c), eye)
    return out.reshape(two, n_gb, S5_GB * r, S5_GB * c)


def kernel(x, c, ctx, c_ctx, ada_w, ada_b, norm_mix, norm_ffn, ssd_w_in, ssd_conv_w, ssd_conv_b, ssd_dt_bias, ssd_a_log, ssd_d, ssd_norm, ssd_w_out, s5_lam_re, s5_lam_im, s5_log_step, s5_b_re, s5_b_im, s5_c_re, s5_c_im, s5_d, s5_w_glu, s5_b_glu, ffn_w_in, ffn_w_out, moe_router_w, moe_router_b, moe_w_in, moe_w_out, norm_final):
    bsz, seq, d = x.shape
    ctx_len = ctx.shape[1]
    depth = ada_w.shape[0]
    assert depth == 2, "one SSD layer followed by one S5 layer"
    rows = seq // GRID_W

    pad_rows = (-(bsz + 1)) % 8
    cc = jnp.concatenate([c, c_ctx[None], jnp.zeros((pad_rows, d), F32)], axis=0)
    mods = _ada(cc, ada_w, ada_b)

    def mod(i, n):
        sl = slice(n * d, (n + 1) * d)
        return mods[i, :bsz, sl], mods[i, bsz:bsz + 1, sl]

    heads = ssd_d.shape[1]
    d_inner = heads * SSD_HEADDIM
    conv_dim = ssd_conv_w.shape[2]
    n_main = d_inner + conv_dim
    w_in0 = ssd_w_in[0]
    w_main = w_in0[:, :n_main].astype(BF16)
    w_dt = w_in0[:, n_main:]
    conv_w_full = jnp.concatenate([jnp.zeros((SSD_CONV, d_inner), F32), 0.5 * ssd_conv_w[0]], axis=1)
    conv_b_full = jnp.concatenate([jnp.zeros((d_inner,), F32), 0.5 * ssd_conv_b[0]])
    d_exp = jnp.repeat(ssd_d[0], SSD_HEADDIM)
    w_out0 = ssd_w_out[0].astype(BF16)
    gw = (heads // SSD_GROUPS) * SSD_HEADDIM

    (sh_x, sh_c), (sc_x, sc_c), (gt_x, gt_c) = mod(0, 0), mod(0, 1), mod(0, 2)
    r3 = lambda a: a[:, None, :]
    dt_bias = ssd_dt_bias[0].reshape(-1)
    a_log = ssd_a_log[0].reshape(-1)

    def ssd_mixer(tok3, shift, scale, gate, h0):
        b_, t_, _ = tok3.shape
        zx, dtc, dtcT = _ssd_inproj(tok3, norm_mix[0], r3(shift), r3(scale), w_main, conv_w_full,
                                    conv_b_full, w_dt, dt_bias, a_log, d_inner=d_inner, heads=heads)
        y, hfin = _ssd_scan(zx, dtc, dtcT, h0, d_inner=d_inner, heads=heads)
        out = _ssd_out(y.reshape(2, b_ * t_, d_inner), zx.reshape(b_ * t_, n_main), d_exp, ssd_norm[0],
                       w_out0, tok3.reshape(b_ * t_, d), r3(gate), seq=t_, d_inner=d_inner)
        return out, hfin

    h_zero = jnp.zeros((bsz, 2, SSD_GROUPS, SSD_STATE, gw), F32)
    ctx2, h_ctx = ssd_mixer(ctx, sh_c, sc_c, gt_c, h_zero)
    x2, _ = ssd_mixer(x, sh_x, sc_x, gt_x, h_ctx)

    ffn_in = ffn_w_in[0].astype(BF16)
    ffn_out = ffn_w_out[0].astype(BF16)
    (sh_x, sh_c), (sc_x, sc_c), (gt_x, gt_c) = mod(0, 3), mod(0, 4), mod(0, 5)
    xt = _ffn(x2.reshape(bsz, seq, d), norm_ffn[0], r3(sh_x), r3(sc_x), r3(gt_x), ffn_in, ffn_out,
              time_major=True).reshape(seq * bsz, d)
    ctx2 = _ffn(ctx2.reshape(bsz, ctx_len, d), norm_ffn[0], r3(sh_c), r3(sc_c), r3(gt_c), ffn_in, ffn_out)

    ct = ctx2.reshape(bsz, ctx_len, d).transpose(1, 0, 2).reshape(ctx_len * bsz, d)

    ar, ai, bbr, bbi = _s5_discretize(s5_lam_re[0], s5_lam_im[0], s5_log_step[0], s5_b_re[0], s5_b_im[0])
    n_groups = ar.shape[1]
    n_gb = n_groups // S5_GB
    wb = jnp.concatenate([_blockdiag(bbr.transpose(0, 1, 3, 2), n_gb),
                          _blockdiag(bbi.transpose(0, 1, 3, 2), n_gb)], axis=-1).astype(BF16)
    wc = jnp.concatenate([_blockdiag(s5_c_re[0].transpose(0, 1, 3, 2), n_gb),
                          _blockdiag(-s5_c_im[0].transpose(0, 1, 3, 2), n_gb)], axis=2).astype(BF16)
    a_coef = jnp.stack([ar.reshape(2, -1), ai.reshape(2, -1)], axis=1)
    nstate = a_coef.shape[-1]

    (sh_x, sh_c), (sc_x, sc_c), (gt_x, _) = mod(1, 0), mod(1, 1), mod(1, 2)
    bc = lambda a: jnp.broadcast_to(a, (bsz, d))
    s_zero = jnp.zeros((2, 2, bsz, nstate), F32)
    _, s_ctx = _s5_scan(ct, norm_mix[1], bc(sh_c), bc(sc_c), wb, wc, a_coef, s_zero, bsz=bsz)
    n_exp, _, two_f = moe_w_in.shape[1:]
    y, _, w_in_b, w_out_b = _s5_scan(xt, norm_mix[1], sh_x, sc_x, wb, wc, a_coef, s_ctx, bsz=bsz,
                                     cast=(moe_w_in[0].reshape(n_exp * d, two_f),
                                           moe_w_out[0].reshape(n_exp * (two_f // 2), d)))
    w_in_b = w_in_b.reshape(n_exp, d, two_f)
    w_out_b = w_out_b.reshape(n_exp, two_f // 2, d)
    xt = _s5_glu(xt, y, norm_mix[1], sh_x, sc_x, gt_x, s5_d[0], s5_w_glu[0].astype(BF16), s5_b_glu[0], bsz=bsz)

    (sh_x, _), (sc_x, _), (gt_x, _) = mod(1, 3), mod(1, 4), mod(1, 5)
    return _moe(xt, norm_ffn[1], sh_x, sc_x, gt_x, moe_router_w[0], moe_router_b[0],
                w_in_b, w_out_b, norm_final, bsz=bsz, grid_rows=rows)
```

```python
import functools

import jax
import jax.numpy as jnp
from jax import lax
from jax.experimental import pallas as pl
from jax.experimental.pallas import tpu as pltpu

F32 = jnp.float32
BF16 = jnp.bfloat16
HIGHEST = lax.Precision.HIGHEST

EPS = 1e-6
GRID_W = 64
N_MOD = 6

SSD_HEADDIM = 64
SSD_GROUPS = 4
SSD_STATE = 128
SSD_CONV = 5
SSD_CHUNK = 128

S5_CH = 16
S5_STATE = 64
S5_GB = 8

TOP_K = 2

V7X_VMEM_BYTES = 64 * 1024 * 1024
VMEM_LIMIT = V7X_VMEM_BYTES - 4 * 1024 * 1024
NEG_BIG = -1e30
LANES = 128
LOG2E = 1.4426950408889634


TR_SKEW = 8


def _tr_rows(n_groups, group):
    return n_groups * (group + TR_SKEW)


def _fill_lane_tiles(res, tr_scr, group):
    pitch = group + TR_SKEW
    for lt in range(res.shape[1] // LANES):
        for g in range(res.shape[0] // group):
            tr_scr[lt, g * pitch:g * pitch + group, :] = res[g * group:(g + 1) * group, lt * LANES:(lt + 1) * LANES]


def _strided_rows_out(tr_scr, dst, *, n_groups, group_rows, stride, base=0):
    for g in range(n_groups):
        for lt in range(tr_scr.shape[0]):
            dst(g, lt)[...] = tr_scr[lt, pl.ds(base + g, group_rows, stride=stride), :]


def _params(*sem):
    return pltpu.CompilerParams(dimension_semantics=sem, vmem_limit_bytes=VMEM_LIMIT)


def _silu(v):
    return v * jax.nn.sigmoid(v)


def _softplus(v):
    return jnp.maximum(v, 0.0) + jnp.log1p(jnp.exp(-jnp.abs(v)))


def _rms(x, g):
    inv = lax.rsqrt(jnp.mean(x * x, axis=-1, keepdims=True) + EPS)
    return (x * inv) * g


def _nt_dot(a, b, precision=None):
    return lax.dot_general(a, b, (((1,), (1,)), ((), ())), precision=precision,
                           preferred_element_type=F32)


def _ada_kernel(c_ref, w_ref, b_ref, o_ref):
    s = _silu(c_ref[...])
    o_ref[0] = jnp.dot(s, w_ref[0], precision=HIGHEST, preferred_element_type=F32) + b_ref[0]


def _ada(cc, ada_w, ada_b):
    depth, d, n = ada_w.shape
    r = cc.shape[0]
    tn = 1536
    return pl.pallas_call(
        _ada_kernel,
        grid=(depth, n // tn),
        in_specs=[pl.BlockSpec((r, d), lambda i, j: (0, 0)),
                  pl.BlockSpec((1, d, tn), lambda i, j: (i, 0, j)),
                  pl.BlockSpec((1, 1, tn), lambda i, j: (i, 0, j))],
        out_specs=pl.BlockSpec((1, r, tn), lambda i, j: (i, 0, j)),
        out_shape=jax.ShapeDtypeStruct((depth, r, n), F32),
        compiler_params=_params("arbitrary", "arbitrary"),
        name="ada_mod",
    )(cc, ada_w, ada_b.reshape(depth, 1, n))


def _ssd_inproj_kernel(x_ref, g_ref, sh_ref, sc_ref, w_ref, cw_ref, cb_ref, wdt_ref, wdtT_ref,
                       bdt_ref, bdtT_ref, alog_ref, alogT_ref,
                       o_ref, dt_ref, dtT_ref, h_scr, pad_scr, *, n_plain, seq, heads, row_chunk, dt_chunk):
    j = pl.program_id(1)
    pad = 8

    @pl.when(j == 0)
    def _():
        for r in range(0, seq, dt_chunk):
            rs = slice(r, r + dt_chunk)
            h = _rms(x_ref[0, rs, :], g_ref[...]) * (1.0 + sc_ref[0]) + sh_ref[0]
            hb = h.astype(BF16)
            h_scr[rs, :] = hb
            hl = (h - hb.astype(F32)).astype(BF16)
            nh2 = 2 * heads
            p = jnp.dot(hb, wdt_ref[...], preferred_element_type=F32)
            q = jnp.dot(hl, wdt_ref[...], preferred_element_type=F32)
            dt = _softplus(p[:, :nh2] + (p[:, nh2:] + q[:, :nh2]) + bdt_ref[...])
            da = dt * (-jnp.exp(alog_ref[...]))
            pT = _nt_dot(wdtT_ref[...], hb)
            qT = _nt_dot(wdtT_ref[...], hl)
            dtT = _softplus(pT[:nh2, :] + (pT[nh2:, :] + qT[:nh2, :]) + bdtT_ref[...])
            daT = dtT * (-jnp.exp(alogT_ref[...]))
            for d in range(2):
                sl = slice(d * heads, (d + 1) * heads)
                dt_ref[d, 0, rs, :] = jnp.concatenate([dt[:, sl], da[:, sl]], axis=-1)
                dtT_ref[d, 0, :, rs] = jnp.concatenate([dtT[sl, :], daT[sl, :]], axis=0)
        zeros = jnp.zeros((pad, pad_scr.shape[1]), F32)
        pad_scr[0:pad, :] = zeros
        pad_scr[pad + seq:2 * pad + seq, :] = zeros

    acc = jnp.dot(h_scr[...], w_ref[...], preferred_element_type=F32)

    @pl.when(j < n_plain)
    def _():
        o_ref[0] = acc.astype(o_ref.dtype)

    @pl.when(j >= n_plain)
    def _():
        pad_scr[pad:pad + seq, :] = acc
        half = (SSD_CONV - 1) // 2
        n_all = row_chunk + 2 * pad
        for r in range(0, seq, row_chunk):
            a = pad_scr[r:r + n_all, :]
            s = cb_ref[...] + cw_ref[half:half + 1, :] * a[pad:pad + row_chunk, :]
            for k in range(SSD_CONV):
                if k != half:
                    shifted = pltpu.roll(a, (half - k) % n_all, axis=0)
                    s = s + cw_ref[k:k + 1, :] * shifted[pad:pad + row_chunk, :]
            o_ref[0, r:r + row_chunk, :] = (s * (1.0 + jnp.tanh(s))).astype(o_ref.dtype)


def _ssd_inproj(x3, g, shift, scale, w_main, conv_w_full, conv_b_full, w_dt, dt_bias, a_log,
                *, d_inner, heads):
    bsz, seq, d = x3.shape
    n = w_main.shape[1]
    tn = 512
    nh2 = 2 * heads
    per_batch = shift.shape[0] > 1
    mod_map = (lambda b, j: (b, 0, 0)) if per_batch else (lambda b, j: (0, 0, 0))
    kern = functools.partial(_ssd_inproj_kernel, n_plain=d_inner // tn, seq=seq, heads=heads,
                             row_chunk=min(64, seq), dt_chunk=min(256, seq))
    w_hi = w_dt.astype(BF16)
    w_dt2 = jnp.concatenate([w_hi, (w_dt - w_hi.astype(F32)).astype(BF16)], axis=1)
    const2 = lambda b, j: (0, 0)
    return pl.pallas_call(
        kern,
        grid=(bsz, n // tn),
        in_specs=[pl.BlockSpec((1, seq, d), lambda b, j: (b, 0, 0), pipeline_mode=pl.Buffered(1)),
                  pl.BlockSpec((1, d), const2),
                  pl.BlockSpec((1, 1, d), mod_map),
                  pl.BlockSpec((1, 1, d), mod_map),
                  pl.BlockSpec((d, tn), lambda b, j: (0, j)),
                  pl.BlockSpec((SSD_CONV, tn), lambda b, j: (0, j)),
                  pl.BlockSpec((1, tn), lambda b, j: (0, j)),
                  pl.BlockSpec((d, 2 * nh2), const2),
                  pl.BlockSpec((2 * nh2, d), const2),
                  pl.BlockSpec((1, nh2), const2),
                  pl.BlockSpec((nh2, 1), const2),
                  pl.BlockSpec((1, nh2), const2),
                  pl.BlockSpec((nh2, 1), const2)],
        out_specs=[pl.BlockSpec((1, seq, tn), lambda b, j: (b, 0, j)),
                   pl.BlockSpec((2, 1, seq, nh2), lambda b, j: (0, b, 0, 0)),
                   pl.BlockSpec((2, 1, nh2, seq), lambda b, j: (0, b, 0, 0))],
        out_shape=[jax.ShapeDtypeStruct((bsz, seq, n), BF16),
                   jax.ShapeDtypeStruct((2, bsz, seq, nh2), F32),
                   jax.ShapeDtypeStruct((2, bsz, nh2, seq), F32)],
        scratch_shapes=[pltpu.VMEM((seq, d), BF16), pltpu.VMEM((seq + 16, tn), F32)],
        compiler_params=_params("arbitrary", "arbitrary"),
        name="ssd_inproj",
    )(x3, g.reshape(1, d), shift, scale, w_main, conv_w_full, conv_b_full.reshape(1, n),
      w_dt2, w_dt2.T, dt_bias.reshape(1, nh2), dt_bias.reshape(nh2, 1),
      a_log.reshape(1, nh2), a_log.reshape(nh2, 1))


def _ssd_scan_kernel(xs_ref, b_ref, c_ref, dtc_ref, dtcT_ref, h0_ref, y_ref, hfin_ref,
                     st_scr, *, heads, n_chunks):
    d = pl.program_id(1)
    k = pl.program_id(2)
    L = xs_ref.shape[1]
    hpg = heads // SSD_GROUPS
    pairs_per_group = hpg // 2
    gw = hpg * SSD_HEADDIM

    @pl.when(k == 0)
    def _():
        st_scr[...] = h0_ref[0, 0]

    row = lax.broadcasted_iota(jnp.int32, (L, L), 0)
    col = lax.broadcasted_iota(jnp.int32, (L, L), 1)
    mask = jnp.where(d == 0, row - col, col - row) >= 0
    mask_f = mask.astype(F32)

    dtc = dtc_ref[0, 0]
    dtcT = dtcT_ref[0, 0]
    assert L == 2 * SSD_HEADDIM, "the head-pair tiles assume chunk length == 2 * head dim"
    da = dtc[:, heads:]
    dtT, daT = dtcT[:heads, :], dtcT[heads:, :]
    cum = jnp.dot(mask_f, da, precision=HIGHEST, preferred_element_type=F32)
    cumT = _nt_dot(daT, mask_f, precision=HIGHEST)
    tot = jnp.sum(da, axis=0, keepdims=True)
    totT = jnp.sum(daT, axis=1, keepdims=True)
    etot = jnp.exp(tot)
    c2 = cum * LOG2E
    c2T = cumT * LOG2E
    wrow = jnp.exp(totT - cumT) * dtT

    lane = lax.broadcasted_iota(jnp.int32, (L, 2 * SSD_HEADDIM), 1)
    lo = lane < SSD_HEADDIM
    lo_row = lax.broadcasted_iota(jnp.int32, (1, 2 * SSD_HEADDIM), 1) < SSD_HEADDIM

    for g in range(SSD_GROUPS):
        bg = b_ref[0, :, g * SSD_STATE:(g + 1) * SSD_STATE]
        cg = c_ref[0, :, g * SSD_STATE:(g + 1) * SSD_STATE]
        scores = _nt_dot(cg, bg) * mask_f
        bgT = bg.astype(F32).T
        yoff = jnp.dot(cg, st_scr[g].astype(BF16), preferred_element_type=F32)
        for q in range(pairs_per_group):
            p = g * pairs_per_group + q
            h0, h1 = 2 * p, 2 * p + 1
            xs2 = xs_ref[0, :, p * 128:(p + 1) * 128]
            zero = jnp.zeros_like(xs2)
            rhs = jnp.concatenate([jnp.where(lo, xs2, zero), jnp.where(lo, zero, xs2)], axis=0)
            ms, bs, es = [], [], []
            for h in (h0, h1):
                col = jnp.broadcast_to(c2[:, h:h + 1], (L, L))
                dec = jnp.exp2(jnp.minimum(col - c2T[h:h + 1, :], 0.0))
                ms.append((scores * dec * dtT[h:h + 1, :]).astype(BF16))
                bs.append((bgT * wrow[h:h + 1, :]).astype(BF16))
                es.append(jnp.exp2(col))
            ydiag = jnp.dot(jnp.concatenate(ms, axis=1), rhs, preferred_element_type=F32)
            new = jnp.dot(jnp.concatenate(bs, axis=1), rhs, preferred_element_type=F32)
            sl = slice(q * 128, (q + 1) * 128)
            y2 = ydiag + yoff[:, sl] * jnp.where(lo, es[0], es[1])
            y_ref[0, 0, :, p * 128:(p + 1) * 128] = y2.astype(y_ref.dtype)
            et2 = jnp.where(lo_row, etot[:, h0:h0 + 1], etot[:, h1:h1 + 1])
            st_scr[g, :, sl] = st_scr[g, :, sl] * et2 + new

    @pl.when(k == n_chunks - 1)
    def _():
        hfin_ref[0, 0] = st_scr[...]


def _ssd_scan(zx, dtc, dtcT, h0, *, d_inner, heads):
    bsz, seq, _ = zx.shape
    L = min(SSD_CHUNK, seq)
    nc = seq // L
    bc_w = SSD_GROUPS * SSD_STATE
    gw = (heads // SSD_GROUPS) * SSD_HEADDIM
    nh2 = 2 * heads
    xs_blk = d_inner // d_inner
    b_blk = (2 * d_inner) // bc_w
    c_blk = b_blk + 1

    def chunk(d, k):
        return k + d * (nc - 1 - 2 * k)

    kern = functools.partial(_ssd_scan_kernel, heads=heads, n_chunks=nc)
    st_shape = (SSD_GROUPS, SSD_STATE, gw)
    return pl.pallas_call(
        kern,
        grid=(bsz, 2, nc),
        in_specs=[pl.BlockSpec((1, L, d_inner), lambda b, d, k: (b, chunk(d, k), xs_blk)),
                  pl.BlockSpec((1, L, bc_w), lambda b, d, k: (b, chunk(d, k), b_blk)),
                  pl.BlockSpec((1, L, bc_w), lambda b, d, k: (b, chunk(d, k), c_blk)),
                  pl.BlockSpec((1, 1, L, nh2), lambda b, d, k: (d, b, chunk(d, k), 0)),
                  pl.BlockSpec((1, 1, nh2, L), lambda b, d, k: (d, b, 0, chunk(d, k))),
                  pl.BlockSpec((1, 1) + st_shape, lambda b, d, k: (b, d, 0, 0, 0))],
        out_specs=[pl.BlockSpec((1, 1, L, d_inner), lambda b, d, k: (d, b, chunk(d, k), 0)),
                   pl.BlockSpec((1, 1) + st_shape, lambda b, d, k: (b, d, 0, 0, 0))],
        out_shape=[jax.ShapeDtypeStruct((2, bsz, seq, d_inner), BF16),
                   jax.ShapeDtypeStruct((bsz, 2) + st_shape, F32)],
        scratch_shapes=[pltpu.VMEM(st_shape, F32)],
        compiler_params=_params("arbitrary", "arbitrary", "arbitrary"),
        name="ssd_scan",
    )(zx, zx, zx, dtc, dtcT, h0)


def _ssd_out_kernel(y_ref, z_ref, xs_ref, dsk_ref, nw_ref, w_ref, x_ref, gate_ref, o_ref):
    y = y_ref[0].astype(F32) + y_ref[1].astype(F32) + dsk_ref[...] * xs_ref[...].astype(F32)
    yg = y * _silu(z_ref[...].astype(F32))
    yn = _rms(yg, nw_ref[...]).astype(BF16)
    out = jnp.dot(yn, w_ref[...], preferred_element_type=F32)
    o_ref[...] = x_ref[...] + gate_ref[0] * out


def _ssd_out(y, zx, d_exp, norm_w, w_out, x2, gate, *, seq, d_inner):
    m, d = x2.shape
    tm = min(512, seq)
    bpt = seq // tm
    per_batch = gate.shape[0] > 1
    gate_map = (lambda i: (i // bpt, 0, 0)) if per_batch else (lambda i: (0, 0, 0))
    return pl.pallas_call(
        _ssd_out_kernel,
        grid=(m // tm,),
        in_specs=[pl.BlockSpec((2, tm, d_inner), lambda i: (0, i, 0)),
                  pl.BlockSpec((tm, d_inner), lambda i: (i, 0)),
                  pl.BlockSpec((tm, d_inner), lambda i: (i, 1)),
                  pl.BlockSpec((1, d_inner), lambda i: (0, 0)),
                  pl.BlockSpec((1, d_inner), lambda i: (0, 0)),
                  pl.BlockSpec((d_inner, d), lambda i: (0, 0)),
                  pl.BlockSpec((tm, d), lambda i: (i, 0)),
                  pl.BlockSpec((1, 1, d), gate_map)],
        out_specs=pl.BlockSpec((tm, d), lambda i: (i, 0)),
        out_shape=jax.ShapeDtypeStruct((m, d), F32),
        compiler_params=_params("arbitrary"),
        name="ssd_out",
    )(y, zx, zx, d_exp.reshape(1, d_inner), norm_w.reshape(1, d_inner), w_out, x2, gate)


def _ffn_kernel(x_ref, g_ref, sh_ref, sc_ref, gate_ref, wg_ref, wu_ref, wo_ref, o_ref,
                h_scr, acc_scr, tr_scr, *, n_f, time_major):
    j = pl.program_id(1)
    bb, tt, dm = x_ref.shape

    @pl.when(j == 0)
    def _():
        h = _rms(x_ref[...], g_ref[...]) * (1.0 + sc_ref[...]) + sh_ref[...]
        h_scr[...] = h.reshape(bb * tt, dm).astype(BF16)
        acc_scr[...] = jnp.zeros_like(acc_scr)

    h = h_scr[...]
    gv = jnp.dot(h, wg_ref[...], preferred_element_type=F32)
    uv = jnp.dot(h, wu_ref[...], preferred_element_type=F32)
    a = (_silu(gv) * uv).astype(BF16)
    acc_scr[...] += jnp.dot(a, wo_ref[...], preferred_element_type=F32)

    @pl.when(j == n_f - 1)
    def _():
        res = x_ref[...] + gate_ref[...] * acc_scr[...].reshape(bb, tt, dm)
        if time_major:
            _fill_lane_tiles(res.reshape(bb * tt, dm), tr_scr, tt)
            _strided_rows_out(tr_scr, lambda c, lt: o_ref.at[c, 0, :, lt * LANES:(lt + 1) * LANES],
                              n_groups=tt, group_rows=bb, stride=tt + TR_SKEW)
        else:
            o_ref[...] = res


def _ffn(x3, g, shift, scale, gate, w_in, w_out, *, time_major=False):
    bsz, seq, d = x3.shape
    f = w_out.shape[0]
    tf = f // 2 if (f // 2) % 128 == 0 else f
    n_f = f // tf
    per_batch = shift.shape[0] > 1
    if time_major:
        bb, tt = 8, GRID_W
        assert bsz % bb == 0 and seq % tt == 0 and per_batch
        rows = seq // tt
        x_map = lambda i, j: (i // rows, i % rows, 0)
        mod_map = lambda i, j: (i // rows, 0, 0)
        out_spec = pl.BlockSpec((tt, 1, bb, d), lambda i, j: (0, i % rows, i // rows, 0))
        out_shape = jax.ShapeDtypeStruct((tt, rows, bsz, d), F32)
        n_blocks = (bsz // bb) * rows
    else:
        bb, tt = 1, min(512, seq)
        bpt = seq // tt
        x_map = lambda i, j: (i // bpt, i % bpt, 0)
        mod_map = (lambda i, j: (i // bpt, 0, 0)) if per_batch else (lambda i, j: (0, 0, 0))
        out_spec = pl.BlockSpec((bb, tt, d), x_map)
        out_shape = jax.ShapeDtypeStruct((bsz, seq, d), F32)
        n_blocks = bsz * bpt
    mb = bb if per_batch else 1
    tm = bb * tt
    kern = functools.partial(_ffn_kernel, n_f=n_f, time_major=time_major)
    return pl.pallas_call(
        kern,
        grid=(n_blocks, n_f),
        in_specs=[pl.BlockSpec((bb, tt, d), x_map),
                  pl.BlockSpec((1, d), lambda i, j: (0, 0)),
                  pl.BlockSpec((mb, 1, d), mod_map),
                  pl.BlockSpec((mb, 1, d), mod_map),
                  pl.BlockSpec((mb, 1, d), mod_map),
                  pl.BlockSpec((d, tf), lambda i, j: (0, j)),
                  pl.BlockSpec((d, tf), lambda i, j: (0, n_f + j)),
                  pl.BlockSpec((tf, d), lambda i, j: (j, 0))],
        out_specs=out_spec,
        out_shape=out_shape,
        scratch_shapes=[pltpu.VMEM((tm, d), BF16), pltpu.VMEM((tm, d), F32),
                        pltpu.VMEM((d // LANES, _tr_rows(bb, tt) if time_major else 8, LANES), F32)],
        compiler_params=_params("arbitrary", "arbitrary"),
        name="dense_swiglu",
    )(x3, g.reshape(1, d), shift, scale, gate, w_in, w_in, w_out)


def _s5_disc_kernel(lre_ref, lim_ref, step_ref, bre_ref, bim_ref, ar_ref, ai_ref, bbr_ref, bbi_ref):
    step = jnp.exp(step_ref[...])
    lre, lim = lre_ref[...], lim_ref[...]
    mag = jnp.exp(lre * step)
    ar = mag * jnp.cos(lim * step)
    ai = mag * jnp.sin(lim * step)
    den = lre * lre + lim * lim
    cr = ((ar - 1.0) * lre + ai * lim) / den
    ci = (ai * lre - (ar - 1.0) * lim) / den
    ar_ref[...] = ar
    ai_ref[...] = ai
    bbr_ref[...] = cr * bre_ref[...] - ci * bim_ref[...]
    bbi_ref[...] = cr * bim_ref[...] + ci * bre_ref[...]


def _s5_discretize(lam_re, lam_im, log_step, b_re, b_im):
    two, g, p = lam_re.shape
    c = b_re.shape[-1]
    lre = jnp.repeat(lam_re, c, axis=-1)
    lim = jnp.repeat(lam_im, c, axis=-1)
    shp = jax.ShapeDtypeStruct((two, g, p * c), F32)
    ar, ai, bbr, bbi = pl.pallas_call(
        _s5_disc_kernel, out_shape=[shp, shp, shp, shp], name="s5_discretize",
    )(lre, lim, log_step.reshape(two, g, 1), b_re.reshape(two, g, p * c), b_im.reshape(two, g, p * c))
    ar = ar.reshape(two, g, p, c)[..., 0]
    ai = ai.reshape(two, g, p, c)[..., 0]
    return ar, ai, bbr.reshape(two, g, p, c), bbi.reshape(two, g, p, c)


def _s5_kernel(x_ref, g_ref, sh_ref, sc_ref, wb_ref, wc_ref, a_ref, s0_ref, *rest, n_chunks, steps, bsz, n_cast):
    cast_in, (y_ref, sfin_ref), cast_out = rest[:n_cast], rest[n_cast:n_cast + 2], rest[n_cast + 2:2 * n_cast + 2]
    st_scr, bu0_scr, bu1_scr, sb0_scr, sb1_scr = rest[2 * n_cast + 2:]
    for src, dst in zip(cast_in, cast_out):
        dst[...] = src[...].astype(dst.dtype)
    bu_bufs = (bu0_scr, bu1_scr)
    sb_bufs = (sb0_scr, sb1_scr)
    d = pl.program_id(0)
    k = pl.program_id(1)
    rows, dm = x_ref.shape
    gl = S5_GB * S5_STATE
    cb = S5_GB * S5_CH

    @pl.when(k == 0)
    def _():
        st_scr[...] = s0_ref[0]

    hn = _rms(x_ref[...], g_ref[...]).reshape(steps, bsz, dm)
    h = hn * (1.0 + sc_ref[...])[None] + sh_ref[...][None]
    u = h.reshape(rows, dm).astype(BF16)

    n_gb = dm // cb

    def bu_dot(gb):
        return jnp.dot(u[:, gb * cb:(gb + 1) * cb], wb_ref[0, gb], preferred_element_type=F32)

    def sweep(reverse):
        bu_bufs[0][...] = bu_dot(0)
        for gb in range(n_gb):
            bu_cur, bu_nxt = bu_bufs[gb % 2], bu_bufs[1 - gb % 2]
            sb_cur = sb_bufs[gb % 2]
            if gb + 1 < n_gb:
                bu_nxt[...] = bu_dot(gb + 1)
            sl = slice(gb * gl, (gb + 1) * gl)
            ar = jnp.broadcast_to(a_ref[0, 0:1, sl], (bsz, gl))
            ai = jnp.broadcast_to(a_ref[0, 1:2, sl], (bsz, gl))
            sr, si = st_scr[0, :, sl], st_scr[1, :, sl]
            for i in range(steps):
                r0 = (steps - 1 - i if reverse else i) * bsz
                bur = bu_cur[r0:r0 + bsz, 0:gl]
                bui = bu_cur[r0:r0 + bsz, gl:2 * gl]
                sr, si = ar * sr - ai * si + bur, ar * si + ai * sr + bui
                sb_cur[r0:r0 + bsz, 0:gl] = sr.astype(BF16)
                sb_cur[r0:r0 + bsz, gl:2 * gl] = si.astype(BF16)
            st_scr[0, :, sl] = sr
            st_scr[1, :, sl] = si
            yv = jnp.dot(sb_cur[...], wc_ref[0, gb], preferred_element_type=F32)
            y_ref[0, :, gb * cb:(gb + 1) * cb] = yv.astype(y_ref.dtype)

    @pl.when(d == 0)
    def _():
        sweep(False)

    @pl.when(d == 1)
    def _():
        sweep(True)

    @pl.when(k == n_chunks - 1)
    def _():
        sfin_ref[0] = st_scr[...]


def _s5_scan(xt, g, shift, scale, wb, wc, a, s0, *, bsz, cast=()):
    m, d = xt.shape
    seq = m // bsz
    steps = min(64, seq)
    rows = steps * bsz
    nc = seq // steps
    nstate = s0.shape[-1]
    n_gb = d // (S5_GB * S5_CH)
    gl = S5_GB * S5_STATE

    def chunk(dd, k):
        return k + dd * (nc - 1 - 2 * k)

    n_steps = 2 * nc
    cast_specs, cast_shapes = [], []
    for w in cast:
        slab = w.shape[0] // n_steps
        assert slab * n_steps == w.shape[0] and slab % 16 == 0
        cast_specs.append(pl.BlockSpec((slab, w.shape[1]), lambda dd, k: (dd * nc + k, 0)))
        cast_shapes.append(jax.ShapeDtypeStruct(w.shape, BF16))

    kern = functools.partial(_s5_kernel, n_chunks=nc, steps=steps, bsz=bsz, n_cast=len(cast))
    return pl.pallas_call(
        kern,
        grid=(2, nc),
        in_specs=[pl.BlockSpec((rows, d), lambda dd, k: (chunk(dd, k), 0)),
                  pl.BlockSpec((1, d), lambda dd, k: (0, 0)),
                  pl.BlockSpec((bsz, d), lambda dd, k: (0, 0)),
                  pl.BlockSpec((bsz, d), lambda dd, k: (0, 0)),
                  pl.BlockSpec((1, n_gb, S5_GB * S5_CH, 2 * gl), lambda dd, k: (dd, 0, 0, 0),
                               pipeline_mode=pl.Buffered(1)),
                  pl.BlockSpec((1, n_gb, 2 * gl, S5_GB * S5_CH), lambda dd, k: (dd, 0, 0, 0),
                               pipeline_mode=pl.Buffered(1)),
                  pl.BlockSpec((1, 2, nstate), lambda dd, k: (dd, 0, 0)),
                  pl.BlockSpec((1, 2, bsz, nstate), lambda dd, k: (dd, 0, 0, 0))] + cast_specs,
        out_specs=[pl.BlockSpec((1, rows, d), lambda dd, k: (dd, chunk(dd, k), 0)),
                   pl.BlockSpec((1, 2, bsz, nstate), lambda dd, k: (dd, 0, 0, 0))] + cast_specs,
        out_shape=[jax.ShapeDtypeStruct((2, m, d), BF16),
                   jax.ShapeDtypeStruct((2, 2, bsz, nstate), F32)] + cast_shapes,
        scratch_shapes=[pltpu.VMEM((2, bsz, nstate), F32),
                        pltpu.VMEM((rows, 2 * gl), F32), pltpu.VMEM((rows, 2 * gl), F32),
                        pltpu.VMEM((rows, 2 * gl), BF16), pltpu.VMEM((rows, 2 * gl), BF16)],
        compiler_params=_params("arbitrary", "arbitrary"),
        name="s5_scan",
    )(xt, g.reshape(1, d), shift, scale, wb, wc, a, s0, *cast)


def _gelu_tanh(v):
    return 0.5 * v * (1.0 + jnp.tanh(0.7978845608028654 * (v + 0.044715 * (v * v * v))))


def _s5_glu_kernel(x_ref, y_ref, g_ref, sh_ref, sc_ref, gate_ref, dsk_ref, w_ref, b_ref, o_ref, *, bsz):
    rows, dm = x_ref.shape
    x = x_ref[...]
    hn = _rms(x, g_ref[...]).reshape(rows // bsz, bsz, dm)
    h = (hn * (1.0 + sc_ref[...])[None] + sh_ref[...][None]).reshape(rows, dm)
    yv = dsk_ref[...] * h + y_ref[0].astype(F32) + y_ref[1].astype(F32)
    ge = _gelu_tanh(yv).astype(BF16)
    o = jnp.dot(ge, w_ref[...], preferred_element_type=F32) + b_ref[...]
    out = o[:, :dm] * jax.nn.sigmoid(o[:, dm:])
    res = x.reshape(rows // bsz, bsz, dm) + gate_ref[...][None] * out.reshape(rows // bsz, bsz, dm)
    o_ref[...] = res.reshape(rows, dm)


def _s5_glu(xt, y, g, shift, scale, gate, d_skip, w_glu, b_glu, *, bsz):
    m, d = xt.shape
    tm = min(512, m)
    kern = functools.partial(_s5_glu_kernel, bsz=bsz)
    full = lambda i: (0, 0)
    return pl.pallas_call(
        kern,
        grid=(m // tm,),
        in_specs=[pl.BlockSpec((tm, d), lambda i: (i, 0)),
                  pl.BlockSpec((2, tm, d), lambda i: (0, i, 0)),
                  pl.BlockSpec((1, d), full),
                  pl.BlockSpec((bsz, d), full),
                  pl.BlockSpec((bsz, d), full),
                  pl.BlockSpec((bsz, d), full),
                  pl.BlockSpec((1, d), full),
                  pl.BlockSpec((d, 2 * d), full),
                  pl.BlockSpec((1, 2 * d), full)],
        out_specs=pl.BlockSpec((tm, d), lambda i: (i, 0)),
        out_shape=jax.ShapeDtypeStruct((m, d), F32),
        compiler_params=_params("arbitrary"),
        name="s5_glu",
    )(xt, y, g.reshape(1, d), shift, scale, gate, d_skip.reshape(1, d), w_glu, b_glu.reshape(1, 2 * d))


MOE_TB = 512
MOE_TILE = 512
MOE_WIN = 256
MOE_PAD = 16
MOE_LANES = 128


def _moe_cap(n_tok, n_blocks):
    rows = n_tok + n_blocks * MOE_PAD + MOE_TILE
    return -(-rows // MOE_TILE) * MOE_TILE


def _moe_route_kernel(x_ref, g_ref, sh_ref, sc_ref, rwT_ref, rbT_ref, ut_ref,
                      sorted_ref, meta_ref, tab_ref,
                      buf_scr, zero_scr, run_smem, sems, *, bsz, n_exp, n_blocks, local_rows):
    b = pl.program_id(0)
    slot = lax.rem(b, 2)
    dm = x_ref.shape[-1]
    rows = x_ref.shape[0] * x_ref.shape[1] * x_ref.shape[2]

    @pl.when(b == 0)
    def _():
        for e in range(n_exp):
            run_smem[e] = 0
        zero_scr[...] = jnp.zeros_like(zero_scr)

    hn = _rms(x_ref[...].reshape(rows // bsz, bsz, dm), g_ref[...])
    h = (hn * (1.0 + sc_ref[...])[None] + sh_ref[...][None]).reshape(rows, dm)
    hb = h.astype(BF16)

    logits = _nt_dot(rwT_ref[...], h, precision=HIGHEST) + rbT_ref[...]
    sub = lax.broadcasted_iota(jnp.int32, logits.shape, 0)
    m1 = jnp.max(logits, axis=0, keepdims=True)
    i1 = jnp.min(jnp.where(logits == m1, sub, n_exp), axis=0, keepdims=True)
    rest = jnp.where(sub == i1, -jnp.inf, logits)
    m2 = jnp.max(rest, axis=0, keepdims=True)
    i2 = jnp.min(jnp.where(rest == m2, sub, n_exp), axis=0, keepdims=True)
    e2 = jnp.exp(m2 - m1)
    g1 = 1.0 / (1.0 + e2)
    g2 = e2 * g1
    sel1 = sub == i1
    sel2 = sub == i2
    oh = jnp.where(sel1, 1.0, 0.0) + jnp.where(sel2, 1.0, 0.0)
    cnt = jnp.sum(oh, axis=1, keepdims=True).astype(jnp.int32)
    cpad_v = ((cnt + (MOE_PAD - 1)) // MOE_PAD) * MOE_PAD
    rank = jnp.dot(oh.astype(BF16), ut_ref[...], preferred_element_type=F32)

    cpad, loc, run = [], [], []
    off = 0
    for e in range(n_exp):
        cpad.append(cpad_v[e, 0])
        loc.append(off)
        off = off + cpad[e]
        run.append(run_smem[e])
    sub1 = lax.broadcasted_iota(jnp.int32, (n_exp, 1), 0)
    loc_v = jnp.zeros((n_exp, 1), jnp.int32)
    run_v = jnp.zeros((n_exp, 1), jnp.int32)
    for e in range(n_exp):
        loc_v = jnp.where(sub1 == e, loc[e], loc_v)
        run_v = jnp.where(sub1 == e, run[e], run_v)
    loc_v = loc_v.astype(F32)
    run_v = run_v.astype(F32)

    def pick(sel, v):
        return jnp.sum(jnp.where(sel, v, 0.0), axis=0, keepdims=True)

    lp1 = pick(sel1, loc_v + rank).astype(jnp.int32)
    lp2 = pick(sel2, loc_v + rank).astype(jnp.int32)
    pos1 = pick(sel1, run_v + rank)
    pos2 = pick(sel2, run_v + rank)

    r = lax.broadcasted_iota(jnp.int32, (local_rows, rows), 0)
    onehot = jnp.where(r == lp1, 1.0, jnp.where(r == lp2, 1.0, 0.0)).astype(BF16)
    buf_scr[slot, 0:local_rows, :] = jnp.dot(onehot, hb, preferred_element_type=F32).astype(BF16)
    buf_scr[slot, local_rows:local_rows + MOE_TILE, :] = jnp.zeros((MOE_TILE, dm), BF16)

    rec = jnp.concatenate([i1.astype(F32), i2.astype(F32), pos1, pos2, g1, g2,
                           jnp.zeros((MOE_LANES - 6, rows), F32)], axis=0)
    meta_ref[...] = rec.T

    base = b * (2 * n_exp)
    for e in range(n_exp):
        tab_ref[base + e] = run[e]
        tab_ref[base + n_exp + e] = cpad[e]
        run_smem[e] = run[e] + cpad[e]

    def group_copy(e, slot_, loc_e, run_e):
        src = buf_scr.at[slot_, pl.ds(pl.multiple_of(loc_e, MOE_PAD), MOE_TILE)]
        dst = sorted_ref.at[e, pl.ds(pl.multiple_of(run_e, MOE_PAD), MOE_TILE)]
        return pltpu.make_async_copy(src, dst, sems.at[slot_, e])

    @pl.when(b > 0)
    def _():
        for e in range(n_exp):
            group_copy(e, 1 - slot, 0, 0).wait()

    for e in range(n_exp):
        group_copy(e, slot, loc[e], run[e]).start()

    @pl.when(b == n_blocks - 1)
    def _():
        fin = (n_blocks) * (2 * n_exp)
        for e in range(n_exp):
            group_copy(e, slot, 0, 0).wait()
        for e in range(n_exp):
            tot = run[e] + cpad[e]
            tab_ref[fin + e] = tot
            tab_ref[fin + n_exp + e] = 0
            dst = sorted_ref.at[e, pl.ds(pl.multiple_of(tot, MOE_PAD), MOE_TILE)]
            pltpu.make_async_copy(zero_scr, dst, sems.at[slot, e]).start()
        for e in range(n_exp):
            dst = sorted_ref.at[e, pl.ds(0, MOE_TILE)]
            pltpu.make_async_copy(zero_scr, dst, sems.at[slot, e]).wait()


def _moe_expert_kernel(te_ref, tr_ref, na_ref, xs_ref, wg_ref, wu_ref, wo_ref, y_ref, acc_scr, *, n_f):
    i = pl.program_id(0)
    f = pl.program_id(1)

    @pl.when(i < na_ref[0])
    def _():
        @pl.when(f == 0)
        def _():
            acc_scr[...] = jnp.zeros_like(acc_scr)

        xs = xs_ref[0]
        gv = jnp.dot(xs, wg_ref[0], preferred_element_type=F32)
        uv = jnp.dot(xs, wu_ref[0], preferred_element_type=F32)
        a = (_silu(gv) * uv).astype(BF16)
        acc_scr[...] += jnp.dot(a, wo_ref[0], preferred_element_type=F32)

        @pl.when(f == n_f - 1)
        def _():
            y_ref[0] = acc_scr[...].astype(y_ref.dtype)


def _moe_combine_kernel(tab_ref, end_ref, x_ref, meta_ref, gate_ref, gf_ref, ys_ref, o_ref,
                        win_scr, acc_scr, tr_scr, sems, *, bsz, n_exp):
    b = pl.program_id(0)
    slot = lax.rem(b, 2)
    n_cols, n_rows, _, dm = x_ref.shape
    rows = n_cols * n_rows * bsz
    n_win = MOE_TILE // MOE_WIN

    def window(blk, slot_, e, c):
        base = blk * (2 * n_exp)
        run_e = tab_ref[base + e]
        live = tab_ref[base + n_exp + e] > c * MOE_WIN
        ws = jnp.minimum(run_e + c * MOE_WIN, end_ref[e] - MOE_WIN)
        k = e * n_win + c
        cp = pltpu.make_async_copy(ys_ref.at[e, pl.ds(pl.multiple_of(ws, MOE_PAD), MOE_WIN)],
                                   win_scr.at[slot_, k], sems.at[slot_, k])
        return live, ws, k, cp

    def fetch(blk, slot_):
        for e in range(n_exp):
            for c in range(n_win):
                live, _, _, cp = window(blk, slot_, e, c)

                @pl.when(live)
                def _():
                    cp.start()

    @pl.when(b == 0)
    def _():
        fetch(0, 0)

    fetch(b + 1, 1 - slot)

    meta = meta_ref[...]
    e1, e2 = meta[:, 0:1], meta[:, 1:2]
    pos1, pos2 = meta[:, 2:3].astype(jnp.int32), meta[:, 3:4].astype(jnp.int32)
    g1, g2 = meta[:, 4:5], meta[:, 5:6]
    acc_scr[...] = jnp.zeros_like(acc_scr)
    lane = lax.broadcasted_iota(jnp.int32, (rows, MOE_WIN), 1)

    for e in range(n_exp):
        for c in range(n_win):
            live, ws, k, cp = window(b, slot, e, c)

            @pl.when(live)
            def _():
                cp.wait()
                rel = jnp.where(e1 == e, pos1, pos2) - ws
                hit = jnp.where(e1 == e, 1.0, jnp.where(e2 == e, 1.0, 0.0))
                pe = jnp.where(rel == lane, hit, 0.0).astype(BF16)
                ge = jnp.where(e1 == e, g1, jnp.where(e2 == e, g2, 0.0))
                acc_scr[...] += ge * jnp.dot(pe, win_scr[slot, k], preferred_element_type=F32)

    upd = gate_ref[...][None] * acc_scr[...].reshape(rows // bsz, bsz, dm)
    xo = x_ref[...].reshape(rows, dm) + upd.reshape(rows, dm)
    _fill_lane_tiles(_rms(xo, gf_ref[...]), tr_scr, n_rows * bsz)
    for rl in range(n_rows):
        _strided_rows_out(
            tr_scr, lambda bb, lt: o_ref.at[bb, rl, :, lt * LANES:(lt + 1) * LANES],
            n_groups=bsz, group_rows=n_cols, stride=n_rows * bsz + TR_SKEW, base=rl * bsz)


def _moe(xt, g, shift, scale, gate, router_w, router_b, w_in, w_out, g_final, *, bsz, grid_rows):
    m, d = xt.shape
    n_exp, f, _ = w_out.shape
    nb = m // MOE_TB
    cap = _moe_cap(m, nb)
    local_rows = -(-(TOP_K * MOE_TB + n_exp * (MOE_PAD - 1)) // 128) * 128
    full1 = lambda i: (0, 0)

    blk_cols = 8
    blk_rows = MOE_TB // (blk_cols * bsz)
    assert blk_rows * blk_cols * bsz == MOE_TB and grid_rows % blk_rows == 0 and GRID_W % blk_cols == 0
    nrg = grid_rows // blk_rows
    x4 = xt.reshape(GRID_W, grid_rows, bsz, d)
    x_spec1 = pl.BlockSpec((blk_cols, blk_rows, bsz, d), lambda i: (i // nrg, i % nrg, 0, 0))

    ut = (jnp.arange(MOE_TB)[:, None] < jnp.arange(MOE_TB)[None, :]).astype(BF16)
    route = functools.partial(_moe_route_kernel, bsz=bsz, n_exp=n_exp, n_blocks=nb, local_rows=local_rows)
    sorted_h, meta, tab = pl.pallas_call(
        route,
        grid=(nb,),
        in_specs=[x_spec1,
                  pl.BlockSpec((1, d), full1),
                  pl.BlockSpec((bsz, d), full1),
                  pl.BlockSpec((bsz, d), full1),
                  pl.BlockSpec((n_exp, d), full1),
                  pl.BlockSpec((n_exp, 1), full1),
                  pl.BlockSpec((MOE_TB, MOE_TB), full1)],
        out_specs=[pl.BlockSpec(memory_space=pl.ANY),
                   pl.BlockSpec((MOE_TB, MOE_LANES), lambda i: (i, 0)),
                   pl.BlockSpec(memory_space=pltpu.SMEM)],
        out_shape=[jax.ShapeDtypeStruct((n_exp, cap, d), BF16),
                   jax.ShapeDtypeStruct((m, MOE_LANES), F32),
                   jax.ShapeDtypeStruct(((nb + 1) * 2 * n_exp,), jnp.int32)],
        scratch_shapes=[pltpu.VMEM((2, local_rows + MOE_TILE, d), BF16),
                        pltpu.VMEM((MOE_TILE, d), BF16),
                        pltpu.SMEM((n_exp,), jnp.int32),
                        pltpu.SemaphoreType.DMA((2, n_exp))],
        compiler_params=_params("arbitrary"),
        name="moe_route",
    )(x4, g.reshape(1, d), shift, scale, router_w.T, router_b.reshape(n_exp, 1), ut)

    tot = tab[nb * 2 * n_exp: nb * 2 * n_exp + n_exp]
    ntile = jnp.maximum((tot + MOE_TILE - 1) // MOE_TILE, 1)
    cum = jnp.cumsum(ntile)
    n_active = cum[-1:]
    max_tiles = (TOP_K * m + nb * n_exp * (MOE_PAD - 1)) // MOE_TILE + n_exp + 1
    idc = jnp.minimum(jnp.arange(max_tiles, dtype=jnp.int32), n_active - 1)
    tile_e = jnp.sum(idc[:, None] >= cum[None, :], axis=1).astype(jnp.int32)
    tile_r = (idc - (cum - ntile)[tile_e]).astype(jnp.int32)
    end = (ntile * MOE_TILE).astype(jnp.int32)

    tf = f // 2 if (f // 2) % 128 == 0 else f
    n_f = f // tf

    def f_idx(i, j, na):
        return jnp.where(i < na[0], j, n_f - 1)

    ys = pl.pallas_call(
        functools.partial(_moe_expert_kernel, n_f=n_f),
        grid_spec=pltpu.PrefetchScalarGridSpec(
            num_scalar_prefetch=3,
            grid=(max_tiles, n_f),
            in_specs=[pl.BlockSpec((1, MOE_TILE, d), lambda i, j, te, tr, na: (te[i], tr[i], 0)),
                      pl.BlockSpec((1, d, tf), lambda i, j, te, tr, na: (te[i], 0, f_idx(i, j, na))),
                      pl.BlockSpec((1, d, tf), lambda i, j, te, tr, na: (te[i], 0, n_f + f_idx(i, j, na))),
                      pl.BlockSpec((1, tf, d), lambda i, j, te, tr, na: (te[i], f_idx(i, j, na), 0))],
            out_specs=pl.BlockSpec((1, MOE_TILE, d), lambda i, j, te, tr, na: (te[i], tr[i], 0)),
            scratch_shapes=[pltpu.VMEM((MOE_TILE, d), F32)]),
        out_shape=jax.ShapeDtypeStruct((n_exp, cap, d), BF16),
        compiler_params=_params("arbitrary", "arbitrary"),
        name="moe_experts",
    )(tile_e, tile_r, n_active.astype(jnp.int32), sorted_h, w_in, w_in, w_out)

    n_win = MOE_TILE // MOE_WIN
    out = pl.pallas_call(
        functools.partial(_moe_combine_kernel, bsz=bsz, n_exp=n_exp),
        grid_spec=pltpu.PrefetchScalarGridSpec(
            num_scalar_prefetch=2,
            grid=(nb,),
            in_specs=[pl.BlockSpec((blk_cols, blk_rows, bsz, d), lambda i, tb, en: (i // nrg, i % nrg, 0, 0)),
                      pl.BlockSpec((MOE_TB, MOE_LANES), lambda i, tb, en: (i, 0)),
                      pl.BlockSpec((bsz, d), lambda i, tb, en: (0, 0)),
                      pl.BlockSpec((1, d), lambda i, tb, en: (0, 0)),
                      pl.BlockSpec(memory_space=pl.ANY)],
            out_specs=pl.BlockSpec((bsz, blk_rows, blk_cols, d), lambda i, tb, en: (0, i % nrg, i // nrg, 0)),
            scratch_shapes=[pltpu.VMEM((2, n_exp * n_win, MOE_WIN, d), BF16),
                            pltpu.VMEM((MOE_TB, d), F32),
                            pltpu.VMEM((d // LANES, _tr_rows(blk_cols, blk_rows * bsz), LANES), F32),
                            pltpu.SemaphoreType.DMA((2, n_exp * n_win))]),
        out_shape=jax.ShapeDtypeStruct((bsz, grid_rows, GRID_W, d), F32),
        compiler_params=_params("arbitrary"),
        name="moe_combine",
    )(tab, end, x4, meta, gate, g_final.reshape(1, d), ys)
    return out.reshape(bsz, grid_rows * GRID_W, d)


def _blockdiag(w, n_gb):
    two, g, r, c = w.shape
    eye = jnp.eye(S5_GB, dtype=w.dtype)
    out = jnp.einsum("dbgrc,gh->dbgrhc", w.reshape(two, n_gb, S5_GB, r, c), eye)
    return out.reshape(two, n_gb, S5_GB * r, S5_GB * c)


def kernel(x, c, ctx, c_ctx, ada_w, ada_b, norm_mix, norm_ffn, ssd_w_in, ssd_conv_w, ssd_conv_b, ssd_dt_bias, ssd_a_log, ssd_d, ssd_norm, ssd_w_out, s5_lam_re, s5_lam_im, s5_log_step, s5_b_re, s5_b_im, s5_c_re, s5_c_im, s5_d, s5_w_glu, s5_b_glu, ffn_w_in, ffn_w_out, moe_router_w, moe_router_b, moe_w_in, moe_w_out, norm_final):
    bsz, seq, d = x.shape
    ctx_len = ctx.shape[1]
    depth = ada_w.shape[0]
    assert depth == 2, "one SSD layer followed by one S5 layer"
    rows = seq // GRID_W

    pad_rows = (-(bsz + 1)) % 8
    cc = jnp.concatenate([c, c_ctx[None], jnp.zeros((pad_rows, d), F32)], axis=0)
    mods = _ada(cc, ada_w, ada_b)

    def mod(i, n):
        sl = slice(n * d, (n + 1) * d)
        return mods[i, :bsz, sl], mods[i, bsz:bsz + 1, sl]

    heads = ssd_d.shape[1]
    d_inner = heads * SSD_HEADDIM
    conv_dim = ssd_conv_w.shape[2]
    n_main = d_inner + conv_dim
    w_in0 = ssd_w_in[0]
    w_main = w_in0[:, :n_main].astype(BF16)
    w_dt = w_in0[:, n_main:]
    conv_w_full = jnp.concatenate([jnp.zeros((SSD_CONV, d_inner), F32), 0.5 * ssd_conv_w[0]], axis=1)
    conv_b_full = jnp.concatenate([jnp.zeros((d_inner,), F32), 0.5 * ssd_conv_b[0]])
    d_exp = jnp.repeat(ssd_d[0], SSD_HEADDIM)
    w_out0 = ssd_w_out[0].astype(BF16)
    gw = (heads // SSD_GROUPS) * SSD_HEADDIM

    (sh_x, sh_c), (sc_x, sc_c), (gt_x, gt_c) = mod(0, 0), mod(0, 1), mod(0, 2)
    r3 = lambda a: a[:, None, :]
    dt_bias = ssd_dt_bias[0].reshape(-1)
    a_log = ssd_a_log[0].reshape(-1)

    def ssd_mixer(tok3, shift, scale, gate, h0):
        b_, t_, _ = tok3.shape
        zx, dtc, dtcT = _ssd_inproj(tok3, norm_mix[0], r3(shift), r3(scale), w_main, conv_w_full,
                                    conv_b_full, w_dt, dt_bias, a_log, d_inner=d_inner, heads=heads)
        y, hfin = _ssd_scan(zx, dtc, dtcT, h0, d_inner=d_inner, heads=heads)
        out = _ssd_out(y.reshape(2, b_ * t_, d_inner), zx.reshape(b_ * t_, n_main), d_exp, ssd_norm[0],
                       w_out0, tok3.reshape(b_ * t_, d), r3(gate), seq=t_, d_inner=d_inner)
        return out, hfin

    h_zero = jnp.zeros((bsz, 2, SSD_GROUPS, SSD_STATE, gw), F32)
    ctx2, h_ctx = ssd_mixer(ctx, sh_c, sc_c, gt_c, h_zero)
    x2, _ = ssd_mixer(x, sh_x, sc_x, gt_x, h_ctx)

    ffn_in = ffn_w_in[0].astype(BF16)
    ffn_out = ffn_w_out[0].astype(BF16)
    (sh_x, sh_c), (sc_x, sc_c), (gt_x, gt_c) = mod(0, 3), mod(0, 4), mod(0, 5)
    xt = _ffn(x2.reshape(bsz, seq, d), norm_ffn[0], r3(sh_x), r3(sc_x), r3(gt_x), ffn_in, ffn_out,
              time_major=True).reshape(seq * bsz, d)
    ctx2 = _ffn(ctx2.reshape(bsz, ctx_len, d), norm_ffn[0], r3(sh_c), r3(sc_c), r3(gt_c), ffn_in, ffn_out)

    ct = ctx2.reshape(bsz, ctx_len, d).transpose(1, 0, 2).reshape(ctx_len * bsz, d)

    ar, ai, bbr, bbi = _s5_discretize(s5_lam_re[0], s5_lam_im[0], s5_log_step[0], s5_b_re[0], s5_b_im[0])
    n_groups = ar.shape[1]
    n_gb = n_groups // S5_GB
    wb = jnp.concatenate([_blockdiag(bbr.transpose(0, 1, 3, 2), n_gb),
                          _blockdiag(bbi.transpose(0, 1, 3, 2), n_gb)], axis=-1).astype(BF16)
    wc = jnp.concatenate([_blockdiag(s5_c_re[0].transpose(0, 1, 3, 2), n_gb),
                          _blockdiag(-s5_c_im[0].transpose(0, 1, 3, 2), n_gb)], axis=2).astype(BF16)
    a_coef = jnp.stack([ar.reshape(2, -1), ai.reshape(2, -1)], axis=1)
    nstate = a_coef.shape[-1]

    (sh_x, sh_c), (sc_x, sc_c), (gt_x, _) = mod(1, 0), mod(1, 1), mod(1, 2)
    bc = lambda a: jnp.broadcast_to(a, (bsz, d))
    s_zero = jnp.zeros((2, 2, bsz, nstate), F32)
    _, s_ctx = _s5_scan(ct, norm_mix[1], bc(sh_c), bc(sc_c), wb, wc, a_coef, s_zero, bsz=bsz)
    n_exp, _, two_f = moe_w_in.shape[1:]
    y, _, w_in_b, w_out_b = _s5_scan(xt, norm_mix[1], sh_x, sc_x, wb, wc, a_coef, s_ctx, bsz=bsz,
                                     cast=(moe_w_in[0].reshape(n_exp * d, two_f),
                                           moe_w_out[0].reshape(n_exp * (two_f // 2), d)))
    w_in_b = w_in_b.reshape(n_exp, d, two_f)
    w_out_b = w_out_b.reshape(n_exp, two_f // 2, d)
    xt = _s5_glu(xt, y, norm_mix[1], sh_x, sc_x, gt_x, s5_d[0], s5_w_glu[0].astype(BF16), s5_b_glu[0], bsz=bsz)

    (sh_x, _), (sc_x, _), (gt_x, _) = mod(1, 3), mod(1, 4), mod(1, 5)
    return _moe(xt, norm_ffn[1], sh_x, sc_x, gt_x, moe_router_w[0], moe_router_b[0],
                w_in_b, w_out_b, norm_final, bsz=bsz, grid_rows=rows)
```

```python
import functools

import jax
import jax.numpy as jnp
from jax import lax
from jax.experimental import pallas as pl
from jax.experimental.pallas import tpu as pltpu

F32 = jnp.float32
BF16 = jnp.bfloat16
HIGHEST = lax.Precision.HIGHEST

EPS = 1e-6
GRID_W = 64
N_MOD = 6

SSD_HEADDIM = 64
SSD_GROUPS = 4
SSD_STATE = 128
SSD_CONV = 5
SSD_CHUNK = 128

S5_CH = 16
S5_STATE = 64
S5_GB = 8

TOP_K = 2

V7X_VMEM_BYTES = 64 * 1024 * 1024
VMEM_LIMIT = V7X_VMEM_BYTES - 4 * 1024 * 1024
NEG_BIG = -1e30
LANES = 128
LOG2E = 1.4426950408889634


TR_SKEW = 8


def _tr_rows(n_groups, group):
    return n_groups * (group + TR_SKEW)


def _fill_lane_tiles(res, tr_scr, group):
    pitch = group + TR_SKEW
    for lt in range(res.shape[1] // LANES):
        for g in range(res.shape[0] // group):
            tr_scr[lt, g * pitch:g * pitch + group, :] = res[g * group:(g + 1) * group, lt * LANES:(lt + 1) * LANES]


def _strided_rows_out(tr_scr, dst, *, n_groups, group_rows, stride, base=0):
    for g in range(n_groups):
        for lt in range(tr_scr.shape[0]):
            dst(g, lt)[...] = tr_scr[lt, pl.ds(base + g, group_rows, stride=stride), :]


def _params(*sem):
    return pltpu.CompilerParams(dimension_semantics=sem, vmem_limit_bytes=VMEM_LIMIT)


def _silu(v):
    return v * jax.nn.sigmoid(v)


def _softplus(v):
    return jnp.maximum(v, 0.0) + jnp.log1p(jnp.exp(-jnp.abs(v)))


def _rms(x, g):
    inv = lax.rsqrt(jnp.mean(x * x, axis=-1, keepdims=True) + EPS)
    return (x * inv) * g


def _nt_dot(a, b, precision=None):
    return lax.dot_general(a, b, (((1,), (1,)), ((), ())), precision=precision,
                           preferred_element_type=F32)


def _ada_kernel(c_ref, w_ref, b_ref, o_ref):
    s = _silu(c_ref[...])
    o_ref[0] = jnp.dot(s, w_ref[0], precision=HIGHEST, preferred_element_type=F32) + b_ref[0]


def _ada(cc, ada_w, ada_b):
    depth, d, n = ada_w.shape
    r = cc.shape[0]
    tn = 1536
    return pl.pallas_call(
        _ada_kernel,
        grid=(depth, n // tn),
        in_specs=[pl.BlockSpec((r, d), lambda i, j: (0, 0)),
                  pl.BlockSpec((1, d, tn), lambda i, j: (i, 0, j)),
                  pl.BlockSpec((1, 1, tn), lambda i, j: (i, 0, j))],
        out_specs=pl.BlockSpec((1, r, tn), lambda i, j: (i, 0, j)),
        out_shape=jax.ShapeDtypeStruct((depth, r, n), F32),
        compiler_params=_params("arbitrary", "arbitrary"),
        name="ada_mod",
    )(cc, ada_w, ada_b.reshape(depth, 1, n))


def _ssd_inproj_kernel(x_ref, g_ref, sh_ref, sc_ref, w_ref, cw_ref, cb_ref, wdt_ref, wdtT_ref,
                       bdt_ref, bdtT_ref, alog_ref, alogT_ref,
                       o_ref, dt_ref, dtT_ref, h_scr, pad_scr, *, n_plain, seq, heads, row_chunk, dt_chunk):
    j = pl.program_id(1)
    pad = 8

    @pl.when(j == 0)
    def _():
        for r in range(0, seq, dt_chunk):
            rs = slice(r, r + dt_chunk)
            h = _rms(x_ref[0, rs, :], g_ref[...]) * (1.0 + sc_ref[0]) + sh_ref[0]
            hb = h.astype(BF16)
            h_scr[rs, :] = hb
            hl = (h - hb.astype(F32)).astype(BF16)
            nh2 = 2 * heads
            p = jnp.dot(hb, wdt_ref[...], preferred_element_type=F32)
            q = jnp.dot(hl, wdt_ref[...], preferred_element_type=F32)
            dt = _softplus(p[:, :nh2] + (p[:, nh2:] + q[:, :nh2]) + bdt_ref[...])
            da = dt * (-jnp.exp(alog_ref[...]))
            pT = _nt_dot(wdtT_ref[...], hb)
            qT = _nt_dot(wdtT_ref[...], hl)
            dtT = _softplus(pT[:nh2, :] + (pT[nh2:, :] + qT[:nh2, :]) + bdtT_ref[...])
            daT = dtT * (-jnp.exp(alogT_ref[...]))
            for d in range(2):
                sl = slice(d * heads, (d + 1) * heads)
                dt_ref[d, 0, rs, :] = jnp.concatenate([dt[:, sl], da[:, sl]], axis=-1)
                dtT_ref[d, 0, :, rs] = jnp.concatenate([dtT[sl, :], daT[sl, :]], axis=0)
        zeros = jnp.zeros((pad, pad_scr.shape[1]), F32)
        pad_scr[0:pad, :] = zeros
        pad_scr[pad + seq:2 * pad + seq, :] = zeros

    acc = jnp.dot(h_scr[...], w_ref[...], preferred_element_type=F32)

    @pl.when(j < n_plain)
    def _():
        o_ref[0] = acc.astype(o_ref.dtype)

    @pl.when(j >= n_plain)
    def _():
        pad_scr[pad:pad + seq, :] = acc
        half = (SSD_CONV - 1) // 2
        n_all = row_chunk + 2 * pad
        for r in range(0, seq, row_chunk):
            a = pad_scr[r:r + n_all, :]
            s = cb_ref[...] + cw_ref[half:half + 1, :] * a[pad:pad + row_chunk, :]
            for k in range(SSD_CONV):
                if k != half:
                    shifted = pltpu.roll(a, (half - k) % n_all, axis=0)
                    s = s + cw_ref[k:k + 1, :] * shifted[pad:pad + row_chunk, :]
            o_ref[0, r:r + row_chunk, :] = (s * (1.0 + jnp.tanh(s))).astype(o_ref.dtype)


def _ssd_inproj(x3, g, shift, scale, w_main, conv_w_full, conv_b_full, w_dt, dt_bias, a_log,
                *, d_inner, heads):
    bsz, seq, d = x3.shape
    n = w_main.shape[1]
    tn = 512
    nh2 = 2 * heads
    per_batch = shift.shape[0] > 1
    mod_map = (lambda b, j: (b, 0, 0)) if per_batch else (lambda b, j: (0, 0, 0))
    kern = functools.partial(_ssd_inproj_kernel, n_plain=d_inner // tn, seq=seq, heads=heads,
                             row_chunk=min(64, seq), dt_chunk=min(256, seq))
    w_hi = w_dt.astype(BF16)
    w_dt2 = jnp.concatenate([w_hi, (w_dt - w_hi.astype(F32)).astype(BF16)], axis=1)
    const2 = lambda b, j: (0, 0)
    return pl.pallas_call(
        kern,
        grid=(bsz, n // tn),
        in_specs=[pl.BlockSpec((1, seq, d), lambda b, j: (b, 0, 0), pipeline_mode=pl.Buffered(1)),
                  pl.BlockSpec((1, d), const2),
                  pl.BlockSpec((1, 1, d), mod_map),
                  pl.BlockSpec((1, 1, d), mod_map),
                  pl.BlockSpec((d, tn), lambda b, j: (0, j)),
                  pl.BlockSpec((SSD_CONV, tn), lambda b, j: (0, j)),
                  pl.BlockSpec((1, tn), lambda b, j: (0, j)),
                  pl.BlockSpec((d, 2 * nh2), const2),
                  pl.BlockSpec((2 * nh2, d), const2),
                  pl.BlockSpec((1, nh2), const2),
                  pl.BlockSpec((nh2, 1), const2),
                  pl.BlockSpec((1, nh2), const2),
                  pl.BlockSpec((nh2, 1), const2)],
        out_specs=[pl.BlockSpec((1, seq, tn), lambda b, j: (b, 0, j)),
                   pl.BlockSpec((2, 1, seq, nh2), lambda b, j: (0, b, 0, 0)),
                   pl.BlockSpec((2, 1, nh2, seq), lambda b, j: (0, b, 0, 0))],
        out_shape=[jax.ShapeDtypeStruct((bsz, seq, n), BF16),
                   jax.ShapeDtypeStruct((2, bsz, seq, nh2), F32),
                   jax.ShapeDtypeStruct((2, bsz, nh2, seq), F32)],
        scratch_shapes=[pltpu.VMEM((seq, d), BF16), pltpu.VMEM((seq + 16, tn), F32)],
        compiler_params=_params("arbitrary", "arbitrary"),
        name="ssd_inproj",
    )(x3, g.reshape(1, d), shift, scale, w_main, conv_w_full, conv_b_full.reshape(1, n),
      w_dt2, w_dt2.T, dt_bias.reshape(1, nh2), dt_bias.reshape(nh2, 1),
      a_log.reshape(1, nh2), a_log.reshape(nh2, 1))


def _ssd_scan_kernel(xs_ref, b_ref, c_ref, dtc_ref, dtcT_ref, h0_ref, y_ref, hfin_ref,
                     st_scr, *, heads, n_chunks):
    d = pl.program_id(1)
    k = pl.program_id(2)
    L = xs_ref.shape[1]
    hpg = heads // SSD_GROUPS
    pairs_per_group = hpg // 2
    gw = hpg * SSD_HEADDIM

    @pl.when(k == 0)
    def _():
        st_scr[...] = h0_ref[0, 0]

    row = lax.broadcasted_iota(jnp.int32, (L, L), 0)
    col = lax.broadcasted_iota(jnp.int32, (L, L), 1)
    mask = jnp.where(d == 0, row - col, col - row) >= 0
    mask_f = mask.astype(F32)

    dtc = dtc_ref[0, 0]
    dtcT = dtcT_ref[0, 0]
    assert L == 2 * SSD_HEADDIM, "the head-pair tiles assume chunk length == 2 * head dim"
    da = dtc[:, heads:]
    dtT, daT = dtcT[:heads, :], dtcT[heads:, :]
    cum = jnp.dot(mask_f, da, precision=HIGHEST, preferred_element_type=F32)
    cumT = _nt_dot(daT, mask_f, precision=HIGHEST)
    tot = jnp.sum(da, axis=0, keepdims=True)
    totT = jnp.sum(daT, axis=1, keepdims=True)
    etot = jnp.exp(tot)
    c2 = cum * LOG2E
    c2T = cumT * LOG2E
    wrow = jnp.exp(totT - cumT) * dtT

    lane = lax.broadcasted_iota(jnp.int32, (L, 2 * SSD_HEADDIM), 1)
    lo = lane < SSD_HEADDIM
    lo_row = lax.broadcasted_iota(jnp.int32, (1, 2 * SSD_HEADDIM), 1) < SSD_HEADDIM

    for g in range(SSD_GROUPS):
        bg = b_ref[0, :, g * SSD_STATE:(g + 1) * SSD_STATE]
        cg = c_ref[0, :, g * SSD_STATE:(g + 1) * SSD_STATE]
        scores = _nt_dot(cg, bg) * mask_f
        bgT = bg.astype(F32).T
        yoff = jnp.dot(cg, st_scr[g].astype(BF16), preferred_element_type=F32)
        for q in range(pairs_per_group):
            p = g * pairs_per_group + q
            h0, h1 = 2 * p, 2 * p + 1
            xs2 = xs_ref[0, :, p * 128:(p + 1) * 128]
            zero = jnp.zeros_like(xs2)
            rhs = jnp.concatenate([jnp.where(lo, xs2, zero), jnp.where(lo, zero, xs2)], axis=0)
            ms, bs, es = [], [], []
            for h in (h0, h1):
                col = jnp.broadcast_to(c2[:, h:h + 1], (L, L))
                dec = jnp.exp2(jnp.minimum(col - c2T[h:h + 1, :], 0.0))
                ms.append((scores * dec * dtT[h:h + 1, :]).astype(BF16))
                bs.append((bgT * wrow[h:h + 1, :]).astype(BF16))
                es.append(jnp.exp2(col))
            ydiag = jnp.dot(jnp.concatenate(ms, axis=1), rhs, preferred_element_type=F32)
            new = jnp.dot(jnp.concatenate(bs, axis=1), rhs, preferred_element_type=F32)
            sl = slice(q * 128, (q + 1) * 128)
            y2 = ydiag + yoff[:, sl] * jnp.where(lo, es[0], es[1])
            y_ref[0, 0, :, p * 128:(p + 1) * 128] = y2.astype(y_ref.dtype)
            et2 = jnp.where(lo_row, etot[:, h0:h0 + 1], etot[:, h1:h1 + 1])
            st_scr[g, :, sl] = st_scr[g, :, sl] * et2 + new

    @pl.when(k == n_chunks - 1)
    def _():
        hfin_ref[0, 0] = st_scr[...]


def _ssd_scan(zx, dtc, dtcT, h0, *, d_inner, heads):
    bsz, seq, _ = zx.shape
    L = min(SSD_CHUNK, seq)
    nc = seq // L
    bc_w = SSD_GROUPS * SSD_STATE
    gw = (heads // SSD_GROUPS) * SSD_HEADDIM
    nh2 = 2 * heads
    xs_blk = d_inner // d_inner
    b_blk = (2 * d_inner) // bc_w
    c_blk = b_blk + 1

    def chunk(d, k):
        return k + d * (nc - 1 - 2 * k)

    kern = functools.partial(_ssd_scan_kernel, heads=heads, n_chunks=nc)
    st_shape = (SSD_GROUPS, SSD_STATE, gw)
    return pl.pallas_call(
        kern,
        grid=(bsz, 2, nc),
        in_specs=[pl.BlockSpec((1, L, d_inner), lambda b, d, k: (b, chunk(d, k), xs_blk)),
                  pl.BlockSpec((1, L, bc_w), lambda b, d, k: (b, chunk(d, k), b_blk)),
                  pl.BlockSpec((1, L, bc_w), lambda b, d, k: (b, chunk(d, k), c_blk)),
                  pl.BlockSpec((1, 1, L, nh2), lambda b, d, k: (d, b, chunk(d, k), 0)),
                  pl.BlockSpec((1, 1, nh2, L), lambda b, d, k: (d, b, 0, chunk(d, k))),
                  pl.BlockSpec((1, 1) + st_shape, lambda b, d, k: (b, d, 0, 0, 0))],
        out_specs=[pl.BlockSpec((1, 1, L, d_inner), lambda b, d, k: (d, b, chunk(d, k), 0)),
                   pl.BlockSpec((1, 1) + st_shape, lambda b, d, k: (b, d, 0, 0, 0))],
        out_shape=[jax.ShapeDtypeStruct((2, bsz, seq, d_inner), BF16),
                   jax.ShapeDtypeStruct((bsz, 2) + st_shape, F32)],
        scratch_shapes=[pltpu.VMEM(st_shape, F32)],
        compiler_params=_params("arbitrary", "arbitrary", "arbitrary"),
        name="ssd_scan",
    )(zx, zx, zx, dtc, dtcT, h0)


def _ssd_out_kernel(y_ref, z_ref, xs_ref, dsk_ref, nw_ref, w_ref, x_ref, gate_ref, o_ref):
    y = y_ref[0].astype(F32) + y_ref[1].astype(F32) + dsk_ref[...] * xs_ref[...].astype(F32)
    hz = 0.5 * z_ref[...].astype(F32)
    yg = y * (hz * (1.0 + jnp.tanh(hz)))
    yn = _rms(yg, nw_ref[...]).astype(BF16)
    out = jnp.dot(yn, w_ref[...], preferred_element_type=F32)
    o_ref[...] = x_ref[...] + gate_ref[0] * out


def _ssd_out(y, zx, d_exp, norm_w, w_out, x2, gate, *, seq, d_inner):
    m, d = x2.shape
    tm = min(512, seq)
    bpt = seq // tm
    per_batch = gate.shape[0] > 1
    gate_map = (lambda i: (i // bpt, 0, 0)) if per_batch else (lambda i: (0, 0, 0))
    return pl.pallas_call(
        _ssd_out_kernel,
        grid=(m // tm,),
        in_specs=[pl.BlockSpec((2, tm, d_inner), lambda i: (0, i, 0)),
                  pl.BlockSpec((tm, d_inner), lambda i: (i, 0)),
                  pl.BlockSpec((tm, d_inner), lambda i: (i, 1)),
                  pl.BlockSpec((1, d_inner), lambda i: (0, 0)),
                  pl.BlockSpec((1, d_inner), lambda i: (0, 0)),
                  pl.BlockSpec((d_inner, d), lambda i: (0, 0)),
                  pl.BlockSpec((tm, d), lambda i: (i, 0)),
                  pl.BlockSpec((1, 1, d), gate_map)],
        out_specs=pl.BlockSpec((tm, d), lambda i: (i, 0)),
        out_shape=jax.ShapeDtypeStruct((m, d), F32),
        compiler_params=_params("arbitrary"),
        name="ssd_out",
    )(y, zx, zx, d_exp.reshape(1, d_inner), norm_w.reshape(1, d_inner), w_out, x2, gate)


def _ffn_kernel(x_ref, g_ref, sh_ref, sc_ref, gate_ref, wg_ref, wu_ref, wo_ref, o_ref,
                h_scr, acc_scr, tr_scr, *, n_f, time_major):
    j = pl.program_id(1)
    bb, tt, dm = x_ref.shape

    @pl.when(j == 0)
    def _():
        h = _rms(x_ref[...], g_ref[...]) * (1.0 + sc_ref[...]) + sh_ref[...]
        h_scr[...] = h.reshape(bb * tt, dm).astype(BF16)
        acc_scr[...] = jnp.zeros_like(acc_scr)

    h = h_scr[...]
    gv = jnp.dot(h, wg_ref[...], preferred_element_type=F32)
    uv = jnp.dot(h, wu_ref[...], preferred_element_type=F32)
    a = (_silu(gv) * uv).astype(BF16)
    acc_scr[...] += jnp.dot(a, wo_ref[...], preferred_element_type=F32)

    @pl.when(j == n_f - 1)
    def _():
        res = x_ref[...] + gate_ref[...] * acc_scr[...].reshape(bb, tt, dm)
        if time_major:
            _fill_lane_tiles(res.reshape(bb * tt, dm), tr_scr, tt)
            _strided_rows_out(tr_scr, lambda c, lt: o_ref.at[c, 0, :, lt * LANES:(lt + 1) * LANES],
                              n_groups=tt, group_rows=bb, stride=tt + TR_SKEW)
        else:
            o_ref[...] = res


def _ffn(x3, g, shift, scale, gate, w_in, w_out, *, time_major=False):
    bsz, seq, d = x3.shape
    f = w_out.shape[0]
    tf = f // 2 if (f // 2) % 128 == 0 else f
    n_f = f // tf
    per_batch = shift.shape[0] > 1
    if time_major:
        bb, tt = 8, GRID_W
        assert bsz % bb == 0 and seq % tt == 0 and per_batch
        rows = seq // tt
        x_map = lambda i, j: (i // rows, i % rows, 0)
        mod_map = lambda i, j: (i // rows, 0, 0)
        out_spec = pl.BlockSpec((tt, 1, bb, d), lambda i, j: (0, i % rows, i // rows, 0))
        out_shape = jax.ShapeDtypeStruct((tt, rows, bsz, d), F32)
        n_blocks = (bsz // bb) * rows
    else:
        tt = min(512, seq)
        bb = 512 // tt
        assert seq % tt == 0 and bsz % bb == 0
        bpt = seq // tt
        x_map = lambda i, j: (i // bpt, i % bpt, 0)
        mod_map = (lambda i, j: (i // bpt, 0, 0)) if per_batch else (lambda i, j: (0, 0, 0))
        out_spec = pl.BlockSpec((bb, tt, d), x_map)
        out_shape = jax.ShapeDtypeStruct((bsz, seq, d), F32)
        n_blocks = (bsz // bb) * bpt
    mb = bb if per_batch else 1
    tm = bb * tt
    kern = functools.partial(_ffn_kernel, n_f=n_f, time_major=time_major)
    return pl.pallas_call(
        kern,
        grid=(n_blocks, n_f),
        in_specs=[pl.BlockSpec((bb, tt, d), x_map),
                  pl.BlockSpec((1, d), lambda i, j: (0, 0)),
                  pl.BlockSpec((mb, 1, d), mod_map),
                  pl.BlockSpec((mb, 1, d), mod_map),
                  pl.BlockSpec((mb, 1, d), mod_map),
                  pl.BlockSpec((d, tf), lambda i, j: (0, j)),
                  pl.BlockSpec((d, tf), lambda i, j: (0, n_f + j)),
                  pl.BlockSpec((tf, d), lambda i, j: (j, 0))],
        out_specs=out_spec,
        out_shape=out_shape,
        scratch_shapes=[pltpu.VMEM((tm, d), BF16), pltpu.VMEM((tm, d), F32),
                        pltpu.VMEM((d // LANES, _tr_rows(bb, tt) if time_major else 8, LANES), F32)],
        compiler_params=_params("arbitrary", "arbitrary"),
        name="dense_swiglu",
    )(x3, g.reshape(1, d), shift, scale, gate, w_in, w_in, w_out)


def _s5_disc_kernel(lre_ref, lim_ref, step_ref, bre_ref, bim_ref, ar_ref, ai_ref, bbr_ref, bbi_ref):
    step = jnp.exp(step_ref[...])
    lre, lim = lre_ref[...], lim_ref[...]
    mag = jnp.exp(lre * step)
    ar = mag * jnp.cos(lim * step)
    ai = mag * jnp.sin(lim * step)
    den = lre * lre + lim * lim
    cr = ((ar - 1.0) * lre + ai * lim) / den
    ci = (ai * lre - (ar - 1.0) * lim) / den
    ar_ref[...] = ar
    ai_ref[...] = ai
    bbr_ref[...] = cr * bre_ref[...] - ci * bim_ref[...]
    bbi_ref[...] = cr * bim_ref[...] + ci * bre_ref[...]


def _s5_discretize(lam_re, lam_im, log_step, b_re, b_im):
    two, g, p = lam_re.shape
    c = b_re.shape[-1]
    lre = jnp.repeat(lam_re, c, axis=-1)
    lim = jnp.repeat(lam_im, c, axis=-1)
    shp = jax.ShapeDtypeStruct((two, g, p * c), F32)
    ar, ai, bbr, bbi = pl.pallas_call(
        _s5_disc_kernel, out_shape=[shp, shp, shp, shp], name="s5_discretize",
    )(lre, lim, log_step.reshape(two, g, 1), b_re.reshape(two, g, p * c), b_im.reshape(two, g, p * c))
    ar = ar.reshape(two, g, p, c)[..., 0]
    ai = ai.reshape(two, g, p, c)[..., 0]
    return ar, ai, bbr.reshape(two, g, p, c), bbi.reshape(two, g, p, c)


def _s5_kernel(x_ref, g_ref, sh_ref, sc_ref, wb_ref, wc_ref, a_ref, s0_ref, *rest, n_chunks, steps, bsz, n_cast):
    cast_in, (y_ref, sfin_ref), cast_out = rest[:n_cast], rest[n_cast:n_cast + 2], rest[n_cast + 2:2 * n_cast + 2]
    st_scr, bu0_scr, bu1_scr, sb0_scr, sb1_scr = rest[2 * n_cast + 2:]
    for src, dst in zip(cast_in, cast_out):
        dst[...] = src[...].astype(dst.dtype)
    bu_bufs = (bu0_scr, bu1_scr)
    sb_bufs = (sb0_scr, sb1_scr)
    d = pl.program_id(0)
    k = pl.program_id(1)
    rows, dm = x_ref.shape
    gl = S5_GB * S5_STATE
    cb = S5_GB * S5_CH

    @pl.when(k == 0)
    def _():
        st_scr[...] = s0_ref[0]

    hn = _rms(x_ref[...], g_ref[...]).reshape(steps, bsz, dm)
    h = hn * (1.0 + sc_ref[...])[None] + sh_ref[...][None]
    u = h.reshape(rows, dm).astype(BF16)

    n_gb = dm // cb

    def bu_dot(gb):
        return jnp.dot(u[:, gb * cb:(gb + 1) * cb], wb_ref[0, gb], preferred_element_type=F32)

    def sweep(reverse):
        bu_bufs[0][...] = bu_dot(0)
        for gb in range(n_gb):
            bu_cur, bu_nxt = bu_bufs[gb % 2], bu_bufs[1 - gb % 2]
            sb_cur = sb_bufs[gb % 2]
            if gb + 1 < n_gb:
                bu_nxt[...] = bu_dot(gb + 1)
            sl = slice(gb * gl, (gb + 1) * gl)
            ar = jnp.broadcast_to(a_ref[0, 0:1, sl], (bsz, gl))
            ai = jnp.broadcast_to(a_ref[0, 1:2, sl], (bsz, gl))
            sr, si = st_scr[0, :, sl], st_scr[1, :, sl]
            for i in range(steps):
                r0 = (steps - 1 - i if reverse else i) * bsz
                bur = bu_cur[r0:r0 + bsz, 0:gl]
                bui = bu_cur[r0:r0 + bsz, gl:2 * gl]
                sr, si = ar * sr - ai * si + bur, ar * si + ai * sr + bui
                sb_cur[r0:r0 + bsz, 0:gl] = sr.astype(BF16)
                sb_cur[r0:r0 + bsz, gl:2 * gl] = si.astype(BF16)
            st_scr[0, :, sl] = sr
            st_scr[1, :, sl] = si
            yv = jnp.dot(sb_cur[...], wc_ref[0, gb], preferred_element_type=F32)
            y_ref[0, :, gb * cb:(gb + 1) * cb] = yv.astype(y_ref.dtype)

    @pl.when(d == 0)
    def _():
        sweep(False)

    @pl.when(d == 1)
    def _():
        sweep(True)

    @pl.when(k == n_chunks - 1)
    def _():
        sfin_ref[0] = st_scr[...]


def _s5_scan(xt, g, shift, scale, wb, wc, a, s0, *, bsz, cast=()):
    m, d = xt.shape
    seq = m // bsz
    steps = min(64, seq)
    rows = steps * bsz
    nc = seq // steps
    nstate = s0.shape[-1]
    n_gb = d // (S5_GB * S5_CH)
    gl = S5_GB * S5_STATE

    def chunk(dd, k):
        return k + dd * (nc - 1 - 2 * k)

    n_steps = 2 * nc
    cast_specs, cast_shapes = [], []
    for w in cast:
        slab = w.shape[0] // n_steps
        assert slab * n_steps == w.shape[0] and slab % 16 == 0
        cast_specs.append(pl.BlockSpec((slab, w.shape[1]), lambda dd, k: (dd * nc + k, 0)))
        cast_shapes.append(jax.ShapeDtypeStruct(w.shape, BF16))

    kern = functools.partial(_s5_kernel, n_chunks=nc, steps=steps, bsz=bsz, n_cast=len(cast))
    return pl.pallas_call(
        kern,
        grid=(2, nc),
        in_specs=[pl.BlockSpec((rows, d), lambda dd, k: (chunk(dd, k), 0)),
                  pl.BlockSpec((1, d), lambda dd, k: (0, 0)),
                  pl.BlockSpec((bsz, d), lambda dd, k: (0, 0)),
                  pl.BlockSpec((bsz, d), lambda dd, k: (0, 0)),
                  pl.BlockSpec((1, n_gb, S5_GB * S5_CH, 2 * gl), lambda dd, k: (dd, 0, 0, 0),
                               pipeline_mode=pl.Buffered(1)),
                  pl.BlockSpec((1, n_gb, 2 * gl, S5_GB * S5_CH), lambda dd, k: (dd, 0, 0, 0),
                               pipeline_mode=pl.Buffered(1)),
                  pl.BlockSpec((1, 2, nstate), lambda dd, k: (dd, 0, 0)),
                  pl.BlockSpec((1, 2, bsz, nstate), lambda dd, k: (dd, 0, 0, 0))] + cast_specs,
        out_specs=[pl.BlockSpec((1, rows, d), lambda dd, k: (dd, chunk(dd, k), 0)),
                   pl.BlockSpec((1, 2, bsz, nstate), lambda dd, k: (dd, 0, 0, 0))] + cast_specs,
        out_shape=[jax.ShapeDtypeStruct((2, m, d), BF16),
                   jax.ShapeDtypeStruct((2, 2, bsz, nstate), F32)] + cast_shapes,
        scratch_shapes=[pltpu.VMEM((2, bsz, nstate), F32),
                        pltpu.VMEM((rows, 2 * gl), F32), pltpu.VMEM((rows, 2 * gl), F32),
                        pltpu.VMEM((rows, 2 * gl), BF16), pltpu.VMEM((rows, 2 * gl), BF16)],
        compiler_params=_params("arbitrary", "arbitrary"),
        name="s5_scan",
    )(xt, g.reshape(1, d), shift, scale, wb, wc, a, s0, *cast)


def _gelu_tanh(v):
    return 0.5 * v * (1.0 + jnp.tanh(0.7978845608028654 * (v + 0.044715 * (v * v * v))))


def _s5_glu_kernel(x_ref, y_ref, g_ref, sh_ref, sc_ref, gate_ref, dsk_ref, w_ref, b_ref, o_ref, *, bsz):
    rows, dm = x_ref.shape
    x = x_ref[...]
    hn = _rms(x, g_ref[...]).reshape(rows // bsz, bsz, dm)
    h = (hn * (1.0 + sc_ref[...])[None] + sh_ref[...][None]).reshape(rows, dm)
    yv = dsk_ref[...] * h + y_ref[0].astype(F32) + y_ref[1].astype(F32)
    ge = _gelu_tanh(yv).astype(BF16)
    o = jnp.dot(ge, w_ref[...], preferred_element_type=F32) + b_ref[...]
    out = o[:, :dm] * jax.nn.sigmoid(o[:, dm:])
    res = x.reshape(rows // bsz, bsz, dm) + gate_ref[...][None] * out.reshape(rows // bsz, bsz, dm)
    o_ref[...] = res.reshape(rows, dm)


def _s5_glu(xt, y, g, shift, scale, gate, d_skip, w_glu, b_glu, *, bsz):
    m, d = xt.shape
    tm = min(512, m)
    kern = functools.partial(_s5_glu_kernel, bsz=bsz)
    full = lambda i: (0, 0)
    return pl.pallas_call(
        kern,
        grid=(m // tm,),
        in_specs=[pl.BlockSpec((tm, d), lambda i: (i, 0)),
                  pl.BlockSpec((2, tm, d), lambda i: (0, i, 0)),
                  pl.BlockSpec((1, d), full),
                  pl.BlockSpec((bsz, d), full),
                  pl.BlockSpec((bsz, d), full),
                  pl.BlockSpec((bsz, d), full),
                  pl.BlockSpec((1, d), full),
                  pl.BlockSpec((d, 2 * d), full),
                  pl.BlockSpec((1, 2 * d), full)],
        out_specs=pl.BlockSpec((tm, d), lambda i: (i, 0)),
        out_shape=jax.ShapeDtypeStruct((m, d), F32),
        compiler_params=_params("arbitrary"),
        name="s5_glu",
    )(xt, y, g.reshape(1, d), shift, scale, gate, d_skip.reshape(1, d), w_glu, b_glu.reshape(1, 2 * d))


MOE_TB = 512
MOE_TILE = 512
MOE_WIN = 256
MOE_PAD = 16
MOE_LANES = 128


def _moe_cap(n_tok, n_blocks):
    rows = n_tok + n_blocks * MOE_PAD + MOE_TILE
    return -(-rows // MOE_TILE) * MOE_TILE


def _moe_route_kernel(x_ref, g_ref, sh_ref, sc_ref, rwT_ref, rbT_ref, ut_ref,
                      sorted_ref, meta_ref, tab_ref,
                      buf_scr, zero_scr, run_smem, sems, *, bsz, n_exp, n_blocks, local_rows):
    b = pl.program_id(0)
    slot = lax.rem(b, 2)
    dm = x_ref.shape[-1]
    rows = x_ref.shape[0] * x_ref.shape[1] * x_ref.shape[2]

    @pl.when(b == 0)
    def _():
        for e in range(n_exp):
            run_smem[e] = 0
        zero_scr[...] = jnp.zeros_like(zero_scr)

    hn = _rms(x_ref[...].reshape(rows // bsz, bsz, dm), g_ref[...])
    h = (hn * (1.0 + sc_ref[...])[None] + sh_ref[...][None]).reshape(rows, dm)
    hb = h.astype(BF16)

    logits = _nt_dot(rwT_ref[...], h, precision=HIGHEST) + rbT_ref[...]
    sub = lax.broadcasted_iota(jnp.int32, logits.shape, 0)
    m1 = jnp.max(logits, axis=0, keepdims=True)
    i1 = jnp.min(jnp.where(logits == m1, sub, n_exp), axis=0, keepdims=True)
    rest = jnp.where(sub == i1, -jnp.inf, logits)
    m2 = jnp.max(rest, axis=0, keepdims=True)
    i2 = jnp.min(jnp.where(rest == m2, sub, n_exp), axis=0, keepdims=True)
    e2 = jnp.exp(m2 - m1)
    g1 = 1.0 / (1.0 + e2)
    g2 = e2 * g1
    sel1 = sub == i1
    sel2 = sub == i2
    oh = jnp.where(sel1, 1.0, 0.0) + jnp.where(sel2, 1.0, 0.0)
    cnt = jnp.sum(oh, axis=1, keepdims=True).astype(jnp.int32)
    cpad_v = ((cnt + (MOE_PAD - 1)) // MOE_PAD) * MOE_PAD
    rank = jnp.dot(oh.astype(BF16), ut_ref[...], preferred_element_type=F32)

    cpad, loc, run = [], [], []
    off = 0
    for e in range(n_exp):
        cpad.append(cpad_v[e, 0])
        loc.append(off)
        off = off + cpad[e]
        run.append(run_smem[e])
    sub1 = lax.broadcasted_iota(jnp.int32, (n_exp, 1), 0)
    loc_v = jnp.zeros((n_exp, 1), jnp.int32)
    run_v = jnp.zeros((n_exp, 1), jnp.int32)
    for e in range(n_exp):
        loc_v = jnp.where(sub1 == e, loc[e], loc_v)
        run_v = jnp.where(sub1 == e, run[e], run_v)
    loc_v = loc_v.astype(F32)
    run_v = run_v.astype(F32)

    def pick(sel, v):
        return jnp.sum(jnp.where(sel, v, 0.0), axis=0, keepdims=True)

    lp1 = pick(sel1, loc_v + rank).astype(jnp.int32)
    lp2 = pick(sel2, loc_v + rank).astype(jnp.int32)
    pos1 = pick(sel1, run_v + rank)
    pos2 = pick(sel2, run_v + rank)

    r = lax.broadcasted_iota(jnp.int32, (local_rows, rows), 0)
    onehot = jnp.where(r == lp1, 1.0, jnp.where(r == lp2, 1.0, 0.0)).astype(BF16)
    buf_scr[slot, 0:local_rows, :] = jnp.dot(onehot, hb, preferred_element_type=F32).astype(BF16)
    buf_scr[slot, local_rows:local_rows + MOE_TILE, :] = jnp.zeros((MOE_TILE, dm), BF16)

    rec = jnp.concatenate([i1.astype(F32), i2.astype(F32), pos1, pos2, g1, g2,
                           jnp.zeros((MOE_LANES - 6, rows), F32)], axis=0)
    meta_ref[...] = rec.T

    base = b * (2 * n_exp)
    for e in range(n_exp):
        tab_ref[base + e] = run[e]
        tab_ref[base + n_exp + e] = cpad[e]
        run_smem[e] = run[e] + cpad[e]

    def group_copy(e, slot_, loc_e, run_e):
        src = buf_scr.at[slot_, pl.ds(pl.multiple_of(loc_e, MOE_PAD), MOE_TILE)]
        dst = sorted_ref.at[e, pl.ds(pl.multiple_of(run_e, MOE_PAD), MOE_TILE)]
        return pltpu.make_async_copy(src, dst, sems.at[slot_, e])

    @pl.when(b > 0)
    def _():
        for e in range(n_exp):
            group_copy(e, 1 - slot, 0, 0).wait()

    for e in range(n_exp):
        group_copy(e, slot, loc[e], run[e]).start()

    @pl.when(b == n_blocks - 1)
    def _():
        fin = (n_blocks) * (2 * n_exp)
        for e in range(n_exp):
            group_copy(e, slot, 0, 0).wait()
        for e in range(n_exp):
            tot = run[e] + cpad[e]
            tab_ref[fin + e] = tot
            tab_ref[fin + n_exp + e] = 0
            dst = sorted_ref.at[e, pl.ds(pl.multiple_of(tot, MOE_PAD), MOE_TILE)]
            pltpu.make_async_copy(zero_scr, dst, sems.at[slot, e]).start()
        for e in range(n_exp):
            dst = sorted_ref.at[e, pl.ds(0, MOE_TILE)]
            pltpu.make_async_copy(zero_scr, dst, sems.at[slot, e]).wait()


def _moe_expert_kernel(te_ref, tr_ref, na_ref, xs_ref, wg_ref, wu_ref, wo_ref, y_ref, acc_scr, *, n_f):
    i = pl.program_id(0)
    f = pl.program_id(1)

    @pl.when(i < na_ref[0])
    def _():
        @pl.when(f == 0)
        def _():
            acc_scr[...] = jnp.zeros_like(acc_scr)

        xs = xs_ref[0]
        gv = jnp.dot(xs, wg_ref[0], preferred_element_type=F32)
        uv = jnp.dot(xs, wu_ref[0], preferred_element_type=F32)
        a = (_silu(gv) * uv).astype(BF16)
        acc_scr[...] += jnp.dot(a, wo_ref[0], preferred_element_type=F32)

        @pl.when(f == n_f - 1)
        def _():
            y_ref[0] = acc_scr[...].astype(y_ref.dtype)


def _moe_combine_kernel(tab_ref, end_ref, x_ref, meta_ref, gate_ref, gf_ref, ys_ref, o_ref,
                        win_scr, acc_scr, tr_scr, sems, *, bsz, n_exp):
    b = pl.program_id(0)
    slot = lax.rem(b, 2)
    n_cols, n_rows, _, dm = x_ref.shape
    rows = n_cols * n_rows * bsz
    n_win = MOE_TILE // MOE_WIN

    def window(blk, slot_, e, c):
        base = blk * (2 * n_exp)
        run_e = tab_ref[base + e]
        live = tab_ref[base + n_exp + e] > c * MOE_WIN
        ws = jnp.minimum(run_e + c * MOE_WIN, end_ref[e] - MOE_WIN)
        k = e * n_win + c
        cp = pltpu.make_async_copy(ys_ref.at[e, pl.ds(pl.multiple_of(ws, MOE_PAD), MOE_WIN)],
                                   win_scr.at[slot_, k], sems.at[slot_, k])
        return live, ws, k, cp

    def fetch(blk, slot_):
        for e in range(n_exp):
            for c in range(n_win):
                live, _, _, cp = window(blk, slot_, e, c)

                @pl.when(live)
                def _():
                    cp.start()

    @pl.when(b == 0)
    def _():
        fetch(0, 0)

    fetch(b + 1, 1 - slot)

    meta = meta_ref[...]
    e1, e2 = meta[:, 0:1], meta[:, 1:2]
    pos1, pos2 = meta[:, 2:3].astype(jnp.int32), meta[:, 3:4].astype(jnp.int32)
    g1, g2 = meta[:, 4:5], meta[:, 5:6]
    acc_scr[...] = jnp.zeros_like(acc_scr)
    lane = lax.broadcasted_iota(jnp.int32, (rows, MOE_WIN), 1)

    for e in range(n_exp):
        for c in range(n_win):
            live, ws, k, cp = window(b, slot, e, c)

            @pl.when(live)
            def _():
                cp.wait()
                rel = jnp.where(e1 == e, pos1, pos2) - ws
                hit = jnp.where(e1 == e, 1.0, jnp.where(e2 == e, 1.0, 0.0))
                pe = jnp.where(rel == lane, hit, 0.0).astype(BF16)
                ge = jnp.where(e1 == e, g1, jnp.where(e2 == e, g2, 0.0))
                acc_scr[...] += ge * jnp.dot(pe, win_scr[slot, k], preferred_element_type=F32)

    upd = gate_ref[...][None] * acc_scr[...].reshape(rows // bsz, bsz, dm)
    xo = x_ref[...].reshape(rows, dm) + upd.reshape(rows, dm)
    _fill_lane_tiles(_rms(xo, gf_ref[...]), tr_scr, n_rows * bsz)
    for rl in range(n_rows):
        _strided_rows_out(
            tr_scr, lambda bb, lt: o_ref.at[bb, rl, :, lt * LANES:(lt + 1) * LANES],
            n_groups=bsz, group_rows=n_cols, stride=n_rows * bsz + TR_SKEW, base=rl * bsz)


def _moe(xt, g, shift, scale, gate, router_w, router_b, w_in, w_out, g_final, *, bsz, grid_rows):
    m, d = xt.shape
    n_exp, f, _ = w_out.shape
    nb = m // MOE_TB
    cap = _moe_cap(m, nb)
    local_rows = -(-(TOP_K * MOE_TB + n_exp * (MOE_PAD - 1)) // 128) * 128
    full1 = lambda i: (0, 0)

    blk_cols = 8
    blk_rows = MOE_TB // (blk_cols * bsz)
    assert blk_rows * blk_cols * bsz == MOE_TB and grid_rows % blk_rows == 0 and GRID_W % blk_cols == 0
    nrg = grid_rows // blk_rows
    x4 = xt.reshape(GRID_W, grid_rows, bsz, d)
    x_spec1 = pl.BlockSpec((blk_cols, blk_rows, bsz, d), lambda i: (i // nrg, i % nrg, 0, 0))

    ut = (jnp.arange(MOE_TB)[:, None] < jnp.arange(MOE_TB)[None, :]).astype(BF16)
    route = functools.partial(_moe_route_kernel, bsz=bsz, n_exp=n_exp, n_blocks=nb, local_rows=local_rows)
    sorted_h, meta, tab = pl.pallas_call(
        route,
        grid=(nb,),
        in_specs=[x_spec1,
                  pl.BlockSpec((1, d), full1),
                  pl.BlockSpec((bsz, d), full1),
                  pl.BlockSpec((bsz, d), full1),
                  pl.BlockSpec((n_exp, d), full1),
                  pl.BlockSpec((n_exp, 1), full1),
                  pl.BlockSpec((MOE_TB, MOE_TB), full1)],
        out_specs=[pl.BlockSpec(memory_space=pl.ANY),
                   pl.BlockSpec((MOE_TB, MOE_LANES), lambda i: (i, 0)),
                   pl.BlockSpec(memory_space=pltpu.SMEM)],
        out_shape=[jax.ShapeDtypeStruct((n_exp, cap, d), BF16),
                   jax.ShapeDtypeStruct((m, MOE_LANES), F32),
                   jax.ShapeDtypeStruct(((nb + 1) * 2 * n_exp,), jnp.int32)],
        scratch_shapes=[pltpu.VMEM((2, local_rows + MOE_TILE, d), BF16),
                        pltpu.VMEM((MOE_TILE, d), BF16),
                        pltpu.SMEM((n_exp,), jnp.int32),
                        pltpu.SemaphoreType.DMA((2, n_exp))],
        compiler_params=_params("arbitrary"),
        name="moe_route",
    )(x4, g.reshape(1, d), shift, scale, router_w.T, router_b.reshape(n_exp, 1), ut)

    tot = tab[nb * 2 * n_exp: nb * 2 * n_exp + n_exp]
    ntile = jnp.maximum((tot + MOE_TILE - 1) // MOE_TILE, 1)
    cum = jnp.cumsum(ntile)
    n_active = cum[-1:]
    max_tiles = (TOP_K * m + nb * n_exp * (MOE_PAD - 1)) // MOE_TILE + n_exp + 1
    idc = jnp.minimum(jnp.arange(max_tiles, dtype=jnp.int32), n_active - 1)
    tile_e = jnp.sum(idc[:, None] >= cum[None, :], axis=1).astype(jnp.int32)
    tile_r = (idc - (cum - ntile)[tile_e]).astype(jnp.int32)
    end = (ntile * MOE_TILE).astype(jnp.int32)

    tf = f // 2 if (f // 2) % 128 == 0 else f
    n_f = f // tf

    def f_idx(i, j, na):
        return jnp.where(i < na[0], j, n_f - 1)

    ys = pl.pallas_call(
        functools.partial(_moe_expert_kernel, n_f=n_f),
        grid_spec=pltpu.PrefetchScalarGridSpec(
            num_scalar_prefetch=3,
            grid=(max_tiles, n_f),
            in_specs=[pl.BlockSpec((1, MOE_TILE, d), lambda i, j, te, tr, na: (te[i], tr[i], 0)),
                      pl.BlockSpec((1, d, tf), lambda i, j, te, tr, na: (te[i], 0, f_idx(i, j, na))),
                      pl.BlockSpec((1, d, tf), lambda i, j, te, tr, na: (te[i], 0, n_f + f_idx(i, j, na))),
                      pl.BlockSpec((1, tf, d), lambda i, j, te, tr, na: (te[i], f_idx(i, j, na), 0))],
            out_specs=pl.BlockSpec((1, MOE_TILE, d), lambda i, j, te, tr, na: (te[i], tr[i], 0)),
            scratch_shapes=[pltpu.VMEM((MOE_TILE, d), F32)]),
        out_shape=jax.ShapeDtypeStruct((n_exp, cap, d), BF16),
        compiler_params=_params("arbitrary", "arbitrary"),
        name="moe_experts",
    )(tile_e, tile_r, n_active.astype(jnp.int32), sorted_h, w_in, w_in, w_out)

    n_win = MOE_TILE // MOE_WIN
    out = pl.pallas_call(
        functools.partial(_moe_combine_kernel, bsz=bsz, n_exp=n_exp),
        grid_spec=pltpu.PrefetchScalarGridSpec(
            num_scalar_prefetch=2,
            grid=(nb,),
            in_specs=[pl.BlockSpec((blk_cols, blk_rows, bsz, d), lambda i, tb, en: (i // nrg, i % nrg, 0, 0)),
                      pl.BlockSpec((MOE_TB, MOE_LANES), lambda i, tb, en: (i, 0)),
                      pl.BlockSpec((bsz, d), lambda i, tb, en: (0, 0)),
                      pl.BlockSpec((1, d), lambda i, tb, en: (0, 0)),
                      pl.BlockSpec(memory_space=pl.ANY)],
            out_specs=pl.BlockSpec((bsz, blk_rows, blk_cols, d), lambda i, tb, en: (0, i % nrg, i // nrg, 0)),
            scratch_shapes=[pltpu.VMEM((2, n_exp * n_win, MOE_WIN, d), BF16),
                            pltpu.VMEM((MOE_TB, d), F32),
                            pltpu.VMEM((d // LANES, _tr_rows(blk_cols, blk_rows * bsz), LANES), F32),
                            pltpu.SemaphoreType.DMA((2, n_exp * n_win))]),
        out_shape=jax.ShapeDtypeStruct((bsz, grid_rows, GRID_W, d), F32),
        compiler_params=_params("arbitrary"),
        name="moe_combine",
    )(tab, end, x4, meta, gate, g_final.reshape(1, d), ys)
    return out.reshape(bsz, grid_rows * GRID_W, d)


def _blockdiag(w, n_gb):
    two, g, r, c = w.shape
    eye = jnp.eye(S5_GB, dtype=w.dtype)
    out = jnp.einsum("dbgrc,gh->dbgrhc", w.reshape(two, n_gb, S5_GB, r, c), eye)
    return out.reshape(two, n_gb, S5_GB * r, S5_GB * c)


def kernel(x, c, ctx, c_ctx, ada_w, ada_b, norm_mix, norm_ffn, ssd_w_in, ssd_conv_w, ssd_conv_b, ssd_dt_bias, ssd_a_log, ssd_d, ssd_norm, ssd_w_out, s5_lam_re, s5_lam_im, s5_log_step, s5_b_re, s5_b_im, s5_c_re, s5_c_im, s5_d, s5_w_glu, s5_b_glu, ffn_w_in, ffn_w_out, moe_router_w, moe_router_b, moe_w_in, moe_w_out, norm_final):
    bsz, seq, d = x.shape
    ctx_len = ctx.shape[1]
    depth = ada_w.shape[0]
    assert depth == 2, "one SSD layer followed by one S5 layer"
    rows = seq // GRID_W

    pad_rows = (-(bsz + 1)) % 8
    cc = jnp.concatenate([c, c_ctx[None], jnp.zeros((pad_rows, d), F32)], axis=0)
    mods = _ada(cc, ada_w, ada_b)

    def mod(i, n):
        sl = slice(n * d, (n + 1) * d)
        return mods[i, :bsz, sl], mods[i, bsz:bsz + 1, sl]

    heads = ssd_d.shape[1]
    d_inner = heads * SSD_HEADDIM
    conv_dim = ssd_conv_w.shape[2]
    n_main = d_inner + conv_dim
    w_in0 = ssd_w_in[0]
    w_main = w_in0[:, :n_main].astype(BF16)
    w_dt = w_in0[:, n_main:]
    conv_w_full = jnp.concatenate([jnp.zeros((SSD_CONV, d_inner), F32), 0.5 * ssd_conv_w[0]], axis=1)
    conv_b_full = jnp.concatenate([jnp.zeros((d_inner,), F32), 0.5 * ssd_conv_b[0]])
    d_exp = jnp.repeat(ssd_d[0], SSD_HEADDIM)
    w_out0 = ssd_w_out[0].astype(BF16)
    gw = (heads // SSD_GROUPS) * SSD_HEADDIM

    (sh_x, sh_c), (sc_x, sc_c), (gt_x, gt_c) = mod(0, 0), mod(0, 1), mod(0, 2)
    r3 = lambda a: a[:, None, :]
    dt_bias = ssd_dt_bias[0].reshape(-1)
    a_log = ssd_a_log[0].reshape(-1)

    def ssd_mixer(tok3, shift, scale, gate, h0):
        b_, t_, _ = tok3.shape
        zx, dtc, dtcT = _ssd_inproj(tok3, norm_mix[0], r3(shift), r3(scale), w_main, conv_w_full,
                                    conv_b_full, w_dt, dt_bias, a_log, d_inner=d_inner, heads=heads)
        y, hfin = _ssd_scan(zx, dtc, dtcT, h0, d_inner=d_inner, heads=heads)
        out = _ssd_out(y.reshape(2, b_ * t_, d_inner), zx.reshape(b_ * t_, n_main), d_exp, ssd_norm[0],
                       w_out0, tok3.reshape(b_ * t_, d), r3(gate), seq=t_, d_inner=d_inner)
        return out, hfin

    h_zero = jnp.zeros((bsz, 2, SSD_GROUPS, SSD_STATE, gw), F32)
    ctx2, h_ctx = ssd_mixer(ctx, sh_c, sc_c, gt_c, h_zero)
    x2, _ = ssd_mixer(x, sh_x, sc_x, gt_x, h_ctx)

    ffn_in = ffn_w_in[0].astype(BF16)
    ffn_out = ffn_w_out[0].astype(BF16)
    (sh_x, sh_c), (sc_x, sc_c), (gt_x, gt_c) = mod(0, 3), mod(0, 4), mod(0, 5)
    xt = _ffn(x2.reshape(bsz, seq, d), norm_ffn[0], r3(sh_x), r3(sc_x), r3(gt_x), ffn_in, ffn_out,
              time_major=True).reshape(seq * bsz, d)
    ctx2 = _ffn(ctx2.reshape(bsz, ctx_len, d), norm_ffn[0], r3(sh_c), r3(sc_c), r3(gt_c), ffn_in, ffn_out)

    ct = ctx2.reshape(bsz, ctx_len, d).transpose(1, 0, 2).reshape(ctx_len * bsz, d)

    ar, ai, bbr, bbi = _s5_discretize(s5_lam_re[0], s5_lam_im[0], s5_log_step[0], s5_b_re[0], s5_b_im[0])
    n_groups = ar.shape[1]
    n_gb = n_groups // S5_GB
    wb = jnp.concatenate([_blockdiag(bbr.transpose(0, 1, 3, 2), n_gb),
                          _blockdiag(bbi.transpose(0, 1, 3, 2), n_gb)], axis=-1).astype(BF16)
    wc = jnp.concatenate([_blockdiag(s5_c_re[0].transpose(0, 1, 3, 2), n_gb),
                          _blockdiag(-s5_c_im[0].transpose(0, 1, 3, 2), n_gb)], axis=2).astype(BF16)
    a_coef = jnp.stack([ar.reshape(2, -1), ai.reshape(2, -1)], axis=1)
    nstate = a_coef.shape[-1]

    (sh_x, sh_c), (sc_x, sc_c), (gt_x, _) = mod(1, 0), mod(1, 1), mod(1, 2)
    bc = lambda a: jnp.broadcast_to(a, (bsz, d))
    s_zero = jnp.zeros((2, 2, bsz, nstate), F32)
    _, s_ctx = _s5_scan(ct, norm_mix[1], bc(sh_c), bc(sc_c), wb, wc, a_coef, s_zero, bsz=bsz)
    n_exp, _, two_f = moe_w_in.shape[1:]
    y, _, w_in_b, w_out_b = _s5_scan(xt, norm_mix[1], sh_x, sc_x, wb, wc, a_coef, s_ctx, bsz=bsz,
                                     cast=(moe_w_in[0].reshape(n_exp * d, two_f),
                                           moe_w_out[0].reshape(n_exp * (two_f // 2), d)))
    w_in_b = w_in_b.reshape(n_exp, d, two_f)
    w_out_b = w_out_b.reshape(n_exp, two_f // 2, d)
    xt = _s5_glu(xt, y, norm_mix[1], sh_x, sc_x, gt_x, s5_d[0], s5_w_glu[0].astype(BF16), s5_b_glu[0], bsz=bsz)

    (sh_x, _), (sc_x, _), (gt_x, _) = mod(1, 3), mod(1, 4), mod(1, 5)
    return _moe(xt, norm_ffn[1], sh_x, sc_x, gt_x, moe_router_w[0], moe_router_b[0],
                w_in_b, w_out_b, norm_final, bsz=bsz, grid_rows=rows)
```

```python
import functools

import jax
import jax.numpy as jnp
from jax import lax
from jax.experimental import pallas as pl
from jax.experimental.pallas import tpu as pltpu

F32 = jnp.float32
BF16 = jnp.bfloat16
HIGHEST = lax.Precision.HIGHEST

EPS = 1e-6
GRID_W = 64
N_MOD = 6

SSD_HEADDIM = 64
SSD_GROUPS = 4
SSD_STATE = 128
SSD_CONV = 5
SSD_CHUNK = 128

S5_CH = 16
S5_STATE = 64
S5_GB = 8

TOP_K = 2

V7X_VMEM_BYTES = 64 * 1024 * 1024
VMEM_LIMIT = V7X_VMEM_BYTES - 4 * 1024 * 1024
NEG_BIG = -1e30
LANES = 128
LOG2E = 1.4426950408889634


TR_SKEW = 8


def _tr_rows(n_groups, group):
    return n_groups * (group + TR_SKEW)


def _fill_lane_tiles(res, tr_scr, group):
    pitch = group + TR_SKEW
    for lt in range(res.shape[1] // LANES):
        for g in range(res.shape[0] // group):
            tr_scr[lt, g * pitch:g * pitch + group, :] = res[g * group:(g + 1) * group, lt * LANES:(lt + 1) * LANES]


def _strided_rows_out(tr_scr, dst, *, n_groups, group_rows, stride, base=0):
    for g in range(n_groups):
        for lt in range(tr_scr.shape[0]):
            dst(g, lt)[...] = tr_scr[lt, pl.ds(base + g, group_rows, stride=stride), :]


def _params(*sem):
    return pltpu.CompilerParams(dimension_semantics=sem, vmem_limit_bytes=VMEM_LIMIT)


def _silu(v):
    return v * jax.nn.sigmoid(v)


def _softplus(v):
    return jnp.maximum(v, 0.0) + jnp.log1p(jnp.exp(-jnp.abs(v)))


def _rms(x, g):
    inv = lax.rsqrt(jnp.mean(x * x, axis=-1, keepdims=True) + EPS)
    return (x * inv) * g


def _nt_dot(a, b, precision=None):
    return lax.dot_general(a, b, (((1,), (1,)), ((), ())), precision=precision,
                           preferred_element_type=F32)


def _ada_kernel(c_ref, w_ref, b_ref, o_ref):
    s = _silu(c_ref[...])
    o_ref[0] = jnp.dot(s, w_ref[0], precision=HIGHEST, preferred_element_type=F32) + b_ref[0]


def _ada(cc, ada_w, ada_b):
    depth, d, n = ada_w.shape
    r = cc.shape[0]
    tn = 1536
    return pl.pallas_call(
        _ada_kernel,
        grid=(depth, n // tn),
        in_specs=[pl.BlockSpec((r, d), lambda i, j: (0, 0)),
                  pl.BlockSpec((1, d, tn), lambda i, j: (i, 0, j)),
                  pl.BlockSpec((1, 1, tn), lambda i, j: (i, 0, j))],
        out_specs=pl.BlockSpec((1, r, tn), lambda i, j: (i, 0, j)),
        out_shape=jax.ShapeDtypeStruct((depth, r, n), F32),
        compiler_params=_params("arbitrary", "arbitrary"),
        name="ada_mod",
    )(cc, ada_w, ada_b.reshape(depth, 1, n))


def _ssd_inproj_kernel(x_ref, g_ref, sh_ref, sc_ref, w_ref, cw_ref, cb_ref, wdt_ref, wdtT_ref,
                       bdt_ref, bdtT_ref, alog_ref, alogT_ref,
                       o_ref, dt_ref, dtT_ref, h_scr, pad_scr, *, n_plain, seq, heads, row_chunk, dt_chunk):
    j = pl.program_id(1)
    pad = 8

    @pl.when(j == 0)
    def _():
        for r in range(0, seq, dt_chunk):
            rs = slice(r, r + dt_chunk)
            h = _rms(x_ref[0, rs, :], g_ref[...]) * (1.0 + sc_ref[0]) + sh_ref[0]
            hb = h.astype(BF16)
            h_scr[rs, :] = hb
            hl = (h - hb.astype(F32)).astype(BF16)
            nh2 = 2 * heads
            p = jnp.dot(hb, wdt_ref[...], preferred_element_type=F32)
            q = jnp.dot(hl, wdt_ref[...], preferred_element_type=F32)
            dt = _softplus(p[:, :nh2] + (p[:, nh2:] + q[:, :nh2]) + bdt_ref[...])
            da = dt * (-jnp.exp(alog_ref[...]))
            pT = _nt_dot(wdtT_ref[...], hb)
            qT = _nt_dot(wdtT_ref[...], hl)
            dtT = _softplus(pT[:nh2, :] + (pT[nh2:, :] + qT[:nh2, :]) + bdtT_ref[...])
            daT = dtT * (-jnp.exp(alogT_ref[...]))
            for d in range(2):
                sl = slice(d * heads, (d + 1) * heads)
                dt_ref[d, 0, rs, :] = jnp.concatenate([dt[:, sl], da[:, sl]], axis=-1)
                dtT_ref[d, 0, :, rs] = jnp.concatenate([dtT[sl, :], daT[sl, :]], axis=0)
        zeros = jnp.zeros((pad, pad_scr.shape[1]), F32)
        pad_scr[0:pad, :] = zeros
        pad_scr[pad + seq:2 * pad + seq, :] = zeros

    acc = jnp.dot(h_scr[...], w_ref[...], preferred_element_type=F32)

    @pl.when(j < n_plain)
    def _():
        o_ref[0] = acc.astype(o_ref.dtype)

    @pl.when(j >= n_plain)
    def _():
        pad_scr[pad:pad + seq, :] = acc
        half = (SSD_CONV - 1) // 2
        n_all = row_chunk + 2 * pad
        for r in range(0, seq, row_chunk):
            a = pad_scr[r:r + n_all, :]
            s = cb_ref[...] + cw_ref[half:half + 1, :] * a[pad:pad + row_chunk, :]
            for k in range(SSD_CONV):
                if k != half:
                    shifted = pltpu.roll(a, (half - k) % n_all, axis=0)
                    s = s + cw_ref[k:k + 1, :] * shifted[pad:pad + row_chunk, :]
            o_ref[0, r:r + row_chunk, :] = (s * (1.0 + jnp.tanh(s))).astype(o_ref.dtype)


def _ssd_inproj(x3, g, shift, scale, w_main, conv_w_full, conv_b_full, w_dt, dt_bias, a_log,
                *, d_inner, heads):
    bsz, seq, d = x3.shape
    n = w_main.shape[1]
    tn = 512 if seq > 512 else 1024
    nh2 = 2 * heads
    per_batch = shift.shape[0] > 1
    mod_map = (lambda b, j: (b, 0, 0)) if per_batch else (lambda b, j: (0, 0, 0))
    kern = functools.partial(_ssd_inproj_kernel, n_plain=d_inner // tn, seq=seq, heads=heads,
                             row_chunk=min(64, seq), dt_chunk=min(256, seq))
    w_hi = w_dt.astype(BF16)
    w_dt2 = jnp.concatenate([w_hi, (w_dt - w_hi.astype(F32)).astype(BF16)], axis=1)
    const2 = lambda b, j: (0, 0)
    return pl.pallas_call(
        kern,
        grid=(bsz, n // tn),
        in_specs=[pl.BlockSpec((1, seq, d), lambda b, j: (b, 0, 0), pipeline_mode=pl.Buffered(1)),
                  pl.BlockSpec((1, d), const2),
                  pl.BlockSpec((1, 1, d), mod_map),
                  pl.BlockSpec((1, 1, d), mod_map),
                  pl.BlockSpec((d, tn), lambda b, j: (0, j)),
                  pl.BlockSpec((SSD_CONV, tn), lambda b, j: (0, j)),
                  pl.BlockSpec((1, tn), lambda b, j: (0, j)),
                  pl.BlockSpec((d, 2 * nh2), const2),
                  pl.BlockSpec((2 * nh2, d), const2),
                  pl.BlockSpec((1, nh2), const2),
                  pl.BlockSpec((nh2, 1), const2),
                  pl.BlockSpec((1, nh2), const2),
                  pl.BlockSpec((nh2, 1), const2)],
        out_specs=[pl.BlockSpec((1, seq, tn), lambda b, j: (b, 0, j)),
                   pl.BlockSpec((2, 1, seq, nh2), lambda b, j: (0, b, 0, 0)),
                   pl.BlockSpec((2, 1, nh2, seq), lambda b, j: (0, b, 0, 0))],
        out_shape=[jax.ShapeDtypeStruct((bsz, seq, n), BF16),
                   jax.ShapeDtypeStruct((2, bsz, seq, nh2), F32),
                   jax.ShapeDtypeStruct((2, bsz, nh2, seq), F32)],
        scratch_shapes=[pltpu.VMEM((seq, d), BF16), pltpu.VMEM((seq + 16, tn), F32)],
        compiler_params=_params("arbitrary", "arbitrary"),
        name="ssd_inproj",
    )(x3, g.reshape(1, d), shift, scale, w_main, conv_w_full, conv_b_full.reshape(1, n),
      w_dt2, w_dt2.T, dt_bias.reshape(1, nh2), dt_bias.reshape(nh2, 1),
      a_log.reshape(1, nh2), a_log.reshape(nh2, 1))


def _ssd_scan_kernel(xs_ref, b_ref, c_ref, dtc_ref, dtcT_ref, h0_ref, y_ref, hfin_ref,
                     st_scr, *, heads, n_chunks):
    d = pl.program_id(1)
    k = pl.program_id(2)
    L = xs_ref.shape[1]
    hpg = heads // SSD_GROUPS
    pairs_per_group = hpg // 2
    gw = hpg * SSD_HEADDIM

    @pl.when(k == 0)
    def _():
        st_scr[...] = h0_ref[0, 0]

    row = lax.broadcasted_iota(jnp.int32, (L, L), 0)
    col = lax.broadcasted_iota(jnp.int32, (L, L), 1)
    mask = jnp.where(d == 0, row - col, col - row) >= 0
    mask_f = mask.astype(F32)

    dtc = dtc_ref[0, 0]
    dtcT = dtcT_ref[0, 0]
    assert L == 2 * SSD_HEADDIM, "the head-pair tiles assume chunk length == 2 * head dim"
    da = dtc[:, heads:]
    dtT, daT = dtcT[:heads, :], dtcT[heads:, :]
    cum = jnp.dot(mask_f, da, precision=HIGHEST, preferred_element_type=F32)
    cumT = _nt_dot(daT, mask_f, precision=HIGHEST)
    tot = jnp.sum(da, axis=0, keepdims=True)
    totT = jnp.sum(daT, axis=1, keepdims=True)
    etot = jnp.exp(tot)
    c2 = cum * LOG2E
    c2T = cumT * LOG2E
    wrow = jnp.exp(totT - cumT) * dtT

    lane = lax.broadcasted_iota(jnp.int32, (L, 2 * SSD_HEADDIM), 1)
    lo = lane < SSD_HEADDIM
    lo_row = lax.broadcasted_iota(jnp.int32, (1, 2 * SSD_HEADDIM), 1) < SSD_HEADDIM

    for g in range(SSD_GROUPS):
        bg = b_ref[0, :, g * SSD_STATE:(g + 1) * SSD_STATE]
        cg = c_ref[0, :, g * SSD_STATE:(g + 1) * SSD_STATE]
        scores = _nt_dot(cg, bg) * mask_f
        bgT = bg.astype(F32).T
        yoff = jnp.dot(cg, st_scr[g].astype(BF16), preferred_element_type=F32)
        for q in range(pairs_per_group):
            p = g * pairs_per_group + q
            h0, h1 = 2 * p, 2 * p + 1
            xs2 = xs_ref[0, :, p * 128:(p + 1) * 128]
            zero = jnp.zeros_like(xs2)
            rhs = jnp.concatenate([jnp.where(lo, xs2, zero), jnp.where(lo, zero, xs2)], axis=0)
            ms, bs, es = [], [], []
            for h in (h0, h1):
                col = jnp.broadcast_to(c2[:, h:h + 1], (L, L))
                dec = jnp.exp2(jnp.minimum(col - c2T[h:h + 1, :], 0.0))
                ms.append((scores * dec * dtT[h:h + 1, :]).astype(BF16))
                bs.append((bgT * wrow[h:h + 1, :]).astype(BF16))
                es.append(jnp.exp2(col))
            ydiag = jnp.dot(jnp.concatenate(ms, axis=1), rhs, preferred_element_type=F32)
            new = jnp.dot(jnp.concatenate(bs, axis=1), rhs, preferred_element_type=F32)
            sl = slice(q * 128, (q + 1) * 128)
            y2 = ydiag + yoff[:, sl] * jnp.where(lo, es[0], es[1])
            y_ref[0, 0, :, p * 128:(p + 1) * 128] = y2.astype(y_ref.dtype)
            et2 = jnp.where(lo_row, etot[:, h0:h0 + 1], etot[:, h1:h1 + 1])
            st_scr[g, :, sl] = st_scr[g, :, sl] * et2 + new

    @pl.when(k == n_chunks - 1)
    def _():
        hfin_ref[0, 0] = st_scr[...]


def _ssd_scan(zx, dtc, dtcT, h0, *, d_inner, heads):
    bsz, seq, _ = zx.shape
    L = min(SSD_CHUNK, seq)
    nc = seq // L
    bc_w = SSD_GROUPS * SSD_STATE
    gw = (heads // SSD_GROUPS) * SSD_HEADDIM
    nh2 = 2 * heads
    xs_blk = d_inner // d_inner
    b_blk = (2 * d_inner) // bc_w
    c_blk = b_blk + 1

    def chunk(d, k):
        return k + d * (nc - 1 - 2 * k)

    kern = functools.partial(_ssd_scan_kernel, heads=heads, n_chunks=nc)
    st_shape = (SSD_GROUPS, SSD_STATE, gw)
    return pl.pallas_call(
        kern,
        grid=(bsz, 2, nc),
        in_specs=[pl.BlockSpec((1, L, d_inner), lambda b, d, k: (b, chunk(d, k), xs_blk)),
                  pl.BlockSpec((1, L, bc_w), lambda b, d, k: (b, chunk(d, k), b_blk)),
                  pl.BlockSpec((1, L, bc_w), lambda b, d, k: (b, chunk(d, k), c_blk)),
                  pl.BlockSpec((1, 1, L, nh2), lambda b, d, k: (d, b, chunk(d, k), 0)),
                  pl.BlockSpec((1, 1, nh2, L), lambda b, d, k: (d, b, 0, chunk(d, k))),
                  pl.BlockSpec((1, 1) + st_shape, lambda b, d, k: (b, d, 0, 0, 0))],
        out_specs=[pl.BlockSpec((1, 1, L, d_inner), lambda b, d, k: (d, b, chunk(d, k), 0)),
                   pl.BlockSpec((1, 1) + st_shape, lambda b, d, k: (b, d, 0, 0, 0))],
        out_shape=[jax.ShapeDtypeStruct((2, bsz, seq, d_inner), BF16),
                   jax.ShapeDtypeStruct((bsz, 2) + st_shape, F32)],
        scratch_shapes=[pltpu.VMEM(st_shape, F32)],
        compiler_params=_params("arbitrary", "arbitrary", "arbitrary"),
        name="ssd_scan",
    )(zx, zx, zx, dtc, dtcT, h0)


def _ssd_out_kernel(y_ref, z_ref, xs_ref, dsk_ref, nw_ref, w_ref, x_ref, gate_ref, o_ref):
    y = y_ref[0].astype(F32) + y_ref[1].astype(F32) + dsk_ref[...] * xs_ref[...].astype(F32)
    hz = 0.5 * z_ref[...].astype(F32)
    yg = y * (hz * (1.0 + jnp.tanh(hz)))
    yn = _rms(yg, nw_ref[...]).astype(BF16)
    out = jnp.dot(yn, w_ref[...], preferred_element_type=F32)
    o_ref[...] = x_ref[...] + gate_ref[0] * out


def _ssd_out(y, zx, d_exp, norm_w, w_out, x2, gate, *, seq, d_inner):
    m, d = x2.shape
    tm = min(512, seq)
    bpt = seq // tm
    per_batch = gate.shape[0] > 1
    gate_map = (lambda i: (i // bpt, 0, 0)) if per_batch else (lambda i: (0, 0, 0))
    return pl.pallas_call(
        _ssd_out_kernel,
        grid=(m // tm,),
        in_specs=[pl.BlockSpec((2, tm, d_inner), lambda i: (0, i, 0)),
                  pl.BlockSpec((tm, d_inner), lambda i: (i, 0)),
                  pl.BlockSpec((tm, d_inner), lambda i: (i, 1)),
                  pl.BlockSpec((1, d_inner), lambda i: (0, 0)),
                  pl.BlockSpec((1, d_inner), lambda i: (0, 0)),
                  pl.BlockSpec((d_inner, d), lambda i: (0, 0)),
                  pl.BlockSpec((tm, d), lambda i: (i, 0)),
                  pl.BlockSpec((1, 1, d), gate_map)],
        out_specs=pl.BlockSpec((tm, d), lambda i: (i, 0)),
        out_shape=jax.ShapeDtypeStruct((m, d), F32),
        compiler_params=_params("arbitrary"),
        name="ssd_out",
    )(y, zx, zx, d_exp.reshape(1, d_inner), norm_w.reshape(1, d_inner), w_out, x2, gate)


def _ffn_kernel(x_ref, g_ref, sh_ref, sc_ref, gate_ref, wg_ref, wu_ref, wo_ref, o_ref,
                h_scr, acc_scr, tr_scr, *, n_f, time_major):
    j = pl.program_id(1)
    bb, tt, dm = x_ref.shape

    @pl.when(j == 0)
    def _():
        h = _rms(x_ref[...], g_ref[...]) * (1.0 + sc_ref[...]) + sh_ref[...]
        h_scr[...] = h.reshape(bb * tt, dm).astype(BF16)
        acc_scr[...] = jnp.zeros_like(acc_scr)

    h = h_scr[...]
    gv = jnp.dot(h, wg_ref[...], preferred_element_type=F32)
    uv = jnp.dot(h, wu_ref[...], preferred_element_type=F32)
    a = (_silu(gv) * uv).astype(BF16)
    acc_scr[...] += jnp.dot(a, wo_ref[...], preferred_element_type=F32)

    @pl.when(j == n_f - 1)
    def _():
        res = x_ref[...] + gate_ref[...] * acc_scr[...].reshape(bb, tt, dm)
        if time_major:
            _fill_lane_tiles(res.reshape(bb * tt, dm), tr_scr, tt)
            _strided_rows_out(tr_scr, lambda c, lt: o_ref.at[c, 0, :, lt * LANES:(lt + 1) * LANES],
                              n_groups=tt, group_rows=bb, stride=tt + TR_SKEW)
        else:
            o_ref[...] = res


def _ffn(x3, g, shift, scale, gate, w_in, w_out, *, time_major=False):
    bsz, seq, d = x3.shape
    f = w_out.shape[0]
    tf = f // 2 if (f // 2) % 128 == 0 else f
    n_f = f // tf
    per_batch = shift.shape[0] > 1
    if time_major:
        bb, tt = 8, GRID_W
        assert bsz % bb == 0 and seq % tt == 0 and per_batch
        rows = seq // tt
        x_map = lambda i, j: (i // rows, i % rows, 0)
        mod_map = lambda i, j: (i // rows, 0, 0)
        out_spec = pl.BlockSpec((tt, 1, bb, d), lambda i, j: (0, i % rows, i // rows, 0))
        out_shape = jax.ShapeDtypeStruct((tt, rows, bsz, d), F32)
        n_blocks = (bsz // bb) * rows
    else:
        tt = min(512, seq)
        bb = 512 // tt
        assert seq % tt == 0 and bsz % bb == 0
        bpt = seq // tt
        x_map = lambda i, j: (i // bpt, i % bpt, 0)
        mod_map = (lambda i, j: (i // bpt, 0, 0)) if per_batch else (lambda i, j: (0, 0, 0))
        out_spec = pl.BlockSpec((bb, tt, d), x_map)
        out_shape = jax.ShapeDtypeStruct((bsz, seq, d), F32)
        n_blocks = (bsz // bb) * bpt
    mb = bb if per_batch else 1
    tm = bb * tt
    kern = functools.partial(_ffn_kernel, n_f=n_f, time_major=time_major)
    return pl.pallas_call(
        kern,
        grid=(n_blocks, n_f),
        in_specs=[pl.BlockSpec((bb, tt, d), x_map),
                  pl.BlockSpec((1, d), lambda i, j: (0, 0)),
                  pl.BlockSpec((mb, 1, d), mod_map),
                  pl.BlockSpec((mb, 1, d), mod_map),
                  pl.BlockSpec((mb, 1, d), mod_map),
                  pl.BlockSpec((d, tf), lambda i, j: (0, j)),
                  pl.BlockSpec((d, tf), lambda i, j: (0, n_f + j)),
                  pl.BlockSpec((tf, d), lambda i, j: (j, 0))],
        out_specs=out_spec,
        out_shape=out_shape,
        scratch_shapes=[pltpu.VMEM((tm, d), BF16), pltpu.VMEM((tm, d), F32),
                        pltpu.VMEM((d // LANES, _tr_rows(bb, tt) if time_major else 8, LANES), F32)],
        compiler_params=_params("arbitrary", "arbitrary"),
        name="dense_swiglu",
    )(x3, g.reshape(1, d), shift, scale, gate, w_in, w_in, w_out)


def _s5_disc_kernel(lre_ref, lim_ref, step_ref, bre_ref, bim_ref, ar_ref, ai_ref, bbr_ref, bbi_ref):
    step = jnp.exp(step_ref[...])
    lre, lim = lre_ref[...], lim_ref[...]
    mag = jnp.exp(lre * step)
    ar = mag * jnp.cos(lim * step)
    ai = mag * jnp.sin(lim * step)
    den = lre * lre + lim * lim
    cr = ((ar - 1.0) * lre + ai * lim) / den
    ci = (ai * lre - (ar - 1.0) * lim) / den
    ar_ref[...] = ar
    ai_ref[...] = ai
    bbr_ref[...] = cr * bre_ref[...] - ci * bim_ref[...]
    bbi_ref[...] = cr * bim_ref[...] + ci * bre_ref[...]


def _s5_discretize(lam_re, lam_im, log_step, b_re, b_im):
    two, g, p = lam_re.shape
    c = b_re.shape[-1]
    lre = jnp.repeat(lam_re, c, axis=-1)
    lim = jnp.repeat(lam_im, c, axis=-1)
    shp = jax.ShapeDtypeStruct((two, g, p * c), F32)
    ar, ai, bbr, bbi = pl.pallas_call(
        _s5_disc_kernel, out_shape=[shp, shp, shp, shp], name="s5_discretize",
    )(lre, lim, log_step.reshape(two, g, 1), b_re.reshape(two, g, p * c), b_im.reshape(two, g, p * c))
    ar = ar.reshape(two, g, p, c)[..., 0]
    ai = ai.reshape(two, g, p, c)[..., 0]
    return ar, ai, bbr.reshape(two, g, p, c), bbi.reshape(two, g, p, c)


def _s5_kernel(x_ref, g_ref, sh_ref, sc_ref, wb_ref, wc_ref, a_ref, s0_ref, *rest, n_chunks, steps, bsz, n_cast):
    cast_in, (y_ref, sfin_ref), cast_out = rest[:n_cast], rest[n_cast:n_cast + 2], rest[n_cast + 2:2 * n_cast + 2]
    st_scr, bu0_scr, bu1_scr, sb0_scr, sb1_scr = rest[2 * n_cast + 2:]
    for src, dst in zip(cast_in, cast_out):
        dst[...] = src[...].astype(dst.dtype)
    bu_bufs = (bu0_scr, bu1_scr)
    sb_bufs = (sb0_scr, sb1_scr)
    d = pl.program_id(0)
    k = pl.program_id(1)
    rows, dm = x_ref.shape
    gl = S5_GB * S5_STATE
    cb = S5_GB * S5_CH

    @pl.when(k == 0)
    def _():
        st_scr[...] = s0_ref[0]

    hn = _rms(x_ref[...], g_ref[...]).reshape(steps, bsz, dm)
    h = hn * (1.0 + sc_ref[...])[None] + sh_ref[...][None]
    u = h.reshape(rows, dm).astype(BF16)

    n_gb = dm // cb

    def bu_dot(gb):
        return jnp.dot(u[:, gb * cb:(gb + 1) * cb], wb_ref[0, gb], preferred_element_type=F32)

    def sweep(reverse):
        bu_bufs[0][...] = bu_dot(0)
        for gb in range(n_gb):
            bu_cur, bu_nxt = bu_bufs[gb % 2], bu_bufs[1 - gb % 2]
            sb_cur = sb_bufs[gb % 2]
            if gb + 1 < n_gb:
                bu_nxt[...] = bu_dot(gb + 1)
            sl = slice(gb * gl, (gb + 1) * gl)
            ar = jnp.broadcast_to(a_ref[0, 0:1, sl], (bsz, gl))
            ai = jnp.broadcast_to(a_ref[0, 1:2, sl], (bsz, gl))
            sr, si = st_scr[0, :, sl], st_scr[1, :, sl]
            for i in range(steps):
                r0 = (steps - 1 - i if reverse else i) * bsz
                bur = bu_cur[r0:r0 + bsz, 0:gl]
                bui = bu_cur[r0:r0 + bsz, gl:2 * gl]
                sr, si = ar * sr - ai * si + bur, ar * si + ai * sr + bui
                sb_cur[r0:r0 + bsz, 0:gl] = sr.astype(BF16)
                sb_cur[r0:r0 + bsz, gl:2 * gl] = si.astype(BF16)
            st_scr[0, :, sl] = sr
            st_scr[1, :, sl] = si
            yv = jnp.dot(sb_cur[...], wc_ref[0, gb], preferred_element_type=F32)
            y_ref[0, :, gb * cb:(gb + 1) * cb] = yv.astype(y_ref.dtype)

    @pl.when(d == 0)
    def _():
        sweep(False)

    @pl.when(d == 1)
    def _():
        sweep(True)

    @pl.when(k == n_chunks - 1)
    def _():
        sfin_ref[0] = st_scr[...]


def _s5_scan(xt, g, shift, scale, wb, wc, a, s0, *, bsz, cast=()):
    m, d = xt.shape
    seq = m // bsz
    steps = min(64, seq)
    rows = steps * bsz
    nc = seq // steps
    nstate = s0.shape[-1]
    n_gb = d // (S5_GB * S5_CH)
    gl = S5_GB * S5_STATE

    def chunk(dd, k):
        return k + dd * (nc - 1 - 2 * k)

    n_steps = 2 * nc
    cast_specs, cast_shapes = [], []
    for w in cast:
        slab = w.shape[0] // n_steps
        assert slab * n_steps == w.shape[0] and slab % 16 == 0
        cast_specs.append(pl.BlockSpec((slab, w.shape[1]), lambda dd, k: (dd * nc + k, 0)))
        cast_shapes.append(jax.ShapeDtypeStruct(w.shape, BF16))

    kern = functools.partial(_s5_kernel, n_chunks=nc, steps=steps, bsz=bsz, n_cast=len(cast))
    return pl.pallas_call(
        kern,
        grid=(2, nc),
        in_specs=[pl.BlockSpec((rows, d), lambda dd, k: (chunk(dd, k), 0)),
                  pl.BlockSpec((1, d), lambda dd, k: (0, 0)),
                  pl.BlockSpec((bsz, d), lambda dd, k: (0, 0)),
                  pl.BlockSpec((bsz, d), lambda dd, k: (0, 0)),
                  pl.BlockSpec((1, n_gb, S5_GB * S5_CH, 2 * gl), lambda dd, k: (dd, 0, 0, 0),
                               pipeline_mode=pl.Buffered(1)),
                  pl.BlockSpec((1, n_gb, 2 * gl, S5_GB * S5_CH), lambda dd, k: (dd, 0, 0, 0),
                               pipeline_mode=pl.Buffered(1)),
                  pl.BlockSpec((1, 2, nstate), lambda dd, k: (dd, 0, 0)),
                  pl.BlockSpec((1, 2, bsz, nstate), lambda dd, k: (dd, 0, 0, 0))] + cast_specs,
        out_specs=[pl.BlockSpec((1, rows, d), lambda dd, k: (dd, chunk(dd, k), 0)),
                   pl.BlockSpec((1, 2, bsz, nstate), lambda dd, k: (dd, 0, 0, 0))] + cast_specs,
        out_shape=[jax.ShapeDtypeStruct((2, m, d), BF16),
                   jax.ShapeDtypeStruct((2, 2, bsz, nstate), F32)] + cast_shapes,
        scratch_shapes=[pltpu.VMEM((2, bsz, nstate), F32),
                        pltpu.VMEM((rows, 2 * gl), F32), pltpu.VMEM((rows, 2 * gl), F32),
                        pltpu.VMEM((rows, 2 * gl), BF16), pltpu.VMEM((rows, 2 * gl), BF16)],
        compiler_params=_params("arbitrary", "arbitrary"),
        name="s5_scan",
    )(xt, g.reshape(1, d), shift, scale, wb, wc, a, s0, *cast)


def _gelu_tanh(v):
    return 0.5 * v * (1.0 + jnp.tanh(0.7978845608028654 * (v + 0.044715 * (v * v * v))))


def _s5_glu_kernel(x_ref, y_ref, g_ref, sh_ref, sc_ref, gate_ref, dsk_ref, w_ref, b_ref, o_ref, *, bsz):
    rows, dm = x_ref.shape
    x = x_ref[...]
    hn = _rms(x, g_ref[...]).reshape(rows // bsz, bsz, dm)
    h = (hn * (1.0 + sc_ref[...])[None] + sh_ref[...][None]).reshape(rows, dm)
    yv = dsk_ref[...] * h + y_ref[0].astype(F32) + y_ref[1].astype(F32)
    ge = _gelu_tanh(yv).astype(BF16)
    o = jnp.dot(ge, w_ref[...], preferred_element_type=F32) + b_ref[...]
    out = o[:, :dm] * jax.nn.sigmoid(o[:, dm:])
    res = x.reshape(rows // bsz, bsz, dm) + gate_ref[...][None] * out.reshape(rows // bsz, bsz, dm)
    o_ref[...] = res.reshape(rows, dm)


def _s5_glu(xt, y, g, shift, scale, gate, d_skip, w_glu, b_glu, *, bsz):
    m, d = xt.shape
    tm = min(512, m)
    kern = functools.partial(_s5_glu_kernel, bsz=bsz)
    full = lambda i: (0, 0)
    return pl.pallas_call(
        kern,
        grid=(m // tm,),
        in_specs=[pl.BlockSpec((tm, d), lambda i: (i, 0)),
                  pl.BlockSpec((2, tm, d), lambda i: (0, i, 0)),
                  pl.BlockSpec((1, d), full),
                  pl.BlockSpec((bsz, d), full),
                  pl.BlockSpec((bsz, d), full),
                  pl.BlockSpec((bsz, d), full),
                  pl.BlockSpec((1, d), full),
                  pl.BlockSpec((d, 2 * d), full),
                  pl.BlockSpec((1, 2 * d), full)],
        out_specs=pl.BlockSpec((tm, d), lambda i: (i, 0)),
        out_shape=jax.ShapeDtypeStruct((m, d), F32),
        compiler_params=_params("arbitrary"),
        name="s5_glu",
    )(xt, y, g.reshape(1, d), shift, scale, gate, d_skip.reshape(1, d), w_glu, b_glu.reshape(1, 2 * d))


MOE_TB = 512
MOE_TILE = 512
MOE_WIN = 256
MOE_PAD = 16
MOE_LANES = 128


def _moe_cap(n_tok, n_blocks):
    rows = n_tok + n_blocks * MOE_PAD + MOE_TILE
    return -(-rows // MOE_TILE) * MOE_TILE


def _moe_route_kernel(x_ref, g_ref, sh_ref, sc_ref, rwT_ref, rbT_ref, ut_ref,
                      sorted_ref, meta_ref, tab_ref,
                      buf_scr, zero_scr, run_smem, sems, *, bsz, n_exp, n_blocks, local_rows):
    b = pl.program_id(0)
    slot = lax.rem(b, 2)
    dm = x_ref.shape[-1]
    rows = x_ref.shape[0] * x_ref.shape[1] * x_ref.shape[2]

    @pl.when(b == 0)
    def _():
        for e in range(n_exp):
            run_smem[e] = 0
        zero_scr[...] = jnp.zeros_like(zero_scr)

    hn = _rms(x_ref[...].reshape(rows // bsz, bsz, dm), g_ref[...])
    h = (hn * (1.0 + sc_ref[...])[None] + sh_ref[...][None]).reshape(rows, dm)
    hb = h.astype(BF16)

    logits = _nt_dot(rwT_ref[...], h, precision=HIGHEST) + rbT_ref[...]
    sub = lax.broadcasted_iota(jnp.int32, logits.shape, 0)
    m1 = jnp.max(logits, axis=0, keepdims=True)
    i1 = jnp.min(jnp.where(logits == m1, sub, n_exp), axis=0, keepdims=True)
    rest = jnp.where(sub == i1, -jnp.inf, logits)
    m2 = jnp.max(rest, axis=0, keepdims=True)
    i2 = jnp.min(jnp.where(rest == m2, sub, n_exp), axis=0, keepdims=True)
    e2 = jnp.exp(m2 - m1)
    g1 = 1.0 / (1.0 + e2)
    g2 = e2 * g1
    sel1 = sub == i1
    sel2 = sub == i2
    oh = jnp.where(sel1, 1.0, 0.0) + jnp.where(sel2, 1.0, 0.0)
    cnt = jnp.sum(oh, axis=1, keepdims=True).astype(jnp.int32)
    cpad_v = ((cnt + (MOE_PAD - 1)) // MOE_PAD) * MOE_PAD
    rank = jnp.dot(oh.astype(BF16), ut_ref[...], preferred_element_type=F32)

    cpad, loc, run = [], [], []
    off = 0
    for e in range(n_exp):
        cpad.append(cpad_v[e, 0])
        loc.append(off)
        off = off + cpad[e]
        run.append(run_smem[e])
    sub1 = lax.broadcasted_iota(jnp.int32, (n_exp, 1), 0)
    loc_v = jnp.zeros((n_exp, 1), jnp.int32)
    run_v = jnp.zeros((n_exp, 1), jnp.int32)
    for e in range(n_exp):
        loc_v = jnp.where(sub1 == e, loc[e], loc_v)
        run_v = jnp.where(sub1 == e, run[e], run_v)
    loc_v = loc_v.astype(F32)
    run_v = run_v.astype(F32)

    def pick(sel, v):
        return jnp.sum(jnp.where(sel, v, 0.0), axis=0, keepdims=True)

    lp1 = pick(sel1, loc_v + rank).astype(jnp.int32)
    lp2 = pick(sel2, loc_v + rank).astype(jnp.int32)
    pos1 = pick(sel1, run_v + rank)
    pos2 = pick(sel2, run_v + rank)

    r = lax.broadcasted_iota(jnp.int32, (local_rows, rows), 0)
    onehot = jnp.where(r == lp1, 1.0, jnp.where(r == lp2, 1.0, 0.0)).astype(BF16)
    buf_scr[slot, 0:local_rows, :] = jnp.dot(onehot, hb, preferred_element_type=F32).astype(BF16)
    buf_scr[slot, local_rows:local_rows + MOE_TILE, :] = jnp.zeros((MOE_TILE, dm), BF16)

    rec = jnp.concatenate([i1.astype(F32), i2.astype(F32), pos1, pos2, g1, g2,
                           jnp.zeros((MOE_LANES - 6, rows), F32)], axis=0)
    meta_ref[...] = rec.T

    base = b * (2 * n_exp)
    for e in range(n_exp):
        tab_ref[base + e] = run[e]
        tab_ref[base + n_exp + e] = cpad[e]
        run_smem[e] = run[e] + cpad[e]

    def group_copy(e, slot_, loc_e, run_e):
        src = buf_scr.at[slot_, pl.ds(pl.multiple_of(loc_e, MOE_PAD), MOE_TILE)]
        dst = sorted_ref.at[e, pl.ds(pl.multiple_of(run_e, MOE_PAD), MOE_TILE)]
        return pltpu.make_async_copy(src, dst, sems.at[slot_, e])

    @pl.when(b > 0)
    def _():
        for e in range(n_exp):
            group_copy(e, 1 - slot, 0, 0).wait()

    for e in range(n_exp):
        group_copy(e, slot, loc[e], run[e]).start()

    @pl.when(b == n_blocks - 1)
    def _():
        fin = (n_blocks) * (2 * n_exp)
        for e in range(n_exp):
            group_copy(e, slot, 0, 0).wait()
        for e in range(n_exp):
            tot = run[e] + cpad[e]
            tab_ref[fin + e] = tot
            tab_ref[fin + n_exp + e] = 0
            dst = sorted_ref.at[e, pl.ds(pl.multiple_of(tot, MOE_PAD), MOE_TILE)]
            pltpu.make_async_copy(zero_scr, dst, sems.at[slot, e]).start()
        for e in range(n_exp):
            dst = sorted_ref.at[e, pl.ds(0, MOE_TILE)]
            pltpu.make_async_copy(zero_scr, dst, sems.at[slot, e]).wait()


def _moe_expert_kernel(te_ref, tr_ref, na_ref, xs_ref, wg_ref, wu_ref, wo_ref, y_ref, acc_scr, *, n_f):
    i = pl.program_id(0)
    f = pl.program_id(1)

    @pl.when(i < na_ref[0])
    def _():
        @pl.when(f == 0)
        def _():
            acc_scr[...] = jnp.zeros_like(acc_scr)

        xs = xs_ref[0]
        gv = jnp.dot(xs, wg_ref[0], preferred_element_type=F32)
        uv = jnp.dot(xs, wu_ref[0], preferred_element_type=F32)
        a = (_silu(gv) * uv).astype(BF16)
        acc_scr[...] += jnp.dot(a, wo_ref[0], preferred_element_type=F32)

        @pl.when(f == n_f - 1)
        def _():
            y_ref[0] = acc_scr[...].astype(y_ref.dtype)


def _moe_combine_kernel(tab_ref, end_ref, x_ref, meta_ref, gate_ref, gf_ref, ys_ref, o_ref,
                        win_scr, acc_scr, tr_scr, sems, *, bsz, n_exp):
    b = pl.program_id(0)
    slot = lax.rem(b, 2)
    n_cols, n_rows, _, dm = x_ref.shape
    rows = n_cols * n_rows * bsz
    n_win = MOE_TILE // MOE_WIN

    def window(blk, slot_, e, c):
        base = blk * (2 * n_exp)
        run_e = tab_ref[base + e]
        live = tab_ref[base + n_exp + e] > c * MOE_WIN
        ws = jnp.minimum(run_e + c * MOE_WIN, end_ref[e] - MOE_WIN)
        k = e * n_win + c
        cp = pltpu.make_async_copy(ys_ref.at[e, pl.ds(pl.multiple_of(ws, MOE_PAD), MOE_WIN)],
                                   win_scr.at[slot_, k], sems.at[slot_, k])
        return live, ws, k, cp

    def fetch(blk, slot_):
        for e in range(n_exp):
            for c in range(n_win):
                live, _, _, cp = window(blk, slot_, e, c)

                @pl.when(live)
                def _():
                    cp.start()

    @pl.when(b == 0)
    def _():
        fetch(0, 0)

    fetch(b + 1, 1 - slot)

    meta = meta_ref[...]
    e1, e2 = meta[:, 0:1], meta[:, 1:2]
    pos1, pos2 = meta[:, 2:3].astype(jnp.int32), meta[:, 3:4].astype(jnp.int32)
    g1, g2 = meta[:, 4:5], meta[:, 5:6]
    acc_scr[...] = jnp.zeros_like(acc_scr)
    lane = lax.broadcasted_iota(jnp.int32, (rows, MOE_WIN), 1)

    for e in range(n_exp):
        for c in range(n_win):
            live, ws, k, cp = window(b, slot, e, c)

            @pl.when(live)
            def _():
                cp.wait()
                rel = jnp.where(e1 == e, pos1, pos2) - ws
                hit = jnp.where(e1 == e, 1.0, jnp.where(e2 == e, 1.0, 0.0))
                pe = jnp.where(rel == lane, hit, 0.0).astype(BF16)
                ge = jnp.where(e1 == e, g1, jnp.where(e2 == e, g2, 0.0))
                acc_scr[...] += ge * jnp.dot(pe, win_scr[slot, k], preferred_element_type=F32)

    upd = gate_ref[...][None] * acc_scr[...].reshape(rows // bsz, bsz, dm)
    xo = x_ref[...].reshape(rows, dm) + upd.reshape(rows, dm)
    _fill_lane_tiles(_rms(xo, gf_ref[...]), tr_scr, n_rows * bsz)
    for rl in range(n_rows):
        _strided_rows_out(
            tr_scr, lambda bb, lt: o_ref.at[bb, rl, :, lt * LANES:(lt + 1) * LANES],
            n_groups=bsz, group_rows=n_cols, stride=n_rows * bsz + TR_SKEW, base=rl * bsz)


def _moe(xt, g, shift, scale, gate, router_w, router_b, w_in, w_out, g_final, *, bsz, grid_rows):
    m, d = xt.shape
    n_exp, f, _ = w_out.shape
    nb = m // MOE_TB
    cap = _moe_cap(m, nb)
    local_rows = -(-(TOP_K * MOE_TB + n_exp * (MOE_PAD - 1)) // 128) * 128
    full1 = lambda i: (0, 0)

    blk_cols = 8
    blk_rows = MOE_TB // (blk_cols * bsz)
    assert blk_rows * blk_cols * bsz == MOE_TB and grid_rows % blk_rows == 0 and GRID_W % blk_cols == 0
    nrg = grid_rows // blk_rows
    x4 = xt.reshape(GRID_W, grid_rows, bsz, d)
    x_spec1 = pl.BlockSpec((blk_cols, blk_rows, bsz, d), lambda i: (i // nrg, i % nrg, 0, 0))

    ut = (jnp.arange(MOE_TB)[:, None] < jnp.arange(MOE_TB)[None, :]).astype(BF16)
    route = functools.partial(_moe_route_kernel, bsz=bsz, n_exp=n_exp, n_blocks=nb, local_rows=local_rows)
    sorted_h, meta, tab = pl.pallas_call(
        route,
        grid=(nb,),
        in_specs=[x_spec1,
                  pl.BlockSpec((1, d), full1),
                  pl.BlockSpec((bsz, d), full1),
                  pl.BlockSpec((bsz, d), full1),
                  pl.BlockSpec((n_exp, d), full1),
                  pl.BlockSpec((n_exp, 1), full1),
                  pl.BlockSpec((MOE_TB, MOE_TB), full1)],
        out_specs=[pl.BlockSpec(memory_space=pl.ANY),
                   pl.BlockSpec((MOE_TB, MOE_LANES), lambda i: (i, 0)),
                   pl.BlockSpec(memory_space=pltpu.SMEM)],
        out_shape=[jax.ShapeDtypeStruct((n_exp, cap, d), BF16),
                   jax.ShapeDtypeStruct((m, MOE_LANES), F32),
                   jax.ShapeDtypeStruct(((nb + 1) * 2 * n_exp,), jnp.int32)],
        scratch_shapes=[pltpu.VMEM((2, local_rows + MOE_TILE, d), BF16),
                        pltpu.VMEM((MOE_TILE, d), BF16),
                        pltpu.SMEM((n_exp,), jnp.int32),
                        pltpu.SemaphoreType.DMA((2, n_exp))],
        compiler_params=_params("arbitrary"),
        name="moe_route",
    )(x4, g.reshape(1, d), shift, scale, router_w.T, router_b.reshape(n_exp, 1), ut)

    tot = tab[nb * 2 * n_exp: nb * 2 * n_exp + n_exp]
    ntile = jnp.maximum((tot + MOE_TILE - 1) // MOE_TILE, 1)
    cum = jnp.cumsum(ntile)
    n_active = cum[-1:]
    max_tiles = (TOP_K * m + nb * n_exp * (MOE_PAD - 1)) // MOE_TILE + n_exp + 1
    idc = jnp.minimum(jnp.arange(max_tiles, dtype=jnp.int32), n_active - 1)
    tile_e = jnp.sum(idc[:, None] >= cum[None, :], axis=1).astype(jnp.int32)
    tile_r = (idc - (cum - ntile)[tile_e]).astype(jnp.int32)
    end = (ntile * MOE_TILE).astype(jnp.int32)

    tf = f // 2 if (f // 2) % 128 == 0 else f
    n_f = f // tf

    def f_idx(i, j, na):
        return jnp.where(i < na[0], j, n_f - 1)

    ys = pl.pallas_call(
        functools.partial(_moe_expert_kernel, n_f=n_f),
        grid_spec=pltpu.PrefetchScalarGridSpec(
            num_scalar_prefetch=3,
            grid=(max_tiles, n_f),
            in_specs=[pl.BlockSpec((1, MOE_TILE, d), lambda i, j, te, tr, na: (te[i], tr[i], 0)),
                      pl.BlockSpec((1, d, tf), lambda i, j, te, tr, na: (te[i], 0, f_idx(i, j, na))),
                      pl.BlockSpec((1, d, tf), lambda i, j, te, tr, na: (te[i], 0, n_f + f_idx(i, j, na))),
                      pl.BlockSpec((1, tf, d), lambda i, j, te, tr, na: (te[i], f_idx(i, j, na), 0))],
            out_specs=pl.BlockSpec((1, MOE_TILE, d), lambda i, j, te, tr, na: (te[i], tr[i], 0)),
            scratch_shapes=[pltpu.VMEM((MOE_TILE, d), F32)]),
        out_shape=jax.ShapeDtypeStruct((n_exp, cap, d), BF16),
        compiler_params=_params("arbitrary", "arbitrary"),
        name="moe_experts",
    )(tile_e, tile_r, n_active.astype(jnp.int32), sorted_h, w_in, w_in, w_out)

    n_win = MOE_TILE // MOE_WIN
    out = pl.pallas_call(
        functools.partial(_moe_combine_kernel, bsz=bsz, n_exp=n_exp),
        grid_spec=pltpu.PrefetchScalarGridSpec(
            num_scalar_prefetch=2,
            grid=(nb,),
            in_specs=[pl.BlockSpec((blk_cols, blk_rows, bsz, d), lambda i, tb, en: (i // nrg, i % nrg, 0, 0)),
                      pl.BlockSpec((MOE_TB, MOE_LANES), lambda i, tb, en: (i, 0)),
                      pl.BlockSpec((bsz, d), lambda i, tb, en: (0, 0)),
                      pl.BlockSpec((1, d), lambda i, tb, en: (0, 0)),
                      pl.BlockSpec(memory_space=pl.ANY)],
            out_specs=pl.BlockSpec((bsz, blk_rows, blk_cols, d), lambda i, tb, en: (0, i % nrg, i // nrg, 0)),
            scratch_shapes=[pltpu.VMEM((2, n_exp * n_win, MOE_WIN, d), BF16),
                            pltpu.VMEM((MOE_TB, d), F32),
                            pltpu.VMEM((d // LANES, _tr_rows(blk_cols, blk_rows * bsz), LANES), F32),
                            pltpu.SemaphoreType.DMA((2, n_exp * n_win))]),
        out_shape=jax.ShapeDtypeStruct((bsz, grid_rows, GRID_W, d), F32),
        compiler_params=_params("arbitrary"),
        name="moe_combine",
    )(tab, end, x4, meta, gate, g_final.reshape(1, d), ys)
    return out.reshape(bsz, grid_rows * GRID_W, d)


def _blockdiag(w, n_gb):
    two, g, r, c = w.shape
    eye = jnp.eye(S5_GB, dtype=w.dtype)
    out = jnp.einsum("dbgrc,gh->dbgrhc", w.reshape(two, n_gb, S5_GB, r, c), eye)
    return out.reshape(two, n_gb, S5_GB * r, S5_GB * c)


def kernel(x, c, ctx, c_ctx, ada_w, ada_b, norm_mix, norm_ffn, ssd_w_in, ssd_conv_w, ssd_conv_b, ssd_dt_bias, ssd_a_log, ssd_d, ssd_norm, ssd_w_out, s5_lam_re, s5_lam_im, s5_log_step, s5_b_re, s5_b_im, s5_c_re, s5_c_im, s5_d, s5_w_glu, s5_b_glu, ffn_w_in, ffn_w_out, moe_router_w, moe_router_b, moe_w_in, moe_w_out, norm_final):
    bsz, seq, d = x.shape
    ctx_len = ctx.shape[1]
    depth = ada_w.shape[0]
    assert depth == 2, "one SSD layer followed by one S5 layer"
    rows = seq // GRID_W

    pad_rows = (-(bsz + 1)) % 8
    cc = jnp.concatenate([c, c_ctx[None], jnp.zeros((pad_rows, d), F32)], axis=0)
    mods = _ada(cc, ada_w, ada_b)

    def mod(i, n):
        sl = slice(n * d, (n + 1) * d)
        return mods[i, :bsz, sl], mods[i, bsz:bsz + 1, sl]

    heads = ssd_d.shape[1]
    d_inner = heads * SSD_HEADDIM
    conv_dim = ssd_conv_w.shape[2]
    n_main = d_inner + conv_dim
    w_in0 = ssd_w_in[0]
    w_main = w_in0[:, :n_main].astype(BF16)
    w_dt = w_in0[:, n_main:]
    conv_w_full = jnp.concatenate([jnp.zeros((SSD_CONV, d_inner), F32), 0.5 * ssd_conv_w[0]], axis=1)
    conv_b_full = jnp.concatenate([jnp.zeros((d_inner,), F32), 0.5 * ssd_conv_b[0]])
    d_exp = jnp.repeat(ssd_d[0], SSD_HEADDIM)
    w_out0 = ssd_w_out[0].astype(BF16)
    gw = (heads // SSD_GROUPS) * SSD_HEADDIM

    (sh_x, sh_c), (sc_x, sc_c), (gt_x, gt_c) = mod(0, 0), mod(0, 1), mod(0, 2)
    r3 = lambda a: a[:, None, :]
    dt_bias = ssd_dt_bias[0].reshape(-1)
    a_log = ssd_a_log[0].reshape(-1)

    def ssd_mixer(tok3, shift, scale, gate, h0):
        b_, t_, _ = tok3.shape
        zx, dtc, dtcT = _ssd_inproj(tok3, norm_mix[0], r3(shift), r3(scale), w_main, conv_w_full,
                                    conv_b_full, w_dt, dt_bias, a_log, d_inner=d_inner, heads=heads)
        y, hfin = _ssd_scan(zx, dtc, dtcT, h0, d_inner=d_inner, heads=heads)
        out = _ssd_out(y.reshape(2, b_ * t_, d_inner), zx.reshape(b_ * t_, n_main), d_exp, ssd_norm[0],
                       w_out0, tok3.reshape(b_ * t_, d), r3(gate), seq=t_, d_inner=d_inner)
        return out, hfin

    h_zero = jnp.zeros((bsz, 2, SSD_GROUPS, SSD_STATE, gw), F32)
    ctx2, h_ctx = ssd_mixer(ctx, sh_c, sc_c, gt_c, h_zero)
    x2, _ = ssd_mixer(x, sh_x, sc_x, gt_x, h_ctx)

    ffn_in = ffn_w_in[0].astype(BF16)
    ffn_out = ffn_w_out[0].astype(BF16)
    (sh_x, sh_c), (sc_x, sc_c), (gt_x, gt_c) = mod(0, 3), mod(0, 4), mod(0, 5)
    xt = _ffn(x2.reshape(bsz, seq, d), norm_ffn[0], r3(sh_x), r3(sc_x), r3(gt_x), ffn_in, ffn_out,
              time_major=True).reshape(seq * bsz, d)
    ctx2 = _ffn(ctx2.reshape(bsz, ctx_len, d), norm_ffn[0], r3(sh_c), r3(sc_c), r3(gt_c), ffn_in, ffn_out)

    ct = ctx2.reshape(bsz, ctx_len, d).transpose(1, 0, 2).reshape(ctx_len * bsz, d)

    ar, ai, bbr, bbi = _s5_discretize(s5_lam_re[0], s5_lam_im[0], s5_log_step[0], s5_b_re[0], s5_b_im[0])
    n_groups = ar.shape[1]
    n_gb = n_groups // S5_GB
    wb = jnp.concatenate([_blockdiag(bbr.transpose(0, 1, 3, 2), n_gb),
                          _blockdiag(bbi.transpose(0, 1, 3, 2), n_gb)], axis=-1).astype(BF16)
    wc = jnp.concatenate([_blockdiag(s5_c_re[0].transpose(0, 1, 3, 2), n_gb),
                          _blockdiag(-s5_c_im[0].transpose(0, 1, 3, 2), n_gb)], axis=2).astype(BF16)
    a_coef = jnp.stack([ar.reshape(2, -1), ai.reshape(2, -1)], axis=1)
    nstate = a_coef.shape[-1]

    (sh_x, sh_c), (sc_x, sc_c), (gt_x, _) = mod(1, 0), mod(1, 1), mod(1, 2)
    bc = lambda a: jnp.broadcast_to(a, (bsz, d))
    s_zero = jnp.zeros((2, 2, bsz, nstate), F32)
    _, s_ctx = _s5_scan(ct, norm_mix[1], bc(sh_c), bc(sc_c), wb, wc, a_coef, s_zero, bsz=bsz)
    n_exp, _, two_f = moe_w_in.shape[1:]
    y, _, w_in_b, w_out_b = _s5_scan(xt, norm_mix[1], sh_x, sc_x, wb, wc, a_coef, s_ctx, bsz=bsz,
                                     cast=(moe_w_in[0].reshape(n_exp * d, two_f),
                                           moe_w_out[0].reshape(n_exp * (two_f // 2), d)))
    w_in_b = w_in_b.reshape(n_exp, d, two_f)
    w_out_b = w_out_b.reshape(n_exp, two_f // 2, d)
    xt = _s5_glu(xt, y, norm_mix[1], sh_x, sc_x, gt_x, s5_d[0], s5_w_glu[0].astype(BF16), s5_b_glu[0], bsz=bsz)

    (sh_x, _), (sc_x, _), (gt_x, _) = mod(1, 3), mod(1, 4), mod(1, 5)
    return _moe(xt, norm_ffn[1], sh_x, sc_x, gt_x, moe_router_w[0], moe_router_b[0],
                w_in_b, w_out_b, norm_final, bsz=bsz, grid_rows=rows)
```
